```python
import math
import jax, jax.numpy as jnp
from jax import lax
import numpy as np

D_MODEL = 4096
BATCH = 2
SEQ = 4096
DEPTH = 2

CHUNK = 64
LEFT_CHUNKS = 8
MIX_WIDTH = D_MODEL
ATTN_WIDTH = MIX_WIDTH // 2
ATTN_HEAD_DIM = 128
N_ATTN_HEADS = ATTN_WIDTH // ATTN_HEAD_DIM
REL_CLIP = 128
LRU_WIDTH = MIX_WIDTH - ATTN_WIDTH
LRU_BLOCKS = 16
LRU_BLOCK_WIDTH = LRU_WIDTH // LRU_BLOCKS
CONV_WIDTH = 4
LRU_C = 8.0
N_EXPERTS = 32
TOP_K = 4
EXPERT_FF = D_MODEL // 4
SWIGLU_LIMIT = 7.0
SWIGLU_ALPHA = 1.702
MOE_BLOCK = 128
EPS = 1e-6

kernel_name = "hymba_style_streaming_attn_rglru_moe"


def rmsnorm(x, gain):
    x32 = x.astype(jnp.float32)
    inv = lax.rsqrt(jnp.mean(x32 * x32, axis=-1, keepdims=True) + EPS)
    return (x32 * inv * gain.astype(jnp.float32)).astype(x.dtype)


def chunk_band_attention(q, k, v, rel_bias):
    B, S, H, Dh = q.shape
    n_chunks = S // CHUNK
    pad = LEFT_CHUNKS * CHUNK
    band = pad + CHUNK
    kp = jnp.pad(k, ((0, 0), (pad, 0), (0, 0), (0, 0)))
    vp = jnp.pad(v, ((0, 0), (pad, 0), (0, 0), (0, 0)))
    qi = jnp.arange(CHUNK)[:, None] + pad
    kj = jnp.arange(band)[None, :]
    rel_idx = jnp.clip(qi - kj, -REL_CLIP, REL_CLIP) + REL_CLIP
    bias = rel_bias.astype(jnp.float32)[:, rel_idx]
    scale = 1.0 / math.sqrt(Dh)
    neg = jnp.finfo(jnp.float32).min

    def one_chunk(c):
        qb = lax.dynamic_slice_in_dim(q, c * CHUNK, CHUNK, axis=1)
        kb = lax.dynamic_slice_in_dim(kp, c * CHUNK, band, axis=1)
        vb = lax.dynamic_slice_in_dim(vp, c * CHUNK, band, axis=1)
        s = jnp.einsum('bqhd,bkhd->bhqk', qb, kb).astype(jnp.float32) * scale + bias
        valid = (c * CHUNK - pad + jnp.arange(band)) >= 0
        s = jnp.where(valid[None, None, None, :], s, neg)
        p = jax.nn.softmax(s, axis=-1).astype(v.dtype)
        return jnp.einsum('bhqk,bkhd->bqhd', p, vb)

    out = lax.map(one_chunk, jnp.arange(n_chunks))
    return out.transpose(1, 0, 2, 3, 4).reshape(B, S, H * Dh)


def causal_depthwise_conv(x, w, b):
    W = x.shape[-1]
    y = lax.conv_general_dilated(
        x, w[:, None, :].astype(x.dtype), window_strides=(1,),
        padding=[(CONV_WIDTH - 1, 0)],
        dimension_numbers=('NWC', 'WIO', 'NWC'),
        feature_group_count=W)
    return y + b.astype(x.dtype)


def rglru(x, w_a, b_a, w_i, b_i, lam):
    B, S, W = x.shape
    xb = x.reshape(B, S, LRU_BLOCKS, LRU_BLOCK_WIDTH)
    r = jax.nn.sigmoid(jnp.einsum('bsnc,ncd->bsnd', xb, w_a).reshape(B, S, W).astype(jnp.float32)
                       + b_a.astype(jnp.float32))
    i = jax.nn.sigmoid(jnp.einsum('bsnc,ncd->bsnd', xb, w_i).reshape(B, S, W).astype(jnp.float32)
                       + b_i.astype(jnp.float32))
    log_a = -LRU_C * r * jax.nn.softplus(-lam.astype(jnp.float32))
    a = jnp.exp(log_a)
    b = jnp.sqrt(-jnp.expm1(2.0 * log_a)) * (i * x.astype(jnp.float32))

    def combine(left, right):
        a_l, b_l = left
        a_r, b_r = right
        return a_l * a_r, a_r * b_l + b_r

    _, h = lax.associative_scan(combine, (a, b), axis=1)
    return h.astype(x.dtype)


def clamped_swiglu(gu):
    glu = jnp.minimum(gu[..., ::2], SWIGLU_LIMIT)
    lin = jnp.clip(gu[..., 1::2], -SWIGLU_LIMIT, SWIGLU_LIMIT)
    return glu * jax.nn.sigmoid(SWIGLU_ALPHA * glu) * (lin + 1.0)


def moe_ffn(h, w_router, b_router, w_gu, b_gu, w_dn, b_dn):
    B, S, D = h.shape
    N = B * S
    t = h.reshape(N, D)
    logits = jnp.dot(t.astype(jnp.float32), w_router.astype(jnp.float32)) + b_router.astype(jnp.float32)
    top_val, top_idx = lax.top_k(logits, TOP_K)
    gates = jax.nn.softmax(top_val, axis=-1).astype(h.dtype)

    n_pairs = N * TOP_K
    flat_e = top_idx.reshape(-1).astype(jnp.int32)
    flat_tok = jnp.arange(n_pairs, dtype=jnp.int32) // TOP_K
    order = jnp.argsort(flat_e, stable=True)
    sorted_e = flat_e[order]
    sorted_tok = flat_tok[order]
    sorted_gate = gates.reshape(-1)[order]

    counts = jnp.bincount(flat_e, length=N_EXPERTS).astype(jnp.int32)
    padded = (counts + MOE_BLOCK - 1) // MOE_BLOCK * MOE_BLOCK
    start = jnp.cumsum(counts) - counts
    pend = jnp.cumsum(padded)
    pstart = pend - padded
    dest = pstart[sorted_e] + (jnp.arange(n_pairs, dtype=jnp.int32) - start[sorted_e])

    R = n_pairs + N_EXPERTS * MOE_BLOCK
    n_blocks = R // MOE_BLOCK
    buf_tok = jnp.full((R,), N, jnp.int32).at[dest].set(sorted_tok)
    buf_gate = jnp.zeros((R,), h.dtype).at[dest].set(sorted_gate)
    block_start = jnp.arange(n_blocks, dtype=jnp.int32) * MOE_BLOCK
    block_e = jnp.minimum(jnp.searchsorted(pend, block_start, side='right'), N_EXPERTS - 1)

    def expert_block(args):
        tok_blk, e = args
        xb = t.at[tok_blk].get(mode='fill', fill_value=0)
        gu = jnp.dot(xb, w_gu[e]) + b_gu[e]
        return jnp.dot(clamped_swiglu(gu), w_dn[e]) + b_dn[e]

    y = lax.map(expert_block, (buf_tok.reshape(n_blocks, MOE_BLOCK), block_e))
    y = y.reshape(R, D) * buf_gate[:, None]
    out = jnp.zeros((N, D), h.dtype).at[buf_tok].add(y, mode='drop')
    return out.reshape(B, S, D)


def setup_inputs(seed: int = 0) -> dict:
    key = jax.random.key(seed)
    ks = jax.random.split(key, 24)
    f32 = jnp.float32
    L, D, E, F = DEPTH, D_MODEL, N_EXPERTS, EXPERT_FF
    n_in = 3 * ATTN_WIDTH + 2 * LRU_WIDTH
    nrm = lambda k, shape, s: jax.random.normal(k, shape, f32) * s
    u = jax.random.uniform(ks[10], (L, LRU_WIDTH), f32, 0.9, 0.999)
    a_base = u ** (1.0 / LRU_C)
    return {
        "x": nrm(ks[0], (BATCH, SEQ, D), 1.0),
        "norm_mix": 1.0 + nrm(ks[1], (L, D), 0.02),
        "w_in": nrm(ks[2], (L, D, n_in), D ** -0.5),
        "attn_rel_bias": nrm(ks[3], (L, N_ATTN_HEADS, 2 * REL_CLIP + 1), 0.1),
        "attn_out_gain": 1.0 + nrm(ks[4], (L, ATTN_WIDTH), 0.02),
        "conv_w": nrm(ks[5], (L, CONV_WIDTH, LRU_WIDTH), CONV_WIDTH ** -0.5),
        "conv_b": nrm(ks[6], (L, LRU_WIDTH), 0.02),
        "lru_w_a": nrm(ks[7], (L, LRU_BLOCKS, LRU_BLOCK_WIDTH, LRU_BLOCK_WIDTH), LRU_BLOCK_WIDTH ** -0.5),
        "lru_b_a": nrm(ks[8], (L, LRU_WIDTH), 0.02),
        "lru_w_i": nrm(ks[9], (L, LRU_BLOCKS, LRU_BLOCK_WIDTH, LRU_BLOCK_WIDTH), LRU_BLOCK_WIDTH ** -0.5),
        "lru_b_i": nrm(ks[11], (L, LRU_WIDTH), 0.02),
        "lru_lambda": jnp.log(a_base / (1.0 - a_base)),
        "rec_out_gain": 1.0 + nrm(ks[12], (L, LRU_WIDTH), 0.02),
        "w_out": nrm(ks[13], (L, MIX_WIDTH, D), MIX_WIDTH ** -0.5),
        "norm_ffn": 1.0 + nrm(ks[14], (L, D), 0.02),
        "router_w": nrm(ks[15], (L, D, E), D ** -0.5),
        "router_b": nrm(ks[16], (L, E), 0.01),
        "expert_w_gate_up": nrm(ks[17], (L, E, D, 2 * F), D ** -0.5),
        "expert_b_gate_up": nrm(ks[18], (L, E, 2 * F), 0.02),
        "expert_w_down": nrm(ks[19], (L, E, F, D), F ** -0.5),
        "expert_b_down": nrm(ks[20], (L, E, D), 0.02),
        "norm_final": 1.0 + nrm(ks[21], (D,), 0.02),
    }


def reference(x, norm_mix, w_in, attn_rel_bias, attn_out_gain, conv_w, conv_b,
              lru_w_a, lru_b_a, lru_w_i, lru_b_i, lru_lambda, rec_out_gain, w_out,
              norm_ffn, router_w, router_b, expert_w_gate_up, expert_b_gate_up,
              expert_w_down, expert_b_down, norm_final):
    B, S, D = x.shape
    split_at = [ATTN_WIDTH, 2 * ATTN_WIDTH, 3 * ATTN_WIDTH, 3 * ATTN_WIDTH + LRU_WIDTH]
    for l in range(DEPTH):
        h = rmsnorm(x, norm_mix[l])
        proj = jnp.einsum('bsd,de->bse', h, w_in[l])
        q, k, v, x_rec, x_gate = jnp.split(proj, split_at, axis=-1)
        hs = (B, S, N_ATTN_HEADS, ATTN_HEAD_DIM)
        attn = chunk_band_attention(q.reshape(hs), k.reshape(hs), v.reshape(hs), attn_rel_bias[l])
        attn = rmsnorm(attn, attn_out_gain[l])
        u = causal_depthwise_conv(x_rec, conv_w[l], conv_b[l])
        rec = rglru(u, lru_w_a[l], lru_b_a[l], lru_w_i[l], lru_b_i[l], lru_lambda[l])
        rec = rmsnorm(rec * jax.nn.gelu(x_gate, approximate=True), rec_out_gain[l])
        mixed = jnp.concatenate([attn, rec], axis=-1)
        x = x + jnp.einsum('bse,ed->bsd', mixed, w_out[l])
        h = rmsnorm(x, norm_ffn[l])
        x = x + moe_ffn(h, router_w[l], router_b[l], expert_w_gate_up[l], expert_b_gate_up[l],
                        expert_w_down[l], expert_b_down[l])
    return rmsnorm(x, norm_final)
```

```python
import functools
import math

import jax
import jax.numpy as jnp
from jax import lax
from jax.experimental import pallas as pl
from jax.experimental.pallas import tpu as pltpu

CHUNK = 64
LEFT_CHUNKS = 8
ATTN_HEAD_DIM = 128
REL_CLIP = 128
LRU_C = 8.0
TOP_K = 4
SWIGLU_LIMIT = 7.0
SWIGLU_ALPHA = 1.702
EPS = 1e-6

V7X_LANES = 128
V7X_SUBLANES = 8
V7X_VMEM_BYTES = 64 * 1024 * 1024

MASK_VALUE = -1e30

f32 = jnp.float32
bf16 = jnp.bfloat16


def _vmem_limit(nbytes):
    return int(min(nbytes + 12 * 1024 * 1024, V7X_VMEM_BYTES - 6 * 1024 * 1024))


def _params(sem, nbytes):
    return pltpu.CompilerParams(dimension_semantics=sem, vmem_limit_bytes=_vmem_limit(nbytes))


def _rmsnorm_kernel(x_ref, g_ref, o_ref):
    x = x_ref[...]
    inv = lax.rsqrt(jnp.mean(x * x, axis=-1, keepdims=True) + EPS)
    o_ref[...] = (x * inv * g_ref[...]).astype(o_ref.dtype)


def rmsnorm_bf16(x2d, gain_l, layer):
    n, d = x2d.shape
    tb = min(256, n)
    return pl.pallas_call(
        _rmsnorm_kernel,
        grid=(n // tb,),
        in_specs=[pl.BlockSpec((tb, d), lambda i: (i, 0)),
                  pl.BlockSpec((None, 1, d), lambda i: (layer, 0, 0))],
        out_specs=pl.BlockSpec((tb, d), lambda i: (i, 0)),
        out_shape=jax.ShapeDtypeStruct((n, d), bf16),
        compiler_params=_params(("arbitrary",), 2 * tb * d * 6),
        name="rmsnorm_bf16",
    )(x2d, gain_l.reshape(gain_l.shape[0], 1, d))


def _matmul_kernel(*refs, n_pairs, has_res):
    a_refs = refs[:n_pairs]
    w_refs = refs[n_pairs:2 * n_pairs]
    pos = 2 * n_pairs
    res_ref = refs[pos] if has_res else None
    pos += int(has_res)
    o_ref = refs[pos]
    wbf_refs = refs[pos + 1:pos + 1 + n_pairs]

    @pl.when(pl.program_id(1) == 0)
    def _cast_weights():
        for w_ref, wbf_ref in zip(w_refs, wbf_refs):
            wbf_ref[...] = w_ref[...].astype(bf16)

    acc = jnp.dot(a_refs[0][...], wbf_refs[0][...], preferred_element_type=f32)
    for a_ref, wbf_ref in zip(a_refs[1:], wbf_refs[1:]):
        acc += jnp.dot(a_ref[...], wbf_ref[...], preferred_element_type=f32)
    if has_res:
        acc += res_ref[...]
    o_ref[...] = acc.astype(o_ref.dtype)


def matmul_ws(a_list, w_stack, layer, k_blocks, col0, n_cols, out_dtype, residual=None, name="matmul_ws"):
    m = a_list[0].shape[0]
    tm = min(1024, m)
    tn = min(512, n_cols)
    assert m % tm == 0 and n_cols % tn == 0 and col0 % tn == 0
    cb0 = col0 // tn
    n_pairs = len(a_list)
    in_specs, scratch = [], []
    nbytes = 2 * tm * tn * 4 * (2 if residual is not None else 1)
    for a in a_list:
        kp = a.shape[1]
        in_specs.append(pl.BlockSpec((tm, kp), lambda j, i: (i, 0)))
        nbytes += 2 * tm * kp * 2
    for a, kb in zip(a_list, k_blocks):
        kp = a.shape[1]
        in_specs.append(pl.BlockSpec((None, kp, tn), functools.partial(lambda j, i, kb: (layer, kb, j + cb0), kb=kb)))
        scratch.append(pltpu.VMEM((kp, tn), bf16))
        nbytes += 2 * kp * tn * 4 + kp * tn * 2
    args = list(a_list) + [w_stack] * n_pairs
    if residual is not None:
        in_specs.append(pl.BlockSpec((tm, tn), lambda j, i: (i, j)))
        args.append(residual)
    return pl.pallas_call(
        functools.partial(_matmul_kernel, n_pairs=n_pairs, has_res=residual is not None),
        grid=(n_cols // tn, m // tm),
        in_specs=in_specs,
        out_specs=pl.BlockSpec((tm, tn), lambda j, i: (i, j)),
        out_shape=jax.ShapeDtypeStruct((m, n_cols), out_dtype),
        scratch_shapes=scratch,
        compiler_params=_params(("arbitrary", "arbitrary"), nbytes),
        name=name,
    )(*args)


def _attn_kernel(q_ref, *refs, n_kb, hb, tq, n_hg):
    k_refs = refs[:n_kb]
    v_refs = refs[n_kb:2 * n_kb]
    bias_ref, gain_ref, o_ref, raw_ref = refs[2 * n_kb:]
    i = pl.program_id(1)
    hg = pl.program_id(2)
    dh = ATTN_HEAD_DIM
    scale = 1.0 / math.sqrt(dh)
    outs = []
    for hh in range(hb):
        sl = slice(hh * dh, (hh + 1) * dh)
        q = q_ref[:, sl]
        bias = bias_ref[hg * hb + hh]
        parts = []
        for j in range(n_kb - 1, -1, -1):
            s = lax.dot_general(q, k_refs[j][:, sl], (((1,), (1,)), ((), ())), preferred_element_type=f32)
            c0 = (n_kb - 1 - j) * tq
            s = s * scale + bias[:, c0:c0 + tq]
            if j > 0:
                s = s + jnp.where(i >= j, 0.0, MASK_VALUE)
            parts.append(s)
        m = parts[0].max(axis=-1, keepdims=True)
        for s in parts[1:]:
            m = jnp.maximum(m, s.max(axis=-1, keepdims=True))
        l = jnp.zeros_like(m)
        acc = jnp.zeros((tq, dh), f32)
        for idx, s in enumerate(parts):
            j = n_kb - 1 - idx
            p = jnp.exp(s - m)
            l = l + p.sum(axis=-1, keepdims=True)
            acc = acc + jnp.dot(p.astype(bf16), v_refs[j][:, sl], preferred_element_type=f32)
        outs.append(acc / l)
    raw_ref[hg] = jnp.concatenate(outs, axis=-1)

    @pl.when(hg == n_hg - 1)
    def _finish():
        ss = jnp.zeros((tq, 1), f32)
        for g in range(n_hg):
            r = raw_ref[g]
            ss = ss + jnp.sum(r * r, axis=-1, keepdims=True)
        inv = lax.rsqrt(ss / (n_hg * hb * dh) + EPS)
        for g in range(n_hg):
            w = hb * dh
            o_ref[:, g * w:(g + 1) * w] = (raw_ref[g] * inv * gain_ref[:, g * w:(g + 1) * w]).astype(o_ref.dtype)


def _attn_bias_table(rel_bias_l, tq, n_kb):
    qpos = (n_kb - 1) * tq + jnp.arange(tq)[:, None]
    kpos = jnp.arange(n_kb * tq)[None, :]
    rel = jnp.clip(qpos - kpos, -REL_CLIP, REL_CLIP) + REL_CLIP
    qc, kc = qpos // CHUNK, kpos // CHUNK
    valid = (kc <= qc) & (kc >= qc - LEFT_CHUNKS)
    bias = rel_bias_l.astype(f32)[:, rel]
    return jnp.where(valid[None], bias, MASK_VALUE)


def attention(qkv, rel_bias_l, gain_l, layer, batch, seq, attn_width):
    n = qkv.shape[0]
    n_heads = attn_width // ATTN_HEAD_DIM
    tq = min(256, seq)
    band = LEFT_CHUNKS * CHUNK
    assert band % tq == 0 and seq % tq == 0 and tq % CHUNK == 0
    n_kb = band // tq + 1
    nq = seq // tq
    hb = min(4, n_heads)
    n_hg = n_heads // hb
    w = hb * ATTN_HEAD_DIM
    bias = _attn_bias_table(rel_bias_l, tq, n_kb)

    def kv_spec(j, which):
        return pl.BlockSpec((tq, w), lambda b, i, g: (b * nq + jnp.maximum(i - j, 0), which * n_hg + g))

    in_specs = [pl.BlockSpec((tq, w), lambda b, i, g: (b * nq + i, g))]
    in_specs += [kv_spec(j, 1) for j in range(n_kb)] + [kv_spec(j, 2) for j in range(n_kb)]
    in_specs += [pl.BlockSpec((n_heads, tq, n_kb * tq), lambda b, i, g: (0, 0, 0)),
                 pl.BlockSpec((None, 1, attn_width), lambda b, i, g: (layer, 0, 0))]
    nbytes = 2 * (2 * n_kb + 1) * tq * w * 2 + 2 * n_heads * tq * n_kb * tq * 4 + 3 * tq * attn_width * 4
    return pl.pallas_call(
        functools.partial(_attn_kernel, n_kb=n_kb, hb=hb, tq=tq, n_hg=n_hg),
        grid=(batch, nq, n_hg),
        in_specs=in_specs,
        out_specs=pl.BlockSpec((tq, attn_width), lambda b, i, g: (b * nq + i, 0)),
        out_shape=jax.ShapeDtypeStruct((n, attn_width), bf16),
        scratch_shapes=[pltpu.VMEM((n_hg, tq, w), f32)],
        compiler_params=_params(("arbitrary", "arbitrary", "arbitrary"), nbytes),
        name="band_attention",
    )(*([qkv] * (2 * n_kb + 1)), bias, gain_l.reshape(gain_l.shape[0], 1, attn_width))


def _sigmoid(x):
    return 1.0 / (1.0 + jnp.exp(-x))


def _lru_kernel(xr_ref, xg_ref, cw_ref, cb_ref, wa_ref, ba_ref, wi_ref, bi_ref, lam_ref, gain_ref,
                o_ref, xe_ref, a_ref, b_ref, h_ref, carry_ref, *, t, n_blocks, conv_width):
    i = pl.program_id(1)
    hdr = V7X_SUBLANES
    bw = wa_ref.shape[-1]

    @pl.when(i == 0)
    def _reset():
        xe_ref[0:hdr, :] = jnp.zeros((hdr, xe_ref.shape[1]), f32)
        carry_ref[...] = jnp.zeros_like(carry_ref)

    x = xr_ref[...]
    xe_ref[hdr:hdr + t, :] = x
    u = cb_ref[...] + cw_ref[conv_width - 1:conv_width, :] * x
    for j in range(conv_width - 1):
        u = u + cw_ref[j:j + 1, :] * xe_ref[pl.ds(hdr - (conv_width - 1) + j, t), :]
    xe_ref[0:hdr, :] = x[t - hdr:t, :]

    lam = lam_ref[...]
    nsp = -LRU_C * (jnp.maximum(-lam, 0.0) + jnp.log1p(jnp.exp(-jnp.abs(lam))))
    for nb in range(n_blocks):
        sl = slice(nb * bw, (nb + 1) * bw)
        ub = u[:, sl]
        ubb = ub.astype(bf16)
        r = _sigmoid(jnp.dot(ubb, wa_ref[nb].astype(bf16), preferred_element_type=f32) + ba_ref[:, sl])
        ig = _sigmoid(jnp.dot(ubb, wi_ref[nb].astype(bf16), preferred_element_type=f32) + bi_ref[:, sl])
        log_a = r * nsp[:, sl]
        a = jnp.exp(log_a)
        a_ref[:, sl] = a
        b_ref[:, sl] = jnp.sqrt(1.0 - a * a) * (ig * ub)

    a = a_ref[...]
    b = b_ref[...]
    row = lax.broadcasted_iota(jnp.int32, a.shape, 0) % hdr
    s = 1
    while s < hdr:
        keep = row >= s
        a_sh = jnp.where(keep, pltpu.roll(a, s, 0), 1.0)
        b_sh = jnp.where(keep, pltpu.roll(b, s, 0), 0.0)
        b = a * b_sh + b
        a = a * a_sh
        s *= 2
    a_ref[...] = a
    b_ref[...] = b

    def tile_step(j, carry):
        r0 = pl.multiple_of(j * hdr, hdr)
        h = a_ref[pl.ds(r0, hdr), :] * carry + b_ref[pl.ds(r0, hdr), :]
        h_ref[pl.ds(r0, hdr), :] = h
        return h[hdr - 1:hdr, :]

    carry_ref[...] = lax.fori_loop(0, t // hdr, tile_step, carry_ref[...])

    y = h_ref[...] * jax.nn.gelu(xg_ref[...], approximate=True)
    inv = lax.rsqrt(jnp.mean(y * y, axis=-1, keepdims=True) + EPS)
    o_ref[...] = (y * inv * gain_ref[...]).astype(o_ref.dtype)


def rglru_block(rg, conv_w, conv_b, w_a, b_a, w_i, b_i, lam, gain, layer, batch, seq):
    n = rg.shape[0]
    n_layers, conv_width, wr = conv_w.shape
    n_blocks = w_a.shape[1]
    bw = w_a.shape[2]
    t = min(256, seq)
    ns = seq // t
    row3 = lambda v: v.reshape(n_layers, 1, wr)
    vec_spec = pl.BlockSpec((None, 1, wr), lambda b, i: (layer, 0, 0))
    wspec = pl.BlockSpec((None, n_blocks, bw, bw), lambda b, i: (layer, 0, 0, 0))
    nbytes = 2 * 2 * t * wr * 4 + 2 * t * wr * 2 + 4 * (t + 8) * wr * 4 + 4 * n_blocks * bw * bw * 4 + 8 * t * wr * 4
    return pl.pallas_call(
        functools.partial(_lru_kernel, t=t, n_blocks=n_blocks, conv_width=conv_width),
        grid=(batch, ns),
        in_specs=[pl.BlockSpec((t, wr), lambda b, i: (b * ns + i, 0)),
                  pl.BlockSpec((t, wr), lambda b, i: (b * ns + i, 1)),
                  pl.BlockSpec((None, conv_width, wr), lambda b, i: (layer, 0, 0)),
                  vec_spec, wspec, vec_spec, wspec, vec_spec, vec_spec, vec_spec],
        out_specs=pl.BlockSpec((t, wr), lambda b, i: (b * ns + i, 0)),
        out_shape=jax.ShapeDtypeStruct((n, wr), bf16),
        scratch_shapes=[pltpu.VMEM((t + V7X_SUBLANES, wr), f32), pltpu.VMEM((t, wr), f32),
                        pltpu.VMEM((t, wr), f32), pltpu.VMEM((t, wr), f32), pltpu.VMEM((1, wr), f32)],
        compiler_params=_params(("arbitrary", "arbitrary"), nbytes),
        name="conv_rglru",
    )(rg, rg, conv_w, row3(conv_b), w_a, row3(b_a), w_i, row3(b_i), row3(lam), row3(gain))


def _router_kernel(x_ref, g_ref, rwt_ref, rb_ref, hp_ref, idx_ref, gate_ref, rank_ref, cnt_ref, carry_ref,
                   *, tb, n_exp):
    step = pl.program_id(0)

    @pl.when(step == 0)
    def _reset():
        carry_ref[...] = jnp.zeros_like(carry_ref)

    x = x_ref[...]
    inv = lax.rsqrt(jnp.mean(x * x, axis=-1, keepdims=True) + EPS)
    h = x * inv * g_ref[...]
    half = h.shape[1] // 2
    bits = lax.bitcast_convert_type(h.astype(bf16).astype(f32), jnp.uint32)
    hp_ref[...] = (bits[:, :half] >> 16) | bits[:, half:]

    logits = lax.dot_general(rwt_ref[...], h, (((1,), (1,)), ((), ())), preferred_element_type=f32,
                             precision=lax.Precision.HIGHEST) + rb_ref[...]
    e_iota = lax.broadcasted_iota(jnp.int32, (n_exp, tb), 0)
    tri = (lax.broadcasted_iota(jnp.int32, (tb, tb), 0) < lax.broadcasted_iota(jnp.int32, (tb, tb), 1)).astype(bf16)
    base = carry_ref[...]
    vals, idxs, ranks = [], [], []
    l = logits
    for _ in range(TOP_K):
        m = l.max(axis=0, keepdims=True)
        idx = jnp.min(jnp.where(l == m, e_iota, n_exp), axis=0, keepdims=True)
        onehot = e_iota == idx
        ohf = onehot.astype(f32)
        before = jnp.dot(ohf.astype(bf16), tri, preferred_element_type=f32)
        ranks.append(jnp.sum(jnp.where(onehot, before + base, 0.0), axis=0, keepdims=True))
        base = base + jnp.sum(ohf, axis=1, keepdims=True)
        vals.append(m)
        idxs.append(idx)
        l = jnp.where(onehot, -jnp.inf, l)
    carry_ref[...] = base
    exps = [jnp.exp(v - vals[0]) for v in vals]
    denom = exps[0]
    for e in exps[1:]:
        denom = denom + e
    idx_ref[...] = jnp.concatenate(idxs, axis=0)
    gate_ref[...] = jnp.concatenate([e / denom for e in exps], axis=0)
    rank_ref[...] = jnp.concatenate(ranks, axis=0).astype(jnp.int32)
    cnt_ref[...] = jnp.broadcast_to(base, cnt_ref.shape).astype(jnp.int32)


def router(x2d, gain_l, router_w_l, router_b_l, layer):
    n, d = x2d.shape
    n_exp = router_w_l.shape[-1]
    tb = min(256, n)
    rwt = jnp.swapaxes(router_w_l[layer], 0, 1)
    rb = router_b_l[layer].reshape(n_exp, 1)
    slot_spec = pl.BlockSpec((TOP_K, tb), lambda i: (0, i))
    nbytes = 2 * tb * d * 4 + 2 * tb * d * 2 + 2 * n_exp * d * 4 + 6 * tb * d * 4
    return pl.pallas_call(
        functools.partial(_router_kernel, tb=tb, n_exp=n_exp),
        grid=(n // tb,),
        in_specs=[pl.BlockSpec((tb, d), lambda i: (i, 0)),
                  pl.BlockSpec((None, 1, d), lambda i: (layer, 0, 0)),
                  pl.BlockSpec((n_exp, d), lambda i: (0, 0)),
                  pl.BlockSpec((n_exp, 1), lambda i: (0, 0))],
        out_specs=[pl.BlockSpec((tb, d // 2), lambda i: (i, 0)), slot_spec, slot_spec, slot_spec,
                   pl.BlockSpec((n_exp, V7X_LANES), lambda i: (0, 0))],
        out_shape=[jax.ShapeDtypeStruct((n, d // 2), jnp.uint32),
                   jax.ShapeDtypeStruct((TOP_K, n), jnp.int32),
                   jax.ShapeDtypeStruct((TOP_K, n), f32),
                   jax.ShapeDtypeStruct((TOP_K, n), jnp.int32),
                   jax.ShapeDtypeStruct((n_exp, V7X_LANES), jnp.int32)],
        scratch_shapes=[pltpu.VMEM((n_exp, 1), f32)],
        compiler_params=_params(("arbitrary",), nbytes),
        name="router_topk",
    )(x2d, gain_l.reshape(gain_l.shape[0], 1, d), rwt, rb)


def _moe_kernel(tile_e_ref, tile_rows_ref, n_used_ref,
                tok_hbm, hp_hbm, wgu_ref, bgu_ref, wdn_ref, bdn_ref,
                y_ref,
                tok_smem, stage_ref, xs_ref, act_ref, wgu_bf, wdn_bf, tok_sem, row_sem,
                *, tm, sub, n_gu, n_dn, tgu):
    t = pl.program_id(0)
    s = pl.program_id(1)
    n_used = n_used_ref[0]
    active = t < n_used
    half = stage_ref.shape[1]
    th = tgu // 2

    def n_sub(tile):
        return (tile_rows_ref[tile] + sub - 1) // sub

    def tok_copy(tile):
        return pltpu.make_async_copy(tok_hbm.at[tile], tok_smem.at[tile % 2], tok_sem)

    def row_copy(tok, r):
        return pltpu.make_async_copy(hp_hbm.at[pl.ds(tok, 1), :], stage_ref.at[pl.ds(r, 1), :], row_sem)

    def start_gather(tile):
        slot = tile % 2

        def issue(r, c):
            row_copy(tok_smem[slot, r], r).start()
            return c

        lax.fori_loop(0, n_sub(tile) * sub, issue, 0)

    def wait_gather(tile):
        def wait(r, c):
            row_copy(0, r).wait()
            return c

        lax.fori_loop(0, n_sub(tile) * sub, wait, 0)

    @pl.when(active & (s == 0))
    def _stage_rows():
        @pl.when(t == 0)
        def _first():
            tok_copy(0).start()
            tok_copy(0).wait()
            start_gather(0)

        wait_gather(t)

        def unpack(rb, c):
            r0 = pl.multiple_of(rb * sub, sub)
            w = stage_ref[pl.ds(r0, sub), :]
            xs_ref[pl.ds(r0, sub), 0:half] = lax.bitcast_convert_type(w << 16, f32).astype(bf16)
            xs_ref[pl.ds(r0, sub), half:2 * half] = lax.bitcast_convert_type(w & jnp.uint32(0xFFFF0000), f32).astype(bf16)
            return c

        lax.fori_loop(0, n_sub(t), unpack, 0)

        @pl.when(t + 1 < n_used)
        def _next_tokens():
            tok_copy(t + 1).start()

    @pl.when(active & (s == 1) & (t + 1 < n_used))
    def _prefetch_rows():
        tok_copy(t + 1).wait()
        start_gather(t + 1)

    @pl.when(active & (s < n_gu))
    def _gate_up():
        wgu_bf[...] = wgu_ref[...].astype(bf16)
        sel = (lax.broadcasted_iota(jnp.int32, (tgu, th), 0) == 2 * lax.broadcasted_iota(jnp.int32, (tgu, th), 1)).astype(bf16)

        def body(rb, c):
            r0 = pl.multiple_of(rb * sub, sub)
            gu = jnp.dot(xs_ref[pl.ds(r0, sub), :], wgu_bf[...], preferred_element_type=f32) + bgu_ref[...]
            glu = jnp.minimum(gu, SWIGLU_LIMIT)
            glu = glu * _sigmoid(SWIGLU_ALPHA * glu)
            lin = jnp.clip(gu, -SWIGLU_LIMIT, SWIGLU_LIMIT) + 1.0
            prod = glu * pltpu.roll(lin, tgu - 1, 1)
            act_ref[s, pl.ds(r0, sub), :] = jnp.dot(prod.astype(bf16), sel, preferred_element_type=f32).astype(bf16)
            return c

        lax.fori_loop(0, n_sub(t), body, 0)

    @pl.when(active & (s >= n_gu))
    def _down():
        wdn_bf[...] = wdn_ref[...].astype(bf16)
        ns = n_sub(t)

        def body(rb, c):
            r0 = pl.multiple_of(rb * sub, sub)
            acc = jnp.zeros((sub, y_ref.shape[1]), f32) + bdn_ref[...]
            for g in range(n_gu):
                acc = acc + jnp.dot(act_ref[g, pl.ds(r0, sub), :], wdn_bf[g * th:(g + 1) * th, :],
                                    preferred_element_type=f32)
            y_ref[pl.ds(r0, sub), :] = acc
            return c

        lax.fori_loop(0, ns, body, 0)

        def clear(rb, c):
            r0 = pl.multiple_of(rb * sub, sub)
            y_ref[pl.ds(r0, sub), :] = jnp.zeros((sub, y_ref.shape[1]), f32)
            return c

        lax.fori_loop(ns, tm // sub, clear, 0)


def moe_experts(hp, tok_tiles, tile_e, tile_rows, n_used, w_gu, b_gu, w_dn, b_dn, layer, tm, sub):
    n, half = hp.shape
    d = 2 * half
    n_layers, n_exp, _, two_f = w_gu.shape
    ff = two_f // 2
    t_max = tok_tiles.shape[0]
    tgu = min(512, two_f)
    tdn = min(512, d)
    n_gu, n_dn = two_f // tgu, d // tdn

    def eff(t, s, te, nu):
        live = t < nu[0]
        return jnp.where(live, t, nu[0] - 1), live

    def gu_idx(t, s, te, tr, nu):
        _, live = eff(t, s, te, nu)
        return (layer, te[t], 0, jnp.where(live, jnp.minimum(s, n_gu - 1), n_gu - 1))

    def dn_idx(t, s, te, tr, nu):
        _, live = eff(t, s, te, nu)
        return (layer, te[t], 0, jnp.where(live, jnp.maximum(s - n_gu, 0), n_dn - 1))

    def y_idx(t, s, te, tr, nu):
        tt, live = eff(t, s, te, nu)
        return (tt, jnp.where(live, jnp.maximum(s - n_gu, 0), n_dn - 1))

    nbytes = (tm * half * 4 + tm * d * 2 + tm * ff * 2 + 2 * d * tgu * 4 + d * tgu * 2
              + 2 * ff * tdn * 4 + ff * tdn * 2 + 2 * tm * tdn * 4)
    grid_spec = pltpu.PrefetchScalarGridSpec(
        num_scalar_prefetch=3,
        grid=(t_max, n_gu + n_dn),
        in_specs=[pl.BlockSpec(memory_space=pl.ANY),
                  pl.BlockSpec(memory_space=pl.ANY),
                  pl.BlockSpec((None, None, d, tgu), gu_idx),
                  pl.BlockSpec((None, None, 1, tgu), gu_idx),
                  pl.BlockSpec((None, None, ff, tdn), dn_idx),
                  pl.BlockSpec((None, None, 1, tdn), dn_idx)],
        out_specs=pl.BlockSpec((tm, tdn), y_idx),
        scratch_shapes=[pltpu.SMEM((2, tm), jnp.int32),
                        pltpu.VMEM((tm, half), jnp.uint32),
                        pltpu.VMEM((tm, d), bf16),
                        pltpu.VMEM((n_gu, tm, tgu // 2), bf16),
                        pltpu.VMEM((d, tgu), bf16),
                        pltpu.VMEM((ff, tdn), bf16),
                        pltpu.SemaphoreType.DMA(()),
                        pltpu.SemaphoreType.DMA(())],
    )
    return pl.pallas_call(
        functools.partial(_moe_kernel, tm=tm, sub=sub, n_gu=n_gu, n_dn=n_dn, tgu=tgu),
        grid_spec=grid_spec,
        out_shape=jax.ShapeDtypeStruct((t_max * tm, d), f32),
        compiler_params=_params(("arbitrary", "arbitrary"), nbytes),
        name="moe_experts",
    )(tile_e, tile_rows, n_used, tok_tiles, hp, w_gu,
      b_gu.reshape(n_layers, n_exp, 1, two_f), w_dn, b_dn.reshape(n_layers, n_exp, 1, d))


def _combine_kernel(dest_hbm, y_hbm, x_ref, gate_ref, gain_ref, *refs, tb, n_steps, final):
    if final:
        o_ref, dest_smem, ybuf, dest_sem, row_sem = refs
        xo_ref = None
    else:
        xo_ref, o_ref, dest_smem, ybuf, dest_sem, row_sem = refs
    i = pl.program_id(0)
    n_rows = TOP_K * tb

    def dest_copy(step):
        return pltpu.make_async_copy(dest_hbm.at[step], dest_smem.at[step % 2], dest_sem)

    def row_copy(step, r, row):
        return pltpu.make_async_copy(y_hbm.at[pl.ds(row, 1), :], ybuf.at[step % 2, pl.ds(r, 1), :], row_sem.at[step % 2])

    def start_gather(step):
        def issue(r, c):
            row_copy(step, r, dest_smem[step % 2, r]).start()
            return c

        lax.fori_loop(0, n_rows, issue, 0)

    @pl.when(i == 0)
    def _first():
        dest_copy(0).start()
        dest_copy(0).wait()
        start_gather(0)

        if n_steps > 1:
            dest_copy(1).start()

    @pl.when(i + 1 < n_steps)
    def _next():
        dest_copy(i + 1).wait()
        start_gather(i + 1)

        @pl.when(i + 2 < n_steps)
        def _():
            dest_copy(i + 2).start()

    def wait(r, c):
        row_copy(i, r, 0).wait()
        return c

    lax.fori_loop(0, n_rows, wait, 0)

    g = gate_ref[...]
    acc = x_ref[...]
    for k in range(TOP_K):
        acc = acc + g[:, k:k + 1] * ybuf[i % 2, k * tb:(k + 1) * tb, :]
    if not final:
        xo_ref[...] = acc
    inv = lax.rsqrt(jnp.mean(acc * acc, axis=-1, keepdims=True) + EPS)
    o_ref[...] = (acc * inv * gain_ref[...]).astype(o_ref.dtype)


def combine(x2d, y, dest, gates_t, gain2d, final):
    n, d = x2d.shape
    tb = min(128, n)
    n_steps = n // tb
    dest_blk = dest.reshape(TOP_K, n_steps, tb).transpose(1, 0, 2).reshape(n_steps, TOP_K * tb)
    row_spec = pl.BlockSpec((tb, d), lambda i: (i, 0))
    if final:
        out_shape = jax.ShapeDtypeStruct((n, d), f32)
        out_specs = row_spec
    else:
        out_shape = [jax.ShapeDtypeStruct((n, d), f32), jax.ShapeDtypeStruct((n, d), bf16)]
        out_specs = [row_spec, row_spec]
    nbytes = 2 * TOP_K * tb * d * 4 + 6 * tb * d * 4
    return pl.pallas_call(
        functools.partial(_combine_kernel, tb=tb, n_steps=n_steps, final=final),
        grid=(n_steps,),
        in_specs=[pl.BlockSpec(memory_space=pl.ANY),
                  pl.BlockSpec(memory_space=pl.ANY),
                  row_spec,
                  pl.BlockSpec((tb, TOP_K), lambda i: (i, 0)),
                  pl.BlockSpec((1, d), lambda i: (0, 0))],
        out_specs=out_specs,
        out_shape=out_shape,
        scratch_shapes=[pltpu.SMEM((2, TOP_K * tb), jnp.int32),
                        pltpu.VMEM((2, TOP_K * tb, d), f32),
                        pltpu.SemaphoreType.DMA(()),
                        pltpu.SemaphoreType.DMA((2,))],
        compiler_params=_params(("arbitrary",), nbytes),
        name="moe_combine",
    )(dest_blk, y, x2d, gates_t, gain2d)


def _dispatch_plan(counts, idx, rank, tm, t_max):
    n_exp = counts.shape[0]
    n = idx.shape[1]
    tiles_e = (counts + tm - 1) // tm
    tile_end = jnp.cumsum(tiles_e)
    tile_start = tile_end - tiles_e
    n_used = tile_end[-1]
    dest = (tile_start * tm)[idx] + rank
    tile = jnp.arange(t_max, dtype=jnp.int32)
    tile_c = jnp.minimum(tile, n_used - 1)
    tile_e = jnp.minimum(jnp.searchsorted(tile_end, tile_c, side="right"), n_exp - 1).astype(jnp.int32)
    rows = jnp.clip(counts[tile_e] - (tile_c - tile_start[tile_e]) * tm, 0, tm)
    tile_rows = jnp.where(tile < n_used, rows, 0).astype(jnp.int32)
    tok = jnp.broadcast_to(jnp.arange(n, dtype=jnp.int32)[None, :], (TOP_K, n))
    tok_tiles = jnp.zeros((t_max * tm,), jnp.int32).at[dest.reshape(-1)].set(tok.reshape(-1))
    return dest, tok_tiles.reshape(t_max, tm), tile_e, tile_rows, n_used.reshape(1).astype(jnp.int32)


def kernel(x, norm_mix, w_in, attn_rel_bias, attn_out_gain, conv_w, conv_b, lru_w_a, lru_b_a, lru_w_i, lru_b_i,
           lru_lambda, rec_out_gain, w_out, norm_ffn, router_w, router_b, expert_w_gate_up, expert_b_gate_up,
           expert_w_down, expert_b_down, norm_final):
    batch, seq, d = x.shape
    n = batch * seq
    depth = norm_mix.shape[0]
    attn_width = attn_out_gain.shape[1]
    lru_width = rec_out_gain.shape[1]
    n_exp = router_w.shape[-1]
    tm = min(1024, n)
    sub = min(256, tm)
    t_max = (n * TOP_K) // tm + n_exp

    x2d = x.reshape(n, d)
    h = rmsnorm_bf16(x2d, norm_mix, 0)
    out = None
    for l in range(depth):
        qkv = matmul_ws([h], w_in, l, [0], 0, 3 * attn_width, bf16, name="in_proj_qkv")
        rg = matmul_ws([h], w_in, l, [0], 3 * attn_width, 2 * lru_width, f32, name="in_proj_rec")
        attn = attention(qkv, attn_rel_bias[l], attn_out_gain, l, batch, seq, attn_width)
        rec = rglru_block(rg, conv_w, conv_b, lru_w_a, lru_b_a, lru_w_i, lru_b_i, lru_lambda, rec_out_gain,
                          l, batch, seq)
        assert attn_width == lru_width
        x2d = matmul_ws([attn, rec], w_out, l, [0, 1], 0, d, f32, residual=x2d, name="out_proj")
        hp, idx, gates, rank, cnt = router(x2d, norm_ffn, router_w, router_b, l)
        dest, tok_tiles, tile_e, tile_rows, n_used = _dispatch_plan(cnt[:, 0], idx, rank, tm, t_max)
        y = moe_experts(hp, tok_tiles, tile_e, tile_rows, n_used, expert_w_gate_up, expert_b_gate_up,
                        expert_w_down, expert_b_down, l, tm, sub)
        final = l == depth - 1
        gain = norm_final.reshape(1, d) if final else norm_mix[l + 1].reshape(1, d)
        res = combine(x2d, y, dest, gates.T, gain, final)
        if final:
            out = res
        else:
            x2d, h = res
    return out.reshape(batch, seq, d)
```

```python
import functools
import math

import jax
import jax.numpy as jnp
from jax import lax
from jax.experimental import pallas as pl
from jax.experimental.pallas import tpu as pltpu

CHUNK = 64
LEFT_CHUNKS = 8
ATTN_HEAD_DIM = 128
REL_CLIP = 128
LRU_C = 8.0
TOP_K = 4
SWIGLU_LIMIT = 7.0
SWIGLU_ALPHA = 1.702
EPS = 1e-6

V7X_LANES = 128
V7X_SUBLANES = 8
V7X_VMEM_BYTES = 64 * 1024 * 1024

MASK_VALUE = -1e30

f32 = jnp.float32
bf16 = jnp.bfloat16


def _vmem_limit(nbytes):
    return int(min(nbytes + 12 * 1024 * 1024, V7X_VMEM_BYTES - 6 * 1024 * 1024))


def _params(sem, nbytes):
    return pltpu.CompilerParams(dimension_semantics=sem, vmem_limit_bytes=_vmem_limit(nbytes))


def _rmsnorm_kernel(x_ref, g_ref, o_ref):
    x = x_ref[...]
    inv = lax.rsqrt(jnp.mean(x * x, axis=-1, keepdims=True) + EPS)
    o_ref[...] = (x * inv * g_ref[...]).astype(o_ref.dtype)


def rmsnorm_bf16(x2d, gain_l, layer):
    n, d = x2d.shape
    tb = min(256, n)
    return pl.pallas_call(
        _rmsnorm_kernel,
        grid=(n // tb,),
        in_specs=[pl.BlockSpec((tb, d), lambda i: (i, 0)),
                  pl.BlockSpec((None, 1, d), lambda i: (layer, 0, 0))],
        out_specs=pl.BlockSpec((tb, d), lambda i: (i, 0)),
        out_shape=jax.ShapeDtypeStruct((n, d), bf16),
        compiler_params=_params(("arbitrary",), 2 * tb * d * 6),
        name="rmsnorm_bf16",
    )(x2d, gain_l.reshape(gain_l.shape[0], 1, d))


def _matmul_kernel(*refs, n_pairs, has_res):
    a_refs = refs[:n_pairs]
    w_refs = refs[n_pairs:2 * n_pairs]
    pos = 2 * n_pairs
    res_ref = refs[pos] if has_res else None
    pos += int(has_res)
    o_ref = refs[pos]
    wbf_refs = refs[pos + 1:pos + 1 + n_pairs]

    @pl.when(pl.program_id(1) == 0)
    def _cast_weights():
        for w_ref, wbf_ref in zip(w_refs, wbf_refs):
            wbf_ref[...] = w_ref[...].astype(bf16)

    acc = jnp.dot(a_refs[0][...], wbf_refs[0][...], preferred_element_type=f32)
    for a_ref, wbf_ref in zip(a_refs[1:], wbf_refs[1:]):
        acc += jnp.dot(a_ref[...], wbf_ref[...], preferred_element_type=f32)
    if has_res:
        acc += res_ref[...]
    o_ref[...] = acc.astype(o_ref.dtype)


def matmul_ws(a_list, w_stack, layer, k_blocks, col0, n_cols, out_dtype, residual=None, name="matmul_ws"):
    m = a_list[0].shape[0]
    tm = min(1024, m)
    tn = min(512, n_cols)
    assert m % tm == 0 and n_cols % tn == 0 and col0 % tn == 0
    cb0 = col0 // tn
    n_pairs = len(a_list)
    in_specs, scratch = [], []
    nbytes = 2 * tm * tn * 4 * (2 if residual is not None else 1)
    for a in a_list:
        kp = a.shape[1]
        in_specs.append(pl.BlockSpec((tm, kp), lambda j, i: (i, 0)))
        nbytes += 2 * tm * kp * 2
    for a, kb in zip(a_list, k_blocks):
        kp = a.shape[1]
        in_specs.append(pl.BlockSpec((None, kp, tn), functools.partial(lambda j, i, kb: (layer, kb, j + cb0), kb=kb)))
        scratch.append(pltpu.VMEM((kp, tn), bf16))
        nbytes += 2 * kp * tn * 4 + kp * tn * 2
    args = list(a_list) + [w_stack] * n_pairs
    if residual is not None:
        in_specs.append(pl.BlockSpec((tm, tn), lambda j, i: (i, j)))
        args.append(residual)
    return pl.pallas_call(
        functools.partial(_matmul_kernel, n_pairs=n_pairs, has_res=residual is not None),
        grid=(n_cols // tn, m // tm),
        in_specs=in_specs,
        out_specs=pl.BlockSpec((tm, tn), lambda j, i: (i, j)),
        out_shape=jax.ShapeDtypeStruct((m, n_cols), out_dtype),
        scratch_shapes=scratch,
        compiler_params=_params(("arbitrary", "arbitrary"), nbytes),
        name=name,
    )(*args)


def _attn_kernel(q_ref, *refs, n_kb, hb, tq, n_hg):
    k_refs = refs[:n_kb]
    v_refs = refs[n_kb:2 * n_kb]
    row0_ref, gain_ref, o_ref, raw_ref, bias_ref = refs[2 * n_kb:]
    i = pl.program_id(1)
    hg = pl.program_id(2)
    dh = ATTN_HEAD_DIM
    scale = 1.0 / math.sqrt(dh)
    n_keys = n_kb * tq

    @pl.when((pl.program_id(0) == 0) & (i == 0) & (hg == 0))
    def _build_bias():
        width = row0_ref.shape[-1]
        qc = ((n_kb - 1) * tq + lax.broadcasted_iota(jnp.int32, (tq, n_keys), 0)) // CHUNK
        kc = lax.broadcasted_iota(jnp.int32, (tq, n_keys), 1) // CHUNK
        valid = (kc <= qc) & (kc >= qc - LEFT_CHUNKS)
        for h in range(bias_ref.shape[0]):
            rows = pltpu.roll(jnp.broadcast_to(row0_ref[h], (tq, width)), 0, 1, stride=1, stride_axis=0)
            bias_ref[h] = jnp.where(valid, rows[:, :n_keys], MASK_VALUE)

    outs = []
    for hh in range(hb):
        sl = slice(hh * dh, (hh + 1) * dh)
        q = q_ref[:, sl]
        bias = bias_ref[hg * hb + hh]
        parts = []
        for j in range(n_kb - 1, -1, -1):
            s = lax.dot_general(q, k_refs[j][:, sl], (((1,), (1,)), ((), ())), preferred_element_type=f32)
            c0 = (n_kb - 1 - j) * tq
            s = s * scale + bias[:, c0:c0 + tq]
            if j > 0:
                s = s + jnp.where(i >= j, 0.0, MASK_VALUE)
            parts.append(s)
        m = parts[0].max(axis=-1, keepdims=True)
        for s in parts[1:]:
            m = jnp.maximum(m, s.max(axis=-1, keepdims=True))
        l = jnp.zeros_like(m)
        acc = jnp.zeros((tq, dh), f32)
        for idx, s in enumerate(parts):
            j = n_kb - 1 - idx
            p = jnp.exp(s - m)
            l = l + p.sum(axis=-1, keepdims=True)
            acc = acc + jnp.dot(p.astype(bf16), v_refs[j][:, sl], preferred_element_type=f32)
        outs.append(acc / l)
    raw_ref[hg] = jnp.concatenate(outs, axis=-1)

    @pl.when(hg == n_hg - 1)
    def _finish():
        ss = jnp.zeros((tq, 1), f32)
        for g in range(n_hg):
            r = raw_ref[g]
            ss = ss + jnp.sum(r * r, axis=-1, keepdims=True)
        inv = lax.rsqrt(ss / (n_hg * hb * dh) + EPS)
        for g in range(n_hg):
            w = hb * dh
            o_ref[:, g * w:(g + 1) * w] = (raw_ref[g] * inv * gain_ref[:, g * w:(g + 1) * w]).astype(o_ref.dtype)


def _attn_bias_row0(rel_bias_l, tq, n_kb):
    n_heads = rel_bias_l.shape[0]
    rb = rel_bias_l.astype(f32)
    far = (n_kb - 1) * tq - REL_CLIP
    near = n_kb * tq - far - (2 * REL_CLIP + 1)
    assert far >= 0 and near >= 0
    fill = lambda col, width: jnp.broadcast_to(rb[:, col:col + 1], (n_heads, width))
    row0 = jnp.concatenate([fill(2 * REL_CLIP, far), rb[:, ::-1], fill(0, near), fill(2 * REL_CLIP, tq)], axis=1)
    return row0.reshape(n_heads, 1, (n_kb + 1) * tq)


def attention(qkv, rel_bias_l, gain_l, layer, batch, seq, attn_width):
    n = qkv.shape[0]
    n_heads = attn_width // ATTN_HEAD_DIM
    tq = min(256, seq)
    band = LEFT_CHUNKS * CHUNK
    assert band % tq == 0 and seq % tq == 0 and tq % CHUNK == 0
    n_kb = band // tq + 1
    nq = seq // tq
    hb = min(4, n_heads)
    n_hg = n_heads // hb
    w = hb * ATTN_HEAD_DIM
    row0 = _attn_bias_row0(rel_bias_l, tq, n_kb)

    def kv_spec(j, which):
        return pl.BlockSpec((tq, w), lambda b, i, g: (b * nq + jnp.maximum(i - j, 0), which * n_hg + g))

    in_specs = [pl.BlockSpec((tq, w), lambda b, i, g: (b * nq + i, g))]
    in_specs += [kv_spec(j, 1) for j in range(n_kb)] + [kv_spec(j, 2) for j in range(n_kb)]
    in_specs += [pl.BlockSpec(row0.shape, lambda b, i, g: (0, 0, 0)),
                 pl.BlockSpec((None, 1, attn_width), lambda b, i, g: (layer, 0, 0))]
    nbytes = 2 * (2 * n_kb + 1) * tq * w * 2 + n_heads * tq * n_kb * tq * 4 + 3 * tq * attn_width * 4
    return pl.pallas_call(
        functools.partial(_attn_kernel, n_kb=n_kb, hb=hb, tq=tq, n_hg=n_hg),
        grid=(batch, nq, n_hg),
        in_specs=in_specs,
        out_specs=pl.BlockSpec((tq, attn_width), lambda b, i, g: (b * nq + i, 0)),
        out_shape=jax.ShapeDtypeStruct((n, attn_width), bf16),
        scratch_shapes=[pltpu.VMEM((n_hg, tq, w), f32), pltpu.VMEM((n_heads, tq, n_kb * tq), f32)],
        compiler_params=_params(("arbitrary", "arbitrary", "arbitrary"), nbytes),
        name="band_attention",
    )(*([qkv] * (2 * n_kb + 1)), row0, gain_l.reshape(gain_l.shape[0], 1, attn_width))


def _sigmoid(x):
    return 1.0 / (1.0 + jnp.exp(-x))


def _lru_kernel(xr_ref, xg_ref, cw_ref, cb_ref, wa_ref, ba_ref, wi_ref, bi_ref, lam_ref, gain_ref,
                o_ref, xe_ref, a_ref, b_ref, h_ref, carry_ref, *, t, n_blocks, conv_width):
    i = pl.program_id(1)
    hdr = V7X_SUBLANES
    bw = wa_ref.shape[-1]

    @pl.when(i == 0)
    def _reset():
        xe_ref[0:hdr, :] = jnp.zeros((hdr, xe_ref.shape[1]), f32)
        carry_ref[...] = jnp.zeros_like(carry_ref)

    x = xr_ref[...]
    xe_ref[hdr:hdr + t, :] = x
    u = cb_ref[...] + cw_ref[conv_width - 1:conv_width, :] * x
    for j in range(conv_width - 1):
        u = u + cw_ref[j:j + 1, :] * xe_ref[pl.ds(hdr - (conv_width - 1) + j, t), :]
    xe_ref[0:hdr, :] = x[t - hdr:t, :]

    lam = lam_ref[...]
    nsp = -LRU_C * (jnp.maximum(-lam, 0.0) + jnp.log1p(jnp.exp(-jnp.abs(lam))))
    for nb in range(n_blocks):
        sl = slice(nb * bw, (nb + 1) * bw)
        ub = u[:, sl]
        ubb = ub.astype(bf16)
        r = _sigmoid(jnp.dot(ubb, wa_ref[nb].astype(bf16), preferred_element_type=f32) + ba_ref[:, sl])
        ig = _sigmoid(jnp.dot(ubb, wi_ref[nb].astype(bf16), preferred_element_type=f32) + bi_ref[:, sl])
        log_a = r * nsp[:, sl]
        a = jnp.exp(log_a)
        a_ref[:, sl] = a
        b_ref[:, sl] = jnp.sqrt(1.0 - a * a) * (ig * ub)

    a = a_ref[...]
    b = b_ref[...]
    row = lax.broadcasted_iota(jnp.int32, a.shape, 0) % hdr
    s = 1
    while s < hdr:
        keep = row >= s
        a_sh = jnp.where(keep, pltpu.roll(a, s, 0), 1.0)
        b_sh = jnp.where(keep, pltpu.roll(b, s, 0), 0.0)
        b = a * b_sh + b
        a = a * a_sh
        s *= 2
    a_ref[...] = a
    b_ref[...] = b

    def tile_step(j, carry):
        r0 = pl.multiple_of(j * hdr, hdr)
        h = a_ref[pl.ds(r0, hdr), :] * carry + b_ref[pl.ds(r0, hdr), :]
        h_ref[pl.ds(r0, hdr), :] = h
        return h[hdr - 1:hdr, :]

    carry_ref[...] = lax.fori_loop(0, t // hdr, tile_step, carry_ref[...])

    y = h_ref[...] * jax.nn.gelu(xg_ref[...], approximate=True)
    inv = lax.rsqrt(jnp.mean(y * y, axis=-1, keepdims=True) + EPS)
    o_ref[...] = (y * inv * gain_ref[...]).astype(o_ref.dtype)


def rglru_block(rg, conv_w, conv_b, w_a, b_a, w_i, b_i, lam, gain, layer, batch, seq):
    n = rg.shape[0]
    n_layers, conv_width, wr = conv_w.shape
    n_blocks = w_a.shape[1]
    bw = w_a.shape[2]
    t = min(256, seq)
    ns = seq // t
    row3 = lambda v: v.reshape(n_layers, 1, wr)
    vec_spec = pl.BlockSpec((None, 1, wr), lambda b, i: (layer, 0, 0))
    wspec = pl.BlockSpec((None, n_blocks, bw, bw), lambda b, i: (layer, 0, 0, 0))
    nbytes = 2 * 2 * t * wr * 4 + 2 * t * wr * 2 + 4 * (t + 8) * wr * 4 + 4 * n_blocks * bw * bw * 4 + 8 * t * wr * 4
    return pl.pallas_call(
        functools.partial(_lru_kernel, t=t, n_blocks=n_blocks, conv_width=conv_width),
        grid=(batch, ns),
        in_specs=[pl.BlockSpec((t, wr), lambda b, i: (b * ns + i, 0)),
                  pl.BlockSpec((t, wr), lambda b, i: (b * ns + i, 1)),
                  pl.BlockSpec((None, conv_width, wr), lambda b, i: (layer, 0, 0)),
                  vec_spec, wspec, vec_spec, wspec, vec_spec, vec_spec, vec_spec],
        out_specs=pl.BlockSpec((t, wr), lambda b, i: (b * ns + i, 0)),
        out_shape=jax.ShapeDtypeStruct((n, wr), bf16),
        scratch_shapes=[pltpu.VMEM((t + V7X_SUBLANES, wr), f32), pltpu.VMEM((t, wr), f32),
                        pltpu.VMEM((t, wr), f32), pltpu.VMEM((t, wr), f32), pltpu.VMEM((1, wr), f32)],
        compiler_params=_params(("arbitrary", "arbitrary"), nbytes),
        name="conv_rglru",
    )(rg, rg, conv_w, row3(conv_b), w_a, row3(b_a), w_i, row3(b_i), row3(lam), row3(gain))


def _router_kernel(x_ref, g_ref, rwt_ref, rb_ref, hp_ref, idx_ref, gate_ref, rank_ref, cnt_ref, carry_ref,
                   *, tb, n_exp):
    step = pl.program_id(0)

    @pl.when(step == 0)
    def _reset():
        carry_ref[...] = jnp.zeros_like(carry_ref)

    x = x_ref[...]
    inv = lax.rsqrt(jnp.mean(x * x, axis=-1, keepdims=True) + EPS)
    h = x * inv * g_ref[...]
    half = h.shape[1] // 2
    bits = lax.bitcast_convert_type(h.astype(bf16).astype(f32), jnp.uint32)
    hp_ref[...] = (bits[:, :half] >> 16) | bits[:, half:]

    logits = lax.dot_general(rwt_ref[...], h, (((1,), (1,)), ((), ())), preferred_element_type=f32,
                             precision=lax.Precision.HIGHEST) + rb_ref[...]
    e_iota = lax.broadcasted_iota(jnp.int32, (n_exp, tb), 0)
    tri = (lax.broadcasted_iota(jnp.int32, (tb, tb), 0) < lax.broadcasted_iota(jnp.int32, (tb, tb), 1)).astype(bf16)
    base = carry_ref[...]
    vals, idxs, ranks = [], [], []
    l = logits
    for _ in range(TOP_K):
        m = l.max(axis=0, keepdims=True)
        idx = jnp.min(jnp.where(l == m, e_iota, n_exp), axis=0, keepdims=True)
        onehot = e_iota == idx
        ohf = onehot.astype(f32)
        before = jnp.dot(ohf.astype(bf16), tri, preferred_element_type=f32)
        ranks.append(jnp.sum(jnp.where(onehot, before + base, 0.0), axis=0, keepdims=True))
        base = base + jnp.sum(ohf, axis=1, keepdims=True)
        vals.append(m)
        idxs.append(idx)
        l = jnp.where(onehot, -jnp.inf, l)
    carry_ref[...] = base
    exps = [jnp.exp(v - vals[0]) for v in vals]
    denom = exps[0]
    for e in exps[1:]:
        denom = denom + e
    idx_ref[...] = jnp.concatenate(idxs, axis=0)
    gate_ref[...] = jnp.concatenate([e / denom for e in exps], axis=0)
    rank_ref[...] = jnp.concatenate(ranks, axis=0).astype(jnp.int32)
    cnt_ref[...] = jnp.broadcast_to(base, cnt_ref.shape).astype(jnp.int32)


def router(x2d, gain_l, router_w_l, router_b_l, layer):
    n, d = x2d.shape
    n_exp = router_w_l.shape[-1]
    tb = min(256, n)
    rwt = jnp.swapaxes(router_w_l[layer], 0, 1)
    rb = router_b_l[layer].reshape(n_exp, 1)
    slot_spec = pl.BlockSpec((TOP_K, tb), lambda i: (0, i))
    nbytes = 2 * tb * d * 4 + 2 * tb * d * 2 + 2 * n_exp * d * 4 + 6 * tb * d * 4
    return pl.pallas_call(
        functools.partial(_router_kernel, tb=tb, n_exp=n_exp),
        grid=(n // tb,),
        in_specs=[pl.BlockSpec((tb, d), lambda i: (i, 0)),
                  pl.BlockSpec((None, 1, d), lambda i: (layer, 0, 0)),
                  pl.BlockSpec((n_exp, d), lambda i: (0, 0)),
                  pl.BlockSpec((n_exp, 1), lambda i: (0, 0))],
        out_specs=[pl.BlockSpec((tb, d // 2), lambda i: (i, 0)), slot_spec, slot_spec, slot_spec,
                   pl.BlockSpec((n_exp, V7X_LANES), lambda i: (0, 0))],
        out_shape=[jax.ShapeDtypeStruct((n, d // 2), jnp.uint32),
                   jax.ShapeDtypeStruct((TOP_K, n), jnp.int32),
                   jax.ShapeDtypeStruct((TOP_K, n), f32),
                   jax.ShapeDtypeStruct((TOP_K, n), jnp.int32),
                   jax.ShapeDtypeStruct((n_exp, V7X_LANES), jnp.int32)],
        scratch_shapes=[pltpu.VMEM((n_exp, 1), f32)],
        compiler_params=_params(("arbitrary",), nbytes),
        name="router_topk",
    )(x2d, gain_l.reshape(gain_l.shape[0], 1, d), rwt, rb)


def _moe_kernel(tile_e_ref, tile_rows_ref, n_used_ref,
                tok_hbm, hp_hbm, wgu_ref, bgu_ref, wdn_ref, bdn_ref,
                y_ref,
                tok_smem, stage_ref, xs_ref, act_ref, wgu_bf, wdn_bf, tok_sem, row_sem,
                *, tm, sub, n_gu, n_dn, tgu, t_max):
    t = pl.program_id(0)
    s = pl.program_id(1)
    n_used = n_used_ref[0]
    active = t < n_used
    half = stage_ref.shape[1]
    th = tgu // 2
    n_sb = tm // sub
    chunk = tm // (n_gu * n_sb)
    n_chunks = n_gu * n_sb
    ns = (tile_rows_ref[t] + sub - 1) // sub
    nxt_slot = (t + 1) % 2

    def tok_copy(tile, slot):
        return pltpu.make_async_copy(tok_hbm.at[tile], tok_smem.at[slot], tok_sem)

    def row_copy(tok, r):
        return pltpu.make_async_copy(hp_hbm.at[pl.ds(tok, 1), :], stage_ref.at[pl.ds(r, 1), :], row_sem)

    def issue_rows_rolled(slot, r0, count):
        def issue(r, c):
            row_copy(tok_smem[slot, r0 + r], r0 + r).start()
            return c

        lax.fori_loop(0, count, issue, 0)

    def wait_gather():
        pltpu.make_async_copy(hp_hbm.at[pl.ds(0, tm), :], stage_ref, row_sem).wait()

    @pl.when(active & (s == 0))
    def _stage_rows():
        @pl.when(t == 0)
        def _first():
            tok_copy(0, 0).start()
            tok_copy(0, 0).wait()
            issue_rows_rolled(0, 0, tm)

        tok_copy(jnp.minimum(t + 1, t_max - 1), nxt_slot).start()
        wait_gather()

        def unpack(rb, c):
            r0 = pl.multiple_of(rb * sub, sub)
            w = stage_ref[pl.ds(r0, sub), :]
            xs_ref[pl.ds(r0, sub), 0:half] = lax.bitcast_convert_type(w << 16, f32).astype(bf16)
            xs_ref[pl.ds(r0, sub), half:2 * half] = lax.bitcast_convert_type(w & jnp.uint32(0xFFFF0000), f32).astype(bf16)
            return c

        lax.fori_loop(0, ns, unpack, 0)
        tok_copy(0, nxt_slot).wait()

    @pl.when(active & (s < n_gu))
    def _gate_up():
        wgu_bf[...] = wgu_ref[...].astype(bf16)
        sel = (lax.broadcasted_iota(jnp.int32, (tgu, th), 0) == 2 * lax.broadcasted_iota(jnp.int32, (tgu, th), 1)).astype(bf16)

        def body(rb, c):
            r0 = pl.multiple_of(rb * sub, sub)
            g0 = (s * n_sb + rb) * chunk
            for r in range(chunk):
                row_copy(tok_smem[nxt_slot, g0 + r], g0 + r).start()
            gu = jnp.dot(xs_ref[pl.ds(r0, sub), :], wgu_bf[...], preferred_element_type=f32) + bgu_ref[...]
            glu = jnp.minimum(gu, SWIGLU_LIMIT)
            glu = glu * _sigmoid(SWIGLU_ALPHA * glu)
            lin = jnp.clip(gu, -SWIGLU_LIMIT, SWIGLU_LIMIT) + 1.0
            prod = glu * pltpu.roll(lin, tgu - 1, 1)
            act_ref[s, pl.ds(r0, sub), :] = jnp.dot(prod.astype(bf16), sel, preferred_element_type=f32).astype(bf16)
            return c

        lax.fori_loop(0, ns, body, 0)

    @pl.when(active & (s == n_gu) & (ns < n_sb))
    def _issue_skipped_chunks():
        def per_chunk(c, carry):
            @pl.when(c % n_sb >= ns)
            def _():
                issue_rows_rolled(nxt_slot, c * chunk, chunk)

            return carry

        lax.fori_loop(0, n_chunks, per_chunk, 0)

    @pl.when(active & (s >= n_gu))
    def _down():
        wdn_bf[...] = wdn_ref[...].astype(bf16)

        def body(rb, c):
            r0 = pl.multiple_of(rb * sub, sub)
            acc = jnp.zeros((sub, y_ref.shape[1]), f32) + bdn_ref[...]
            for g in range(n_gu):
                acc = acc + jnp.dot(act_ref[g, pl.ds(r0, sub), :], wdn_bf[g * th:(g + 1) * th, :],
                                    preferred_element_type=f32)
            y_ref[pl.ds(r0, sub), :] = acc
            return c

        lax.fori_loop(0, ns, body, 0)

        def clear(rb, c):
            r0 = pl.multiple_of(rb * sub, sub)
            y_ref[pl.ds(r0, sub), :] = jnp.zeros((sub, y_ref.shape[1]), f32)
            return c

        lax.fori_loop(ns, n_sb, clear, 0)

    @pl.when((t == n_used - 1) & (s == n_gu + n_dn - 1))
    def _drain():
        wait_gather()


def moe_experts(hp, tok_tiles, tile_e, tile_rows, n_used, w_gu, b_gu, w_dn, b_dn, layer, tm, sub):
    n, half = hp.shape
    d = 2 * half
    n_layers, n_exp, _, two_f = w_gu.shape
    ff = two_f // 2
    t_max = tok_tiles.shape[0]
    tgu = min(512, two_f)
    tdn = min(512, d)
    n_gu, n_dn = two_f // tgu, d // tdn

    def eff(t, s, te, nu):
        live = t < nu[0]
        return jnp.where(live, t, nu[0] - 1), live

    def gu_idx(t, s, te, tr, nu):
        _, live = eff(t, s, te, nu)
        return (layer, te[t], 0, jnp.where(live, jnp.minimum(s, n_gu - 1), n_gu - 1))

    def dn_idx(t, s, te, tr, nu):
        _, live = eff(t, s, te, nu)
        return (layer, te[t], 0, jnp.where(live, jnp.maximum(s - n_gu, 0), n_dn - 1))

    def y_idx(t, s, te, tr, nu):
        tt, live = eff(t, s, te, nu)
        return (tt, jnp.where(live, jnp.maximum(s - n_gu, 0), n_dn - 1))

    nbytes = (tm * half * 4 + tm * d * 2 + tm * ff * 2 + 2 * d * tgu * 4 + d * tgu * 2
              + 2 * ff * tdn * 4 + ff * tdn * 2 + 2 * tm * tdn * 4)
    grid_spec = pltpu.PrefetchScalarGridSpec(
        num_scalar_prefetch=3,
        grid=(t_max, n_gu + n_dn),
        in_specs=[pl.BlockSpec(memory_space=pl.ANY),
                  pl.BlockSpec(memory_space=pl.ANY),
                  pl.BlockSpec((None, None, d, tgu), gu_idx),
                  pl.BlockSpec((None, None, 1, tgu), gu_idx),
                  pl.BlockSpec((None, None, ff, tdn), dn_idx),
                  pl.BlockSpec((None, None, 1, tdn), dn_idx)],
        out_specs=pl.BlockSpec((tm, tdn), y_idx),
        scratch_shapes=[pltpu.SMEM((2, tm), jnp.int32),
                        pltpu.VMEM((tm, half), jnp.uint32),
                        pltpu.VMEM((tm, d), bf16),
                        pltpu.VMEM((n_gu, tm, tgu // 2), bf16),
                        pltpu.VMEM((d, tgu), bf16),
                        pltpu.VMEM((ff, tdn), bf16),
                        pltpu.SemaphoreType.DMA(()),
                        pltpu.SemaphoreType.DMA(())],
    )
    return pl.pallas_call(
        functools.partial(_moe_kernel, tm=tm, sub=sub, n_gu=n_gu, n_dn=n_dn, tgu=tgu, t_max=t_max),
        grid_spec=grid_spec,
        out_shape=jax.ShapeDtypeStruct((t_max * tm, d), f32),
        compiler_params=_params(("arbitrary", "arbitrary"), nbytes),
        name="moe_experts",
    )(tile_e, tile_rows, n_used, tok_tiles, hp, w_gu,
      b_gu.reshape(n_layers, n_exp, 1, two_f), w_dn, b_dn.reshape(n_layers, n_exp, 1, d))


def _combine_kernel(dest_hbm, y_hbm, x_ref, gate_ref, gain_ref, *refs, tb, n_steps, final):
    if final:
        o_ref, dest_smem, ybuf, dest_sem, row_sem = refs
        xo_ref = None
    else:
        xo_ref, o_ref, dest_smem, ybuf, dest_sem, row_sem = refs
    i = pl.program_id(0)
    n_rows = TOP_K * tb

    def dest_copy(step):
        return pltpu.make_async_copy(dest_hbm.at[step], dest_smem.at[step % 2], dest_sem)

    def row_copy(step, r, row):
        return pltpu.make_async_copy(y_hbm.at[pl.ds(row, 1), :], ybuf.at[step % 2, pl.ds(r, 1), :], row_sem.at[step % 2])

    def start_gather(step):
        def issue(r, c):
            row_copy(step, r, dest_smem[step % 2, r]).start()
            return c

        lax.fori_loop(0, n_rows, issue, 0, unroll=8)

    @pl.when(i == 0)
    def _first():
        dest_copy(0).start()
        dest_copy(0).wait()
        start_gather(0)

        if n_steps > 1:
            dest_copy(1).start()

    @pl.when(i + 1 < n_steps)
    def _next():
        dest_copy(i + 1).wait()
        start_gather(i + 1)

        @pl.when(i + 2 < n_steps)
        def _():
            dest_copy(i + 2).start()

    pltpu.make_async_copy(y_hbm.at[pl.ds(0, n_rows), :], ybuf.at[i % 2], row_sem.at[i % 2]).wait()

    g = gate_ref[...]
    acc = x_ref[...]
    for k in range(TOP_K):
        acc = acc + g[:, k:k + 1] * ybuf[i % 2, k * tb:(k + 1) * tb, :]
    if not final:
        xo_ref[...] = acc
    inv = lax.rsqrt(jnp.mean(acc * acc, axis=-1, keepdims=True) + EPS)
    o_ref[...] = (acc * inv * gain_ref[...]).astype(o_ref.dtype)


def combine(x2d, y, dest, gates_t, gain2d, final):
    n, d = x2d.shape
    tb = min(128, n)
    n_steps = n // tb
    dest_blk = dest.reshape(TOP_K, n_steps, tb).transpose(1, 0, 2).reshape(n_steps, TOP_K * tb)
    row_spec = pl.BlockSpec((tb, d), lambda i: (i, 0))
    if final:
        out_shape = jax.ShapeDtypeStruct((n, d), f32)
        out_specs = row_spec
    else:
        out_shape = [jax.ShapeDtypeStruct((n, d), f32), jax.ShapeDtypeStruct((n, d), bf16)]
        out_specs = [row_spec, row_spec]
    nbytes = 2 * TOP_K * tb * d * 4 + 6 * tb * d * 4
    return pl.pallas_call(
        functools.partial(_combine_kernel, tb=tb, n_steps=n_steps, final=final),
        grid=(n_steps,),
        in_specs=[pl.BlockSpec(memory_space=pl.ANY),
                  pl.BlockSpec(memory_space=pl.ANY),
                  row_spec,
                  pl.BlockSpec((tb, TOP_K), lambda i: (i, 0)),
                  pl.BlockSpec((1, d), lambda i: (0, 0))],
        out_specs=out_specs,
        out_shape=out_shape,
        scratch_shapes=[pltpu.SMEM((2, TOP_K * tb), jnp.int32),
                        pltpu.VMEM((2, TOP_K * tb, d), f32),
                        pltpu.SemaphoreType.DMA(()),
                        pltpu.SemaphoreType.DMA((2,))],
        compiler_params=_params(("arbitrary",), nbytes),
        name="moe_combine",
    )(dest_blk, y, x2d, gates_t, gain2d)


def _dispatch_plan(counts, idx, rank, tm, t_max):
    n_exp = counts.shape[0]
    n = idx.shape[1]
    tiles_e = (counts + tm - 1) // tm
    tile_end = jnp.cumsum(tiles_e)
    tile_start = tile_end - tiles_e
    n_used = tile_end[-1]
    expert = jnp.arange(n_exp, dtype=jnp.int32)[:, None, None]
    dest = rank + jnp.sum(jnp.where(idx[None] == expert, (tile_start * tm)[:, None, None], 0), axis=0)
    tile = jnp.arange(t_max, dtype=jnp.int32)
    tile_c = jnp.minimum(tile, n_used - 1)
    tile_e = jnp.minimum(jnp.searchsorted(tile_end, tile_c, side="right"), n_exp - 1).astype(jnp.int32)
    rows = jnp.clip(counts[tile_e] - (tile_c - tile_start[tile_e]) * tm, 0, tm)
    tile_rows = jnp.where(tile < n_used, rows, 0).astype(jnp.int32)
    tok = jnp.broadcast_to(jnp.arange(n, dtype=jnp.int32)[None, :], (TOP_K, n))
    tok_tiles = jnp.zeros((t_max * tm,), jnp.int32).at[dest.reshape(-1)].set(tok.reshape(-1))
    return dest, tok_tiles.reshape(t_max, tm), tile_e, tile_rows, n_used.reshape(1).astype(jnp.int32)


def kernel(x, norm_mix, w_in, attn_rel_bias, attn_out_gain, conv_w, conv_b, lru_w_a, lru_b_a, lru_w_i, lru_b_i,
           lru_lambda, rec_out_gain, w_out, norm_ffn, router_w, router_b, expert_w_gate_up, expert_b_gate_up,
           expert_w_down, expert_b_down, norm_final):
    batch, seq, d = x.shape
    n = batch * seq
    depth = norm_mix.shape[0]
    attn_width = attn_out_gain.shape[1]
    lru_width = rec_out_gain.shape[1]
    n_exp = router_w.shape[-1]
    tm = min(1024, n)
    sub = min(512, tm)
    t_max = (n * TOP_K) // tm + n_exp

    x2d = x.reshape(n, d)
    h = rmsnorm_bf16(x2d, norm_mix, 0)
    out = None
    for l in range(depth):
        qkv = matmul_ws([h], w_in, l, [0], 0, 3 * attn_width, bf16, name="in_proj_qkv")
        rg = matmul_ws([h], w_in, l, [0], 3 * attn_width, 2 * lru_width, f32, name="in_proj_rec")
        attn = attention(qkv, attn_rel_bias[l], attn_out_gain, l, batch, seq, attn_width)
        rec = rglru_block(rg, conv_w, conv_b, lru_w_a, lru_b_a, lru_w_i, lru_b_i, lru_lambda, rec_out_gain,
                          l, batch, seq)
        assert attn_width == lru_width
        x2d = matmul_ws([attn, rec], w_out, l, [0, 1], 0, d, f32, residual=x2d, name="out_proj")
        hp, idx, gates, rank, cnt = router(x2d, norm_ffn, router_w, router_b, l)
        dest, tok_tiles, tile_e, tile_rows, n_used = _dispatch_plan(cnt[:, 0], idx, rank, tm, t_max)
        y = moe_experts(hp, tok_tiles, tile_e, tile_rows, n_used, expert_w_gate_up, expert_b_gate_up,
                        expert_w_down, expert_b_down, l, tm, sub)
        final = l == depth - 1
        gain = norm_final.reshape(1, d) if final else norm_mix[l + 1].reshape(1, d)
        res = combine(x2d, y, dest, gates.T, gain, final)
        if final:
            out = res
        else:
            x2d, h = res
    return out.reshape(batch, seq, d)
```

```python
import functools
import math

import jax
import jax.numpy as jnp
from jax import lax
from jax.experimental import pallas as pl
from jax.experimental.pallas import tpu as pltpu

CHUNK = 64
LEFT_CHUNKS = 8
ATTN_HEAD_DIM = 128
REL_CLIP = 128
LRU_C = 8.0
TOP_K = 4
SWIGLU_LIMIT = 7.0
SWIGLU_ALPHA = 1.702
EPS = 1e-6

V7X_LANES = 128
V7X_SUBLANES = 8
V7X_VMEM_BYTES = 64 * 1024 * 1024

MASK_VALUE = -1e30

f32 = jnp.float32
bf16 = jnp.bfloat16


def _vmem_limit(nbytes):
    return int(min(nbytes + 12 * 1024 * 1024, V7X_VMEM_BYTES - 6 * 1024 * 1024))


def _params(sem, nbytes):
    return pltpu.CompilerParams(dimension_semantics=sem, vmem_limit_bytes=_vmem_limit(nbytes))


def _rmsnorm_kernel(x_ref, g_ref, o_ref):
    x = x_ref[...]
    inv = lax.rsqrt(jnp.mean(x * x, axis=-1, keepdims=True) + EPS)
    o_ref[...] = (x * inv * g_ref[...]).astype(o_ref.dtype)


def rmsnorm_bf16(x2d, gain_l, layer):
    n, d = x2d.shape
    tb = min(256, n)
    return pl.pallas_call(
        _rmsnorm_kernel,
        grid=(n // tb,),
        in_specs=[pl.BlockSpec((tb, d), lambda i: (i, 0)),
                  pl.BlockSpec((None, 1, d), lambda i: (layer, 0, 0))],
        out_specs=pl.BlockSpec((tb, d), lambda i: (i, 0)),
        out_shape=jax.ShapeDtypeStruct((n, d), bf16),
        compiler_params=_params(("arbitrary",), 2 * tb * d * 6),
        name="rmsnorm_bf16",
    )(x2d, gain_l.reshape(gain_l.shape[0], 1, d))


def _matmul_kernel(*refs, n_pairs, has_res):
    a_refs = refs[:n_pairs]
    w_refs = refs[n_pairs:2 * n_pairs]
    pos = 2 * n_pairs
    res_ref = refs[pos] if has_res else None
    pos += int(has_res)
    o_ref = refs[pos]
    wbf_refs = refs[pos + 1:pos + 1 + n_pairs]

    @pl.when(pl.program_id(1) == 0)
    def _cast_weights():
        for w_ref, wbf_ref in zip(w_refs, wbf_refs):
            wbf_ref[...] = w_ref[...].astype(bf16)

    acc = jnp.dot(a_refs[0][...], wbf_refs[0][...], preferred_element_type=f32)
    for a_ref, wbf_ref in zip(a_refs[1:], wbf_refs[1:]):
        acc += jnp.dot(a_ref[...], wbf_ref[...], preferred_element_type=f32)
    if has_res:
        acc += res_ref[...]
    o_ref[...] = acc.astype(o_ref.dtype)


def matmul_ws(a_list, w_stack, layer, k_blocks, col0, n_cols, out_dtype, residual=None, name="matmul_ws"):
    m = a_list[0].shape[0]
    tm = min(1024, m)
    tn = min(512, n_cols)
    assert m % tm == 0 and n_cols % tn == 0 and col0 % tn == 0
    cb0 = col0 // tn
    n_pairs = len(a_list)
    in_specs, scratch = [], []
    nbytes = 2 * tm * tn * 4 * (2 if residual is not None else 1)
    for a in a_list:
        kp = a.shape[1]
        in_specs.append(pl.BlockSpec((tm, kp), lambda j, i: (i, 0)))
        nbytes += 2 * tm * kp * 2
    for a, kb in zip(a_list, k_blocks):
        kp = a.shape[1]
        in_specs.append(pl.BlockSpec((None, kp, tn), functools.partial(lambda j, i, kb: (layer, kb, j + cb0), kb=kb)))
        scratch.append(pltpu.VMEM((kp, tn), bf16))
        nbytes += 2 * kp * tn * 4 + kp * tn * 2
    args = list(a_list) + [w_stack] * n_pairs
    if residual is not None:
        in_specs.append(pl.BlockSpec((tm, tn), lambda j, i: (i, j)))
        args.append(residual)
    return pl.pallas_call(
        functools.partial(_matmul_kernel, n_pairs=n_pairs, has_res=residual is not None),
        grid=(n_cols // tn, m // tm),
        in_specs=in_specs,
        out_specs=pl.BlockSpec((tm, tn), lambda j, i: (i, j)),
        out_shape=jax.ShapeDtypeStruct((m, n_cols), out_dtype),
        scratch_shapes=scratch,
        compiler_params=_params(("arbitrary", "arbitrary"), nbytes),
        name=name,
    )(*args)


def _attn_kernel(q_ref, *refs, n_kb, hb, tq, n_hg):
    k_refs = refs[:n_kb]
    v_refs = refs[n_kb:2 * n_kb]
    row0_ref, gain_ref, o_ref, raw_ref, bias_ref = refs[2 * n_kb:]
    i = pl.program_id(1)
    hg = pl.program_id(2)
    dh = ATTN_HEAD_DIM
    scale = 1.0 / math.sqrt(dh)
    n_keys = n_kb * tq

    @pl.when((pl.program_id(0) == 0) & (i == 0) & (hg == 0))
    def _build_bias():
        width = row0_ref.shape[-1]
        qc = ((n_kb - 1) * tq + lax.broadcasted_iota(jnp.int32, (tq, n_keys), 0)) // CHUNK
        kc = lax.broadcasted_iota(jnp.int32, (tq, n_keys), 1) // CHUNK
        valid = (kc <= qc) & (kc >= qc - LEFT_CHUNKS)
        for h in range(bias_ref.shape[0]):
            rows = pltpu.roll(jnp.broadcast_to(row0_ref[h], (tq, width)), 0, 1, stride=1, stride_axis=0)
            bias_ref[h] = jnp.where(valid, rows[:, :n_keys], MASK_VALUE)

    outs = []
    for hh in range(hb):
        sl = slice(hh * dh, (hh + 1) * dh)
        q = q_ref[:, sl]
        bias = bias_ref[hg * hb + hh]
        parts = []
        for j in range(n_kb - 1, -1, -1):
            s = lax.dot_general(q, k_refs[j][:, sl], (((1,), (1,)), ((), ())), preferred_element_type=f32)
            c0 = (n_kb - 1 - j) * tq
            s = s * scale + bias[:, c0:c0 + tq]
            if j > 0:
                s = s + jnp.where(i >= j, 0.0, MASK_VALUE)
            parts.append(s)
        m = parts[0].max(axis=-1, keepdims=True)
        for s in parts[1:]:
            m = jnp.maximum(m, s.max(axis=-1, keepdims=True))
        l = jnp.zeros_like(m)
        acc = jnp.zeros((tq, dh), f32)
        for idx, s in enumerate(parts):
            j = n_kb - 1 - idx
            p = jnp.exp(s - m)
            l = l + p.sum(axis=-1, keepdims=True)
            acc = acc + jnp.dot(p.astype(bf16), v_refs[j][:, sl], preferred_element_type=f32)
        outs.append(acc / l)
    raw_ref[hg] = jnp.concatenate(outs, axis=-1)

    @pl.when(hg == n_hg - 1)
    def _finish():
        ss = jnp.zeros((tq, 1), f32)
        for g in range(n_hg):
            r = raw_ref[g]
            ss = ss + jnp.sum(r * r, axis=-1, keepdims=True)
        inv = lax.rsqrt(ss / (n_hg * hb * dh) + EPS)
        for g in range(n_hg):
            w = hb * dh
            o_ref[:, g * w:(g + 1) * w] = (raw_ref[g] * inv * gain_ref[:, g * w:(g + 1) * w]).astype(o_ref.dtype)


def _attn_bias_row0(rel_bias_l, tq, n_kb):
    n_heads = rel_bias_l.shape[0]
    rb = rel_bias_l.astype(f32)
    far = (n_kb - 1) * tq - REL_CLIP
    near = n_kb * tq - far - (2 * REL_CLIP + 1)
    assert far >= 0 and near >= 0
    fill = lambda col, width: jnp.broadcast_to(rb[:, col:col + 1], (n_heads, width))
    row0 = jnp.concatenate([fill(2 * REL_CLIP, far), rb[:, ::-1], fill(0, near), fill(2 * REL_CLIP, tq)], axis=1)
    return row0.reshape(n_heads, 1, (n_kb + 1) * tq)


def attention(qkv, rel_bias_l, gain_l, layer, batch, seq, attn_width):
    n = qkv.shape[0]
    n_heads = attn_width // ATTN_HEAD_DIM
    tq = min(256, seq)
    band = LEFT_CHUNKS * CHUNK
    assert band % tq == 0 and seq % tq == 0 and tq % CHUNK == 0
    n_kb = band // tq + 1
    nq = seq // tq
    hb = min(4, n_heads)
    n_hg = n_heads // hb
    w = hb * ATTN_HEAD_DIM
    row0 = _attn_bias_row0(rel_bias_l, tq, n_kb)

    def kv_spec(j, which):
        return pl.BlockSpec((tq, w), lambda b, i, g: (b * nq + jnp.maximum(i - j, 0), which * n_hg + g))

    in_specs = [pl.BlockSpec((tq, w), lambda b, i, g: (b * nq + i, g))]
    in_specs += [kv_spec(j, 1) for j in range(n_kb)] + [kv_spec(j, 2) for j in range(n_kb)]
    in_specs += [pl.BlockSpec(row0.shape, lambda b, i, g: (0, 0, 0)),
                 pl.BlockSpec((None, 1, attn_width), lambda b, i, g: (layer, 0, 0))]
    nbytes = 2 * (2 * n_kb + 1) * tq * w * 2 + n_heads * tq * n_kb * tq * 4 + 3 * tq * attn_width * 4
    return pl.pallas_call(
        functools.partial(_attn_kernel, n_kb=n_kb, hb=hb, tq=tq, n_hg=n_hg),
        grid=(batch, nq, n_hg),
        in_specs=in_specs,
        out_specs=pl.BlockSpec((tq, attn_width), lambda b, i, g: (b * nq + i, 0)),
        out_shape=jax.ShapeDtypeStruct((n, attn_width), bf16),
        scratch_shapes=[pltpu.VMEM((n_hg, tq, w), f32), pltpu.VMEM((n_heads, tq, n_kb * tq), f32)],
        compiler_params=_params(("arbitrary", "arbitrary", "arbitrary"), nbytes),
        name="band_attention",
    )(*([qkv] * (2 * n_kb + 1)), row0, gain_l.reshape(gain_l.shape[0], 1, attn_width))


def _sigmoid(x):
    return 1.0 / (1.0 + jnp.exp(-x))


def _lru_kernel(xr_ref, xg_ref, cw_ref, cb_ref, wa_ref, ba_ref, wi_ref, bi_ref, lam_ref, gain_ref,
                o_ref, xe_ref, a_ref, b_ref, h_ref, carry_ref, *, t, n_blocks, conv_width):
    i = pl.program_id(1)
    hdr = V7X_SUBLANES
    bw = wa_ref.shape[-1]

    @pl.when(i == 0)
    def _reset():
        xe_ref[0:hdr, :] = jnp.zeros((hdr, xe_ref.shape[1]), f32)
        carry_ref[...] = jnp.zeros_like(carry_ref)

    x = xr_ref[...]
    xe_ref[hdr:hdr + t, :] = x
    u = cb_ref[...] + cw_ref[conv_width - 1:conv_width, :] * x
    for j in range(conv_width - 1):
        u = u + cw_ref[j:j + 1, :] * xe_ref[pl.ds(hdr - (conv_width - 1) + j, t), :]
    xe_ref[0:hdr, :] = x[t - hdr:t, :]

    lam = lam_ref[...]
    nsp = -LRU_C * (jnp.maximum(-lam, 0.0) + jnp.log1p(jnp.exp(-jnp.abs(lam))))
    for nb in range(n_blocks):
        sl = slice(nb * bw, (nb + 1) * bw)
        ub = u[:, sl]
        ubb = ub.astype(bf16)
        r = _sigmoid(jnp.dot(ubb, wa_ref[nb].astype(bf16), preferred_element_type=f32) + ba_ref[:, sl])
        ig = _sigmoid(jnp.dot(ubb, wi_ref[nb].astype(bf16), preferred_element_type=f32) + bi_ref[:, sl])
        log_a = r * nsp[:, sl]
        a = jnp.exp(log_a)
        a_ref[:, sl] = a
        b_ref[:, sl] = jnp.sqrt(1.0 - a * a) * (ig * ub)

    a = a_ref[...]
    b = b_ref[...]
    row = lax.broadcasted_iota(jnp.int32, a.shape, 0) % hdr
    s = 1
    while s < hdr:
        keep = row >= s
        a_sh = jnp.where(keep, pltpu.roll(a, s, 0), 1.0)
        b_sh = jnp.where(keep, pltpu.roll(b, s, 0), 0.0)
        b = a * b_sh + b
        a = a * a_sh
        s *= 2
    a_ref[...] = a
    b_ref[...] = b

    def tile_step(j, carry):
        r0 = pl.multiple_of(j * hdr, hdr)
        h = a_ref[pl.ds(r0, hdr), :] * carry + b_ref[pl.ds(r0, hdr), :]
        h_ref[pl.ds(r0, hdr), :] = h
        return h[hdr - 1:hdr, :]

    carry_ref[...] = lax.fori_loop(0, t // hdr, tile_step, carry_ref[...])

    y = h_ref[...] * jax.nn.gelu(xg_ref[...], approximate=True)
    inv = lax.rsqrt(jnp.mean(y * y, axis=-1, keepdims=True) + EPS)
    o_ref[...] = (y * inv * gain_ref[...]).astype(o_ref.dtype)


def rglru_block(rg, conv_w, conv_b, w_a, b_a, w_i, b_i, lam, gain, layer, batch, seq):
    n = rg.shape[0]
    n_layers, conv_width, wr = conv_w.shape
    n_blocks = w_a.shape[1]
    bw = w_a.shape[2]
    t = min(256, seq)
    ns = seq // t
    row3 = lambda v: v.reshape(n_layers, 1, wr)
    vec_spec = pl.BlockSpec((None, 1, wr), lambda b, i: (layer, 0, 0))
    wspec = pl.BlockSpec((None, n_blocks, bw, bw), lambda b, i: (layer, 0, 0, 0))
    nbytes = 2 * 2 * t * wr * 4 + 2 * t * wr * 2 + 4 * (t + 8) * wr * 4 + 4 * n_blocks * bw * bw * 4 + 8 * t * wr * 4
    return pl.pallas_call(
        functools.partial(_lru_kernel, t=t, n_blocks=n_blocks, conv_width=conv_width),
        grid=(batch, ns),
        in_specs=[pl.BlockSpec((t, wr), lambda b, i: (b * ns + i, 0)),
                  pl.BlockSpec((t, wr), lambda b, i: (b * ns + i, 1)),
                  pl.BlockSpec((None, conv_width, wr), lambda b, i: (layer, 0, 0)),
                  vec_spec, wspec, vec_spec, wspec, vec_spec, vec_spec, vec_spec],
        out_specs=pl.BlockSpec((t, wr), lambda b, i: (b * ns + i, 0)),
        out_shape=jax.ShapeDtypeStruct((n, wr), bf16),
        scratch_shapes=[pltpu.VMEM((t + V7X_SUBLANES, wr), f32), pltpu.VMEM((t, wr), f32),
                        pltpu.VMEM((t, wr), f32), pltpu.VMEM((t, wr), f32), pltpu.VMEM((1, wr), f32)],
        compiler_params=_params(("arbitrary", "arbitrary"), nbytes),
        name="conv_rglru",
    )(rg, rg, conv_w, row3(conv_b), w_a, row3(b_a), w_i, row3(b_i), row3(lam), row3(gain))


def _router_kernel(x_ref, g_ref, rwt_ref, rb_ref, hp_ref, idx_ref, gate_ref, rank_ref, cnt_ref, carry_ref,
                   *, tb, n_exp):
    step = pl.program_id(0)

    @pl.when(step == 0)
    def _reset():
        carry_ref[...] = jnp.zeros_like(carry_ref)

    x = x_ref[...]
    inv = lax.rsqrt(jnp.mean(x * x, axis=-1, keepdims=True) + EPS)
    h = x * inv * g_ref[...]
    half = h.shape[1] // 2
    bits = lax.bitcast_convert_type(h.astype(bf16).astype(f32), jnp.uint32)
    hp_ref[...] = (bits[:, :half] >> 16) | bits[:, half:]

    logits = lax.dot_general(rwt_ref[...], h, (((1,), (1,)), ((), ())), preferred_element_type=f32,
                             precision=lax.Precision.HIGHEST) + rb_ref[...]
    e_iota = lax.broadcasted_iota(jnp.int32, (n_exp, tb), 0)
    tri = (lax.broadcasted_iota(jnp.int32, (tb, tb), 0) < lax.broadcasted_iota(jnp.int32, (tb, tb), 1)).astype(bf16)
    base = carry_ref[...]
    vals, idxs, ranks = [], [], []
    l = logits
    for _ in range(TOP_K):
        m = l.max(axis=0, keepdims=True)
        idx = jnp.min(jnp.where(l == m, e_iota, n_exp), axis=0, keepdims=True)
        onehot = e_iota == idx
        ohf = onehot.astype(f32)
        before = jnp.dot(ohf.astype(bf16), tri, preferred_element_type=f32)
        ranks.append(jnp.sum(jnp.where(onehot, before + base, 0.0), axis=0, keepdims=True))
        base = base + jnp.sum(ohf, axis=1, keepdims=True)
        vals.append(m)
        idxs.append(idx)
        l = jnp.where(onehot, -jnp.inf, l)
    carry_ref[...] = base
    exps = [jnp.exp(v - vals[0]) for v in vals]
    denom = exps[0]
    for e in exps[1:]:
        denom = denom + e
    idx_ref[...] = jnp.concatenate(idxs, axis=0)
    gate_ref[...] = jnp.concatenate([e / denom for e in exps], axis=0)
    rank_ref[...] = jnp.concatenate(ranks, axis=0).astype(jnp.int32)
    cnt_ref[...] = jnp.broadcast_to(base, cnt_ref.shape).astype(jnp.int32)


def router(x2d, gain_l, router_w_l, router_b_l, layer):
    n, d = x2d.shape
    n_exp = router_w_l.shape[-1]
    tb = min(256, n)
    rwt = jnp.swapaxes(router_w_l[layer], 0, 1)
    rb = router_b_l[layer].reshape(n_exp, 1)
    slot_spec = pl.BlockSpec((TOP_K, tb), lambda i: (0, i))
    nbytes = 2 * tb * d * 4 + 2 * tb * d * 2 + 2 * n_exp * d * 4 + 6 * tb * d * 4
    return pl.pallas_call(
        functools.partial(_router_kernel, tb=tb, n_exp=n_exp),
        grid=(n // tb,),
        in_specs=[pl.BlockSpec((tb, d), lambda i: (i, 0)),
                  pl.BlockSpec((None, 1, d), lambda i: (layer, 0, 0)),
                  pl.BlockSpec((n_exp, d), lambda i: (0, 0)),
                  pl.BlockSpec((n_exp, 1), lambda i: (0, 0))],
        out_specs=[pl.BlockSpec((tb, d // 2), lambda i: (i, 0)), slot_spec, slot_spec, slot_spec,
                   pl.BlockSpec((n_exp, V7X_LANES), lambda i: (0, 0))],
        out_shape=[jax.ShapeDtypeStruct((n, d // 2), jnp.uint32),
                   jax.ShapeDtypeStruct((TOP_K, n), jnp.int32),
                   jax.ShapeDtypeStruct((TOP_K, n), f32),
                   jax.ShapeDtypeStruct((TOP_K, n), jnp.int32),
                   jax.ShapeDtypeStruct((n_exp, V7X_LANES), jnp.int32)],
        scratch_shapes=[pltpu.VMEM((n_exp, 1), f32)],
        compiler_params=_params(("arbitrary",), nbytes),
        name="router_topk",
    )(x2d, gain_l.reshape(gain_l.shape[0], 1, d), rwt, rb)


def _moe_kernel(tile_e_ref, tile_rows_ref, n_used_ref,
                xs_hbm, wgu_ref, bgu_ref, wdn_ref, bdn_ref,
                y_ref,
                stage_ref, xs_ref, act_ref, wgu_bf, wdn_bf, row_sem,
                *, tm, sub, n_gu, n_dn, tgu, t_max):
    t = pl.program_id(0)
    s = pl.program_id(1)
    n_used = n_used_ref[0]
    active = t < n_used
    half = stage_ref.shape[1]
    th = tgu // 2
    n_sb = tm // sub
    rows = tile_rows_ref[t]
    ns = (rows + sub - 1) // sub

    def sub_copy(tile, rb):
        r0 = pl.multiple_of(rb * sub, sub)
        return pltpu.make_async_copy(xs_hbm.at[pl.ds(tile * tm + r0, sub), :], stage_ref.at[pl.ds(r0, sub), :],
                                     row_sem.at[rb])

    def start_tile(tile):
        n_next = (tile_rows_ref[tile] + sub - 1) // sub

        def issue(rb, c):
            sub_copy(tile, rb).start()
            return c

        lax.fori_loop(0, n_next, issue, 0)

    @pl.when(active & (s == 0))
    def _stage_rows():
        @pl.when(t == 0)
        def _first():
            start_tile(0)

        def unpack(rb, c):
            sub_copy(t, rb).wait()
            r0 = pl.multiple_of(rb * sub, sub)
            live = lax.broadcasted_iota(jnp.int32, (sub, half), 0) < rows - r0
            w = jnp.where(live, stage_ref[pl.ds(r0, sub), :], jnp.uint32(0))
            xs_ref[pl.ds(r0, sub), 0:half] = lax.bitcast_convert_type(w << 16, f32).astype(bf16)
            xs_ref[pl.ds(r0, sub), half:2 * half] = lax.bitcast_convert_type(w & jnp.uint32(0xFFFF0000), f32).astype(bf16)
            return c

        lax.fori_loop(0, ns, unpack, 0)

    @pl.when(active & (s == 1) & (t + 1 < n_used))
    def _prefetch_rows():
        start_tile(jnp.minimum(t + 1, t_max - 1))

    @pl.when(active & (s < n_gu))
    def _gate_up():
        wgu_bf[...] = wgu_ref[...].astype(bf16)
        sel = (lax.broadcasted_iota(jnp.int32, (tgu, th), 0) == 2 * lax.broadcasted_iota(jnp.int32, (tgu, th), 1)).astype(bf16)

        def body(rb, c):
            r0 = pl.multiple_of(rb * sub, sub)
            gu = jnp.dot(xs_ref[pl.ds(r0, sub), :], wgu_bf[...], preferred_element_type=f32) + bgu_ref[...]
            glu = jnp.minimum(gu, SWIGLU_LIMIT)
            glu = glu * _sigmoid(SWIGLU_ALPHA * glu)
            lin = jnp.clip(gu, -SWIGLU_LIMIT, SWIGLU_LIMIT) + 1.0
            prod = glu * pltpu.roll(lin, tgu - 1, 1)
            act_ref[s, pl.ds(r0, sub), :] = jnp.dot(prod.astype(bf16), sel, preferred_element_type=f32).astype(bf16)
            return c

        lax.fori_loop(0, ns, body, 0)

    @pl.when(active & (s >= n_gu))
    def _down():
        wdn_bf[...] = wdn_ref[...].astype(bf16)

        def body(rb, c):
            r0 = pl.multiple_of(rb * sub, sub)
            acc = jnp.zeros((sub, y_ref.shape[1]), f32) + bdn_ref[...]
            for g in range(n_gu):
                acc = acc + jnp.dot(act_ref[g, pl.ds(r0, sub), :], wdn_bf[g * th:(g + 1) * th, :],
                                    preferred_element_type=f32)
            y_ref[pl.ds(r0, sub), :] = acc
            return c

        lax.fori_loop(0, ns, body, 0)

        def clear(rb, c):
            r0 = pl.multiple_of(rb * sub, sub)
            y_ref[pl.ds(r0, sub), :] = jnp.zeros((sub, y_ref.shape[1]), f32)
            return c

        lax.fori_loop(ns, n_sb, clear, 0)


def moe_experts(xs_sorted, tile_e, tile_rows, n_used, w_gu, b_gu, w_dn, b_dn, layer, tm, sub):
    half = xs_sorted.shape[1]
    d = 2 * half
    n_layers, n_exp, _, two_f = w_gu.shape
    ff = two_f // 2
    t_max = xs_sorted.shape[0] // tm
    tgu = min(512, two_f)
    tdn = min(512, d)
    n_gu, n_dn = two_f // tgu, d // tdn

    def eff(t, s, te, nu):
        live = t < nu[0]
        return jnp.where(live, t, jnp.maximum(nu[0] - 1, 0)), live

    def gu_idx(t, s, te, tr, nu):
        _, live = eff(t, s, te, nu)
        return (layer, te[t], 0, jnp.where(live, jnp.minimum(s, n_gu - 1), n_gu - 1))

    def dn_idx(t, s, te, tr, nu):
        _, live = eff(t, s, te, nu)
        return (layer, te[t], 0, jnp.where(live, jnp.maximum(s - n_gu, 0), n_dn - 1))

    def y_idx(t, s, te, tr, nu):
        tt, live = eff(t, s, te, nu)
        return (tt, jnp.where(live, jnp.maximum(s - n_gu, 0), n_dn - 1))

    nbytes = (tm * half * 4 + tm * d * 2 + tm * ff * 2 + 2 * d * tgu * 4 + d * tgu * 2
              + 2 * ff * tdn * 4 + ff * tdn * 2 + 2 * tm * tdn * 4)
    grid_spec = pltpu.PrefetchScalarGridSpec(
        num_scalar_prefetch=3,
        grid=(t_max, n_gu + n_dn),
        in_specs=[pl.BlockSpec(memory_space=pl.ANY),
                  pl.BlockSpec((None, None, d, tgu), gu_idx),
                  pl.BlockSpec((None, None, 1, tgu), gu_idx),
                  pl.BlockSpec((None, None, ff, tdn), dn_idx),
                  pl.BlockSpec((None, None, 1, tdn), dn_idx)],
        out_specs=pl.BlockSpec((tm, tdn), y_idx),
        scratch_shapes=[pltpu.VMEM((tm, half), jnp.uint32),
                        pltpu.VMEM((tm, d), bf16),
                        pltpu.VMEM((n_gu, tm, tgu // 2), bf16),
                        pltpu.VMEM((d, tgu), bf16),
                        pltpu.VMEM((ff, tdn), bf16),
                        pltpu.SemaphoreType.DMA((tm // sub,))],
    )
    return pl.pallas_call(
        functools.partial(_moe_kernel, tm=tm, sub=sub, n_gu=n_gu, n_dn=n_dn, tgu=tgu, t_max=t_max),
        grid_spec=grid_spec,
        out_shape=jax.ShapeDtypeStruct((t_max * tm, d), f32),
        compiler_params=_params(("arbitrary", "arbitrary"), nbytes),
        name="moe_experts",
    )(tile_e, tile_rows, n_used, xs_sorted, w_gu,
      b_gu.reshape(n_layers, n_exp, 1, two_f), w_dn, b_dn.reshape(n_layers, n_exp, 1, d))


def _dispatch_kernel(dest_hbm, hp_ref, xs_hbm, dest_smem, dest_sem, row_sem, *, tb, n_steps):
    i = pl.program_id(0)
    slot = i % 2

    def dest_copy(step):
        return pltpu.make_async_copy(dest_hbm.at[step], dest_smem.at[step % 2], dest_sem)

    def row_copy(r, row):
        return pltpu.make_async_copy(hp_ref.at[pl.ds(r, 1), :], xs_hbm.at[pl.ds(row, 1), :], row_sem)

    @pl.when(i == 0)
    def _first():
        dest_copy(0).start()

    dest_copy(i).wait()

    @pl.when(i + 1 < n_steps)
    def _next():
        dest_copy(i + 1).start()

    def issue(r, c):
        for k in range(TOP_K):
            row_copy(r, dest_smem[slot, k * tb + r]).start()
        return c

    lax.fori_loop(0, tb, issue, 0, unroll=2)
    for _ in range(TOP_K):
        pltpu.make_async_copy(hp_ref, xs_hbm.at[pl.ds(0, tb), :], row_sem).wait()


def dispatch(hp, dest_blk, n_rows_out):
    n, half = hp.shape
    n_steps, per_step = dest_blk.shape
    tb = per_step // TOP_K
    return pl.pallas_call(
        functools.partial(_dispatch_kernel, tb=tb, n_steps=n_steps),
        grid=(n_steps,),
        in_specs=[pl.BlockSpec(memory_space=pl.ANY),
                  pl.BlockSpec((tb, half), lambda i: (i, 0))],
        out_specs=pl.BlockSpec(memory_space=pl.ANY),
        out_shape=jax.ShapeDtypeStruct((n_rows_out, half), jnp.uint32),
        scratch_shapes=[pltpu.SMEM((2, per_step), jnp.int32),
                        pltpu.SemaphoreType.DMA(()),
                        pltpu.SemaphoreType.DMA(())],
        compiler_params=_params(("arbitrary",), 2 * tb * half * 4),
        name="moe_dispatch",
    )(dest_blk, hp)


def _combine_kernel(dest_hbm, y_hbm, x_ref, gate_ref, gain_ref, *refs, tb, n_steps, final):
    if final:
        o_ref, dest_smem, ybuf, dest_sem, row_sem = refs
        xo_ref = None
    else:
        xo_ref, o_ref, dest_smem, ybuf, dest_sem, row_sem = refs
    i = pl.program_id(0)
    n_rows = TOP_K * tb

    def dest_copy(step):
        return pltpu.make_async_copy(dest_hbm.at[step], dest_smem.at[step % 2], dest_sem)

    def row_copy(step, r, row):
        return pltpu.make_async_copy(y_hbm.at[pl.ds(row, 1), :], ybuf.at[step % 2, pl.ds(r, 1), :], row_sem.at[step % 2])

    def start_gather(step):
        def issue(r, c):
            row_copy(step, r, dest_smem[step % 2, r]).start()
            return c

        lax.fori_loop(0, n_rows, issue, 0, unroll=8)

    @pl.when(i == 0)
    def _first():
        dest_copy(0).start()
        dest_copy(0).wait()
        start_gather(0)

        if n_steps > 1:
            dest_copy(1).start()

    @pl.when(i + 1 < n_steps)
    def _next():
        dest_copy(i + 1).wait()
        start_gather(i + 1)

        @pl.when(i + 2 < n_steps)
        def _():
            dest_copy(i + 2).start()

    pltpu.make_async_copy(y_hbm.at[pl.ds(0, n_rows), :], ybuf.at[i % 2], row_sem.at[i % 2]).wait()

    g = gate_ref[...]
    acc = x_ref[...]
    for k in range(TOP_K):
        acc = acc + g[:, k:k + 1] * ybuf[i % 2, k * tb:(k + 1) * tb, :]
    if not final:
        xo_ref[...] = acc
    inv = lax.rsqrt(jnp.mean(acc * acc, axis=-1, keepdims=True) + EPS)
    o_ref[...] = (acc * inv * gain_ref[...]).astype(o_ref.dtype)


def combine(x2d, y, dest_blk, gates_t, gain2d, final):
    n, d = x2d.shape
    n_steps, per_step = dest_blk.shape
    tb = per_step // TOP_K
    row_spec = pl.BlockSpec((tb, d), lambda i: (i, 0))
    if final:
        out_shape = jax.ShapeDtypeStruct((n, d), f32)
        out_specs = row_spec
    else:
        out_shape = [jax.ShapeDtypeStruct((n, d), f32), jax.ShapeDtypeStruct((n, d), bf16)]
        out_specs = [row_spec, row_spec]
    nbytes = 2 * TOP_K * tb * d * 4 + 6 * tb * d * 4
    return pl.pallas_call(
        functools.partial(_combine_kernel, tb=tb, n_steps=n_steps, final=final),
        grid=(n_steps,),
        in_specs=[pl.BlockSpec(memory_space=pl.ANY),
                  pl.BlockSpec(memory_space=pl.ANY),
                  row_spec,
                  pl.BlockSpec((tb, TOP_K), lambda i: (i, 0)),
                  pl.BlockSpec((1, d), lambda i: (0, 0))],
        out_specs=out_specs,
        out_shape=out_shape,
        scratch_shapes=[pltpu.SMEM((2, TOP_K * tb), jnp.int32),
                        pltpu.VMEM((2, TOP_K * tb, d), f32),
                        pltpu.SemaphoreType.DMA(()),
                        pltpu.SemaphoreType.DMA((2,))],
        compiler_params=_params(("arbitrary",), nbytes),
        name="moe_combine",
    )(dest_blk, y, x2d, gates_t, gain2d)


def _dispatch_plan(counts, idx, rank, tm, t_max, tb):
    n_exp = counts.shape[0]
    n = idx.shape[1]
    tiles_e = (counts + tm - 1) // tm
    tile_end = jnp.cumsum(tiles_e)
    tile_start = tile_end - tiles_e
    n_used = tile_end[-1]
    expert = jnp.arange(n_exp, dtype=jnp.int32)[:, None, None]
    dest = rank + jnp.sum(jnp.where(idx[None] == expert, (tile_start * tm)[:, None, None], 0), axis=0)
    tile = jnp.arange(t_max, dtype=jnp.int32)
    tile_c = jnp.minimum(tile, n_used - 1)
    tile_e = jnp.minimum(jnp.searchsorted(tile_end, tile_c, side="right"), n_exp - 1).astype(jnp.int32)
    rows = jnp.clip(counts[tile_e] - (tile_c - tile_start[tile_e]) * tm, 0, tm)
    tile_rows = jnp.where(tile < n_used, rows, 0).astype(jnp.int32)
    dest_blk = dest.reshape(TOP_K, n // tb, tb).transpose(1, 0, 2).reshape(n // tb, TOP_K * tb)
    return dest_blk, tile_e, tile_rows, n_used.reshape(1).astype(jnp.int32)


def kernel(x, norm_mix, w_in, attn_rel_bias, attn_out_gain, conv_w, conv_b, lru_w_a, lru_b_a, lru_w_i, lru_b_i,
           lru_lambda, rec_out_gain, w_out, norm_ffn, router_w, router_b, expert_w_gate_up, expert_b_gate_up,
           expert_w_down, expert_b_down, norm_final):
    batch, seq, d = x.shape
    n = batch * seq
    depth = norm_mix.shape[0]
    attn_width = attn_out_gain.shape[1]
    lru_width = rec_out_gain.shape[1]
    n_exp = router_w.shape[-1]
    tm = min(1024, n)
    sub = min(512, tm)
    t_max = (n * TOP_K) // tm + n_exp
    tb = min(128, n)

    x2d = x.reshape(n, d)
    h = rmsnorm_bf16(x2d, norm_mix, 0)
    out = None
    for l in range(depth):
        qkv = matmul_ws([h], w_in, l, [0], 0, 3 * attn_width, bf16, name="in_proj_qkv")
        rg = matmul_ws([h], w_in, l, [0], 3 * attn_width, 2 * lru_width, f32, name="in_proj_rec")
        attn = attention(qkv, attn_rel_bias[l], attn_out_gain, l, batch, seq, attn_width)
        rec = rglru_block(rg, conv_w, conv_b, lru_w_a, lru_b_a, lru_w_i, lru_b_i, lru_lambda, rec_out_gain,
                          l, batch, seq)
        assert attn_width == lru_width
        x2d = matmul_ws([attn, rec], w_out, l, [0, 1], 0, d, f32, residual=x2d, name="out_proj")
        hp, idx, gates, rank, cnt = router(x2d, norm_ffn, router_w, router_b, l)
        dest_blk, tile_e, tile_rows, n_used = _dispatch_plan(cnt[:, 0], idx, rank, tm, t_max, tb)
        xs_sorted = dispatch(hp, dest_blk, t_max * tm)
        y = moe_experts(xs_sorted, tile_e, tile_rows, n_used, expert_w_gate_up, expert_b_gate_up,
                        expert_w_down, expert_b_down, l, tm, sub)
        final = l == depth - 1
        gain = norm_final.reshape(1, d) if final else norm_mix[l + 1].reshape(1, d)
        res = combine(x2d, y, dest_blk, gates.T, gain, final)
        if final:
            out = res
        else:
            x2d, h = res
    return out.reshape(batch, seq, d)
```

```python
import functools
import math

import jax
import jax.numpy as jnp
from jax import lax
from jax.experimental import pallas as pl
from jax.experimental.pallas import tpu as pltpu

CHUNK = 64
LEFT_CHUNKS = 8
ATTN_HEAD_DIM = 128
REL_CLIP = 128
LRU_C = 8.0
TOP_K = 4
SWIGLU_LIMIT = 7.0
SWIGLU_ALPHA = 1.702
EPS = 1e-6

V7X_LANES = 128
V7X_SUBLANES = 8
V7X_VMEM_BYTES = 64 * 1024 * 1024

MASK_VALUE = -1e30

f32 = jnp.float32
bf16 = jnp.bfloat16


def _vmem_limit(nbytes):
    return int(min(nbytes + 12 * 1024 * 1024, V7X_VMEM_BYTES - 6 * 1024 * 1024))


def _params(sem, nbytes):
    return pltpu.CompilerParams(dimension_semantics=sem, vmem_limit_bytes=_vmem_limit(nbytes))


def _rmsnorm_kernel(x_ref, g_ref, o_ref):
    x = x_ref[...]
    inv = lax.rsqrt(jnp.mean(x * x, axis=-1, keepdims=True) + EPS)
    o_ref[...] = (x * inv * g_ref[...]).astype(o_ref.dtype)


def rmsnorm_bf16(x2d, gain_l, layer):
    n, d = x2d.shape
    tb = min(256, n)
    return pl.pallas_call(
        _rmsnorm_kernel,
        grid=(n // tb,),
        in_specs=[pl.BlockSpec((tb, d), lambda i: (i, 0)),
                  pl.BlockSpec((None, 1, d), lambda i: (layer, 0, 0))],
        out_specs=pl.BlockSpec((tb, d), lambda i: (i, 0)),
        out_shape=jax.ShapeDtypeStruct((n, d), bf16),
        compiler_params=_params(("arbitrary",), 2 * tb * d * 6),
        name="rmsnorm_bf16",
    )(x2d, gain_l.reshape(gain_l.shape[0], 1, d))


def _matmul_kernel(*refs, n_pairs, has_res):
    a_refs = refs[:n_pairs]
    w_refs = refs[n_pairs:2 * n_pairs]
    pos = 2 * n_pairs
    res_ref = refs[pos] if has_res else None
    pos += int(has_res)
    o_ref = refs[pos]
    wbf_refs = refs[pos + 1:pos + 1 + n_pairs]

    @pl.when(pl.program_id(1) == 0)
    def _cast_weights():
        for w_ref, wbf_ref in zip(w_refs, wbf_refs):
            wbf_ref[...] = w_ref[...].astype(bf16)

    acc = jnp.dot(a_refs[0][...], wbf_refs[0][...], preferred_element_type=f32)
    for a_ref, wbf_ref in zip(a_refs[1:], wbf_refs[1:]):
        acc += jnp.dot(a_ref[...], wbf_ref[...], preferred_element_type=f32)
    if has_res:
        acc += res_ref[...]
    o_ref[...] = acc.astype(o_ref.dtype)


def matmul_ws(a_list, w_stack, layer, k_blocks, col0, n_cols, out_dtype, residual=None, name="matmul_ws"):
    m = a_list[0].shape[0]
    tm = min(1024, m)
    tn = min(512, n_cols)
    assert m % tm == 0 and n_cols % tn == 0 and col0 % tn == 0
    cb0 = col0 // tn
    n_pairs = len(a_list)
    in_specs, scratch = [], []
    nbytes = 2 * tm * tn * 4 * (2 if residual is not None else 1)
    for a in a_list:
        kp = a.shape[1]
        in_specs.append(pl.BlockSpec((tm, kp), lambda j, i: (i, 0)))
        nbytes += 2 * tm * kp * 2
    for a, kb in zip(a_list, k_blocks):
        kp = a.shape[1]
        in_specs.append(pl.BlockSpec((None, kp, tn), functools.partial(lambda j, i, kb: (layer, kb, j + cb0), kb=kb)))
        scratch.append(pltpu.VMEM((kp, tn), bf16))
        nbytes += 2 * kp * tn * 4 + kp * tn * 2
    args = list(a_list) + [w_stack] * n_pairs
    if residual is not None:
        in_specs.append(pl.BlockSpec((tm, tn), lambda j, i: (i, j)))
        args.append(residual)
    return pl.pallas_call(
        functools.partial(_matmul_kernel, n_pairs=n_pairs, has_res=residual is not None),
        grid=(n_cols // tn, m // tm),
        in_specs=in_specs,
        out_specs=pl.BlockSpec((tm, tn), lambda j, i: (i, j)),
        out_shape=jax.ShapeDtypeStruct((m, n_cols), out_dtype),
        scratch_shapes=scratch,
        compiler_params=_params(("arbitrary", "arbitrary"), nbytes),
        name=name,
    )(*args)


def _attn_kernel(q_ref, *refs, n_kb, hb, tq, n_hg):
    k_refs = refs[:n_kb]
    v_refs = refs[n_kb:2 * n_kb]
    row0_ref, gain_ref, o_ref, raw_ref, bias_ref = refs[2 * n_kb:]
    i = pl.program_id(1)
    hg = pl.program_id(2)
    dh = ATTN_HEAD_DIM
    scale = 1.0 / math.sqrt(dh)
    n_keys = n_kb * tq

    @pl.when((pl.program_id(0) == 0) & (i == 0) & (hg == 0))
    def _build_bias():
        width = row0_ref.shape[-1]
        qc = ((n_kb - 1) * tq + lax.broadcasted_iota(jnp.int32, (tq, n_keys), 0)) // CHUNK
        kc = lax.broadcasted_iota(jnp.int32, (tq, n_keys), 1) // CHUNK
        valid = (kc <= qc) & (kc >= qc - LEFT_CHUNKS)
        for h in range(bias_ref.shape[0]):
            rows = pltpu.roll(jnp.broadcast_to(row0_ref[h], (tq, width)), 0, 1, stride=1, stride_axis=0)
            bias_ref[h] = jnp.where(valid, rows[:, :n_keys], MASK_VALUE)

    outs = []
    for hh in range(hb):
        sl = slice(hh * dh, (hh + 1) * dh)
        q = q_ref[:, sl]
        bias = bias_ref[hg * hb + hh]
        parts = []
        for j in range(n_kb - 1, -1, -1):
            s = lax.dot_general(q, k_refs[j][:, sl], (((1,), (1,)), ((), ())), preferred_element_type=f32)
            c0 = (n_kb - 1 - j) * tq
            s = s * scale + bias[:, c0:c0 + tq]
            if j > 0:
                s = s + jnp.where(i >= j, 0.0, MASK_VALUE)
            parts.append(s)
        m = parts[0].max(axis=-1, keepdims=True)
        for s in parts[1:]:
            m = jnp.maximum(m, s.max(axis=-1, keepdims=True))
        l = jnp.zeros_like(m)
        acc = jnp.zeros((tq, dh), f32)
        for idx, s in enumerate(parts):
            j = n_kb - 1 - idx
            p = jnp.exp(s - m)
            l = l + p.sum(axis=-1, keepdims=True)
            acc = acc + jnp.dot(p.astype(bf16), v_refs[j][:, sl], preferred_element_type=f32)
        outs.append(acc / l)
    raw_ref[hg] = jnp.concatenate(outs, axis=-1)

    @pl.when(hg == n_hg - 1)
    def _finish():
        ss = jnp.zeros((tq, 1), f32)
        for g in range(n_hg):
            r = raw_ref[g]
            ss = ss + jnp.sum(r * r, axis=-1, keepdims=True)
        inv = lax.rsqrt(ss / (n_hg * hb * dh) + EPS)
        for g in range(n_hg):
            w = hb * dh
            o_ref[:, g * w:(g + 1) * w] = (raw_ref[g] * inv * gain_ref[:, g * w:(g + 1) * w]).astype(o_ref.dtype)


def _attn_bias_row0(rel_bias_l, tq, n_kb):
    n_heads = rel_bias_l.shape[0]
    rb = rel_bias_l.astype(f32)
    far = (n_kb - 1) * tq - REL_CLIP
    near = n_kb * tq - far - (2 * REL_CLIP + 1)
    assert far >= 0 and near >= 0
    fill = lambda col, width: jnp.broadcast_to(rb[:, col:col + 1], (n_heads, width))
    row0 = jnp.concatenate([fill(2 * REL_CLIP, far), rb[:, ::-1], fill(0, near), fill(2 * REL_CLIP, tq)], axis=1)
    return row0.reshape(n_heads, 1, (n_kb + 1) * tq)


def attention(qkv, rel_bias_l, gain_l, layer, batch, seq, attn_width):
    n = qkv.shape[0]
    n_heads = attn_width // ATTN_HEAD_DIM
    tq = min(256, seq)
    band = LEFT_CHUNKS * CHUNK
    assert band % tq == 0 and seq % tq == 0 and tq % CHUNK == 0
    n_kb = band // tq + 1
    nq = seq // tq
    hb = min(4, n_heads)
    n_hg = n_heads // hb
    w = hb * ATTN_HEAD_DIM
    row0 = _attn_bias_row0(rel_bias_l, tq, n_kb)

    def kv_spec(j, which):
        return pl.BlockSpec((tq, w), lambda b, i, g: (b * nq + jnp.maximum(i - j, 0), which * n_hg + g))

    in_specs = [pl.BlockSpec((tq, w), lambda b, i, g: (b * nq + i, g))]
    in_specs += [kv_spec(j, 1) for j in range(n_kb)] + [kv_spec(j, 2) for j in range(n_kb)]
    in_specs += [pl.BlockSpec(row0.shape, lambda b, i, g: (0, 0, 0)),
                 pl.BlockSpec((None, 1, attn_width), lambda b, i, g: (layer, 0, 0))]
    nbytes = 2 * (2 * n_kb + 1) * tq * w * 2 + n_heads * tq * n_kb * tq * 4 + 3 * tq * attn_width * 4
    return pl.pallas_call(
        functools.partial(_attn_kernel, n_kb=n_kb, hb=hb, tq=tq, n_hg=n_hg),
        grid=(batch, nq, n_hg),
        in_specs=in_specs,
        out_specs=pl.BlockSpec((tq, attn_width), lambda b, i, g: (b * nq + i, 0)),
        out_shape=jax.ShapeDtypeStruct((n, attn_width), bf16),
        scratch_shapes=[pltpu.VMEM((n_hg, tq, w), f32), pltpu.VMEM((n_heads, tq, n_kb * tq), f32)],
        compiler_params=_params(("arbitrary", "arbitrary", "arbitrary"), nbytes),
        name="band_attention",
    )(*([qkv] * (2 * n_kb + 1)), row0, gain_l.reshape(gain_l.shape[0], 1, attn_width))


def _sigmoid(x):
    return 0.5 * jnp.tanh(0.5 * x) + 0.5


def _lru_kernel(xr_ref, xg_ref, cw_ref, cb_ref, wa_ref, ba_ref, wi_ref, bi_ref, lam_ref, gain_ref,
                o_ref, xe_ref, a_ref, b_ref, h_ref, carry_ref, *, t, n_blocks, conv_width):
    i = pl.program_id(1)
    hdr = V7X_SUBLANES
    bw = wa_ref.shape[-1]

    @pl.when(i == 0)
    def _reset():
        xe_ref[0:hdr, :] = jnp.zeros((hdr, xe_ref.shape[1]), f32)
        carry_ref[...] = jnp.zeros_like(carry_ref)

    x = xr_ref[...]
    xe_ref[hdr:hdr + t, :] = x
    u = cb_ref[...] + cw_ref[conv_width - 1:conv_width, :] * x
    for j in range(conv_width - 1):
        u = u + cw_ref[j:j + 1, :] * xe_ref[pl.ds(hdr - (conv_width - 1) + j, t), :]
    xe_ref[0:hdr, :] = x[t - hdr:t, :]

    lam = lam_ref[...]
    nsp = -LRU_C * (jnp.maximum(-lam, 0.0) + jnp.log1p(jnp.exp(-jnp.abs(lam))))
    for nb in range(n_blocks):
        sl = slice(nb * bw, (nb + 1) * bw)
        ub = u[:, sl]
        ubb = ub.astype(bf16)
        r = _sigmoid(jnp.dot(ubb, wa_ref[nb].astype(bf16), preferred_element_type=f32) + ba_ref[:, sl])
        ig = _sigmoid(jnp.dot(ubb, wi_ref[nb].astype(bf16), preferred_element_type=f32) + bi_ref[:, sl])
        log_a = r * nsp[:, sl]
        a = jnp.exp(log_a)
        a_ref[:, sl] = a
        b_ref[:, sl] = jnp.sqrt(1.0 - a * a) * (ig * ub)

    a = a_ref[...]
    b = b_ref[...]
    row = lax.broadcasted_iota(jnp.int32, a.shape, 0) % hdr
    s = 1
    while s < hdr:
        keep = row >= s
        a_sh = jnp.where(keep, pltpu.roll(a, s, 0), 1.0)
        b_sh = jnp.where(keep, pltpu.roll(b, s, 0), 0.0)
        b = a * b_sh + b
        a = a * a_sh
        s *= 2
    a_ref[...] = a
    b_ref[...] = b

    def tile_step(j, carry):
        r0 = pl.multiple_of(j * hdr, hdr)
        h = a_ref[pl.ds(r0, hdr), :] * carry + b_ref[pl.ds(r0, hdr), :]
        h_ref[pl.ds(r0, hdr), :] = h
        return h[hdr - 1:hdr, :]

    carry_ref[...] = lax.fori_loop(0, t // hdr, tile_step, carry_ref[...])

    y = h_ref[...] * jax.nn.gelu(xg_ref[...], approximate=True)
    inv = lax.rsqrt(jnp.mean(y * y, axis=-1, keepdims=True) + EPS)
    o_ref[...] = (y * inv * gain_ref[...]).astype(o_ref.dtype)


def rglru_block(rg, conv_w, conv_b, w_a, b_a, w_i, b_i, lam, gain, layer, batch, seq):
    n = rg.shape[0]
    n_layers, conv_width, wr = conv_w.shape
    n_blocks = w_a.shape[1]
    bw = w_a.shape[2]
    t = min(256, seq)
    ns = seq // t
    row3 = lambda v: v.reshape(n_layers, 1, wr)
    vec_spec = pl.BlockSpec((None, 1, wr), lambda b, i: (layer, 0, 0))
    wspec = pl.BlockSpec((None, n_blocks, bw, bw), lambda b, i: (layer, 0, 0, 0))
    nbytes = 2 * 2 * t * wr * 4 + 2 * t * wr * 2 + 4 * (t + 8) * wr * 4 + 4 * n_blocks * bw * bw * 4 + 8 * t * wr * 4
    return pl.pallas_call(
        functools.partial(_lru_kernel, t=t, n_blocks=n_blocks, conv_width=conv_width),
        grid=(batch, ns),
        in_specs=[pl.BlockSpec((t, wr), lambda b, i: (b * ns + i, 0)),
                  pl.BlockSpec((t, wr), lambda b, i: (b * ns + i, 1)),
                  pl.BlockSpec((None, conv_width, wr), lambda b, i: (layer, 0, 0)),
                  vec_spec, wspec, vec_spec, wspec, vec_spec, vec_spec, vec_spec],
        out_specs=pl.BlockSpec((t, wr), lambda b, i: (b * ns + i, 0)),
        out_shape=jax.ShapeDtypeStruct((n, wr), bf16),
        scratch_shapes=[pltpu.VMEM((t + V7X_SUBLANES, wr), f32), pltpu.VMEM((t, wr), f32),
                        pltpu.VMEM((t, wr), f32), pltpu.VMEM((t, wr), f32), pltpu.VMEM((1, wr), f32)],
        compiler_params=_params(("arbitrary", "arbitrary"), nbytes),
        name="conv_rglru",
    )(rg, rg, conv_w, row3(conv_b), w_a, row3(b_a), w_i, row3(b_i), row3(lam), row3(gain))


def _router_kernel(x_ref, g_ref, whi_ref, wlo_ref, rb_ref, hp_ref, idx_ref, gate_ref, rank_ref, cnt_ref, carry_ref,
                   *, tb, n_exp):
    step = pl.program_id(0)

    @pl.when(step == 0)
    def _reset():
        carry_ref[...] = jnp.zeros_like(carry_ref)

    x = x_ref[...]
    inv = lax.rsqrt(jnp.mean(x * x, axis=-1, keepdims=True) + EPS)
    h = x * inv * g_ref[...]
    half = h.shape[1] // 2
    h_hi = h.astype(bf16)
    h_hi32 = h_hi.astype(f32)
    bits = lax.bitcast_convert_type(h_hi32, jnp.uint32)
    hp_ref[...] = (bits[:, :half] >> 16) | bits[:, half:]

    h_lo = (h - h_hi32).astype(bf16)
    w_hi = whi_ref[...]
    acc = (jnp.dot(h_hi, w_hi, preferred_element_type=f32) + jnp.dot(h_lo, w_hi, preferred_element_type=f32)
           + jnp.dot(h_hi, wlo_ref[...], preferred_element_type=f32))
    logits = acc.T[:n_exp] + rb_ref[...]
    e_iota = lax.broadcasted_iota(jnp.int32, (n_exp, tb), 0)
    tri = (lax.broadcasted_iota(jnp.int32, (tb, tb), 0) < lax.broadcasted_iota(jnp.int32, (tb, tb), 1)).astype(bf16)
    base = carry_ref[...]
    vals, idxs, ranks = [], [], []
    l = logits
    for _ in range(TOP_K):
        m = l.max(axis=0, keepdims=True)
        idx = jnp.min(jnp.where(l == m, e_iota, n_exp), axis=0, keepdims=True)
        onehot = e_iota == idx
        ohf = onehot.astype(f32)
        before = jnp.dot(ohf.astype(bf16), tri, preferred_element_type=f32)
        ranks.append(jnp.sum(jnp.where(onehot, before + base, 0.0), axis=0, keepdims=True))
        base = base + jnp.sum(ohf, axis=1, keepdims=True)
        vals.append(m)
        idxs.append(idx)
        l = jnp.where(onehot, -jnp.inf, l)
    carry_ref[...] = base
    exps = [jnp.exp(v - vals[0]) for v in vals]
    denom = exps[0]
    for e in exps[1:]:
        denom = denom + e
    idx_ref[...] = jnp.concatenate(idxs, axis=0)
    gate_ref[...] = jnp.concatenate([e / denom for e in exps], axis=0)
    rank_ref[...] = jnp.concatenate(ranks, axis=0).astype(jnp.int32)
    cnt_ref[...] = jnp.broadcast_to(base, cnt_ref.shape).astype(jnp.int32)


def router(x2d, gain_l, router_w_l, router_b_l, layer):
    n, d = x2d.shape
    n_exp = router_w_l.shape[-1]
    tb = min(512, n)
    e_pad = -(-n_exp // V7X_LANES) * V7X_LANES
    rw = jnp.pad(router_w_l[layer].astype(f32), ((0, 0), (0, e_pad - n_exp)))
    w_hi = rw.astype(bf16)
    w_lo = (rw - w_hi.astype(f32)).astype(bf16)
    rb = router_b_l[layer].reshape(n_exp, 1)
    slot_spec = pl.BlockSpec((TOP_K, tb), lambda i: (0, i))
    nbytes = 2 * tb * d * 4 + 2 * tb * d * 2 + 4 * e_pad * d * 2 + 6 * tb * d * 4
    return pl.pallas_call(
        functools.partial(_router_kernel, tb=tb, n_exp=n_exp),
        grid=(n // tb,),
        in_specs=[pl.BlockSpec((tb, d), lambda i: (i, 0)),
                  pl.BlockSpec((None, 1, d), lambda i: (layer, 0, 0)),
                  pl.BlockSpec((d, e_pad), lambda i: (0, 0)),
                  pl.BlockSpec((d, e_pad), lambda i: (0, 0)),
                  pl.BlockSpec((n_exp, 1), lambda i: (0, 0))],
        out_specs=[pl.BlockSpec((tb, d // 2), lambda i: (i, 0)), slot_spec, slot_spec, slot_spec,
                   pl.BlockSpec((n_exp, V7X_LANES), lambda i: (0, 0))],
        out_shape=[jax.ShapeDtypeStruct((n, d // 2), jnp.uint32),
                   jax.ShapeDtypeStruct((TOP_K, n), jnp.int32),
                   jax.ShapeDtypeStruct((TOP_K, n), f32),
                   jax.ShapeDtypeStruct((TOP_K, n), jnp.int32),
                   jax.ShapeDtypeStruct((n_exp, V7X_LANES), jnp.int32)],
        scratch_shapes=[pltpu.VMEM((n_exp, 1), f32)],
        compiler_params=_params(("arbitrary",), nbytes),
        name="router_topk",
    )(x2d, gain_l.reshape(gain_l.shape[0], 1, d), w_hi, w_lo, rb)


def _moe_kernel(tile_e_ref, tile_rows_ref, n_used_ref,
                xs_hbm, wgu_ref, bgu_ref, wdn_ref, bdn_ref,
                y_ref,
                stage_ref, xs_ref, act_ref, wgu_bf, wdn_bf, gu_a, gu_b, row_sem,
                *, tm, sub, n_gu, n_dn, tgu, t_max):
    t = pl.program_id(0)
    s = pl.program_id(1)
    n_used = n_used_ref[0]
    active = t < n_used
    half = stage_ref.shape[1]
    th = tgu // 2
    n_sb = tm // sub
    rows = tile_rows_ref[t]
    ns = (rows + sub - 1) // sub

    def sub_copy(tile, rb):
        r0 = pl.multiple_of(rb * sub, sub)
        return pltpu.make_async_copy(xs_hbm.at[pl.ds(tile * tm + r0, sub), :], stage_ref.at[pl.ds(r0, sub), :],
                                     row_sem.at[rb])

    def start_tile(tile):
        n_next = (tile_rows_ref[tile] + sub - 1) // sub

        def issue(rb, c):
            sub_copy(tile, rb).start()
            return c

        lax.fori_loop(0, n_next, issue, 0)

    @pl.when(active & (s == 0))
    def _stage_rows():
        @pl.when(t == 0)
        def _first():
            start_tile(0)

        def unpack(rb, c):
            sub_copy(t, rb).wait()
            r0 = pl.multiple_of(rb * sub, sub)
            live = lax.broadcasted_iota(jnp.int32, (sub, half), 0) < rows - r0
            w = jnp.where(live, stage_ref[pl.ds(r0, sub), :], jnp.uint32(0))
            xs_ref[pl.ds(r0, sub), 0:half] = lax.bitcast_convert_type(w << 16, f32).astype(bf16)
            xs_ref[pl.ds(r0, sub), half:2 * half] = lax.bitcast_convert_type(w & jnp.uint32(0xFFFF0000), f32).astype(bf16)
            return c

        lax.fori_loop(0, ns, unpack, 0)

    @pl.when(active & (s == 1) & (t + 1 < n_used))
    def _prefetch_rows():
        start_tile(jnp.minimum(t + 1, t_max - 1))

    def activate(raw_ref, s_dst, rb_dst):
        sel = (lax.broadcasted_iota(jnp.int32, (tgu, th), 0) == 2 * lax.broadcasted_iota(jnp.int32, (tgu, th), 1)).astype(bf16)
        gu = raw_ref[...]
        glu = jnp.minimum(gu, SWIGLU_LIMIT)
        glu = glu * _sigmoid(SWIGLU_ALPHA * glu)
        lin = jnp.clip(gu, -SWIGLU_LIMIT, SWIGLU_LIMIT) + 1.0
        prod = glu * pltpu.roll(lin, tgu - 1, 1)
        act_ref[s_dst, pl.ds(pl.multiple_of(rb_dst * sub, sub), sub), :] = jnp.dot(
            prod.astype(bf16), sel, preferred_element_type=f32).astype(bf16)

    @pl.when(active & (s < n_gu))
    def _gate_up():
        wgu_bf[...] = wgu_ref[...].astype(bf16)

        @pl.when((t == 0) & (s == 0))
        def _define_raw():
            gu_a[...] = jnp.zeros_like(gu_a)
            gu_b[...] = jnp.zeros_like(gu_b)

        def body(rb, c):
            k = s * ns + rb
            r0 = pl.multiple_of(rb * sub, sub)
            first = k == 0
            s_prev = jnp.where(first, n_gu, jnp.where(rb == 0, s - 1, s))
            rb_prev = jnp.where(first, 0, jnp.where(rb == 0, ns - 1, rb - 1))

            def stage(cur_ref, prev_ref):
                cur_ref[...] = jnp.dot(xs_ref[pl.ds(r0, sub), :], wgu_bf[...],
                                       preferred_element_type=f32) + bgu_ref[...]
                activate(prev_ref, s_prev, rb_prev)

            @pl.when(k % 2 == 0)
            def _():
                stage(gu_a, gu_b)

            @pl.when(k % 2 == 1)
            def _():
                stage(gu_b, gu_a)

            return c

        lax.fori_loop(0, ns, body, 0)

    @pl.when(active & (s == n_gu))
    def _last_activation():
        k_last = n_gu * ns - 1

        @pl.when(k_last % 2 == 0)
        def _():
            activate(gu_a, n_gu - 1, ns - 1)

        @pl.when(k_last % 2 == 1)
        def _():
            activate(gu_b, n_gu - 1, ns - 1)

    @pl.when(active & (s >= n_gu))
    def _down():
        wdn_bf[...] = wdn_ref[...].astype(bf16)

        def body(rb, c):
            r0 = pl.multiple_of(rb * sub, sub)
            acc = jnp.zeros((sub, y_ref.shape[1]), f32) + bdn_ref[...]
            for g in range(n_gu):
                acc = acc + jnp.dot(act_ref[g, pl.ds(r0, sub), :], wdn_bf[g * th:(g + 1) * th, :],
                                    preferred_element_type=f32)
            y_ref[pl.ds(r0, sub), :] = acc
            return c

        lax.fori_loop(0, ns, body, 0)

        def clear(rb, c):
            r0 = pl.multiple_of(rb * sub, sub)
            y_ref[pl.ds(r0, sub), :] = jnp.zeros((sub, y_ref.shape[1]), f32)
            return c

        lax.fori_loop(ns, n_sb, clear, 0)


def moe_experts(xs_sorted, tile_e, tile_rows, n_used, w_gu, b_gu, w_dn, b_dn, layer, tm, sub):
    half = xs_sorted.shape[1]
    d = 2 * half
    n_layers, n_exp, _, two_f = w_gu.shape
    ff = two_f // 2
    t_max = xs_sorted.shape[0] // tm
    tgu = min(512, two_f)
    tdn = min(512, d)
    n_gu, n_dn = two_f // tgu, d // tdn

    def eff(t, s, te, nu):
        live = t < nu[0]
        return jnp.where(live, t, jnp.maximum(nu[0] - 1, 0)), live

    def gu_idx(t, s, te, tr, nu):
        _, live = eff(t, s, te, nu)
        return (layer, te[t], 0, jnp.where(live, jnp.minimum(s, n_gu - 1), n_gu - 1))

    def dn_idx(t, s, te, tr, nu):
        _, live = eff(t, s, te, nu)
        return (layer, te[t], 0, jnp.where(live, jnp.maximum(s - n_gu, 0), n_dn - 1))

    def y_idx(t, s, te, tr, nu):
        tt, live = eff(t, s, te, nu)
        return (tt, jnp.where(live, jnp.maximum(s - n_gu, 0), n_dn - 1))

    nbytes = (tm * half * 4 + tm * d * 2 + tm * ff * 2 + 2 * d * tgu * 4 + d * tgu * 2
              + 2 * ff * tdn * 4 + ff * tdn * 2 + 2 * tm * tdn * 4)
    grid_spec = pltpu.PrefetchScalarGridSpec(
        num_scalar_prefetch=3,
        grid=(t_max, n_gu + n_dn),
        in_specs=[pl.BlockSpec(memory_space=pl.ANY),
                  pl.BlockSpec((None, None, d, tgu), gu_idx),
                  pl.BlockSpec((None, None, 1, tgu), gu_idx),
                  pl.BlockSpec((None, None, ff, tdn), dn_idx),
                  pl.BlockSpec((None, None, 1, tdn), dn_idx)],
        out_specs=pl.BlockSpec((tm, tdn), y_idx),
        scratch_shapes=[pltpu.VMEM((tm, half), jnp.uint32),
                        pltpu.VMEM((tm, d), bf16),
                        pltpu.VMEM((n_gu + 1, tm, tgu // 2), bf16),
                        pltpu.VMEM((d, tgu), bf16),
                        pltpu.VMEM((ff, tdn), bf16),
                        pltpu.VMEM((sub, tgu), f32),
                        pltpu.VMEM((sub, tgu), f32),
                        pltpu.SemaphoreType.DMA((tm // sub,))],
    )
    return pl.pallas_call(
        functools.partial(_moe_kernel, tm=tm, sub=sub, n_gu=n_gu, n_dn=n_dn, tgu=tgu, t_max=t_max),
        grid_spec=grid_spec,
        out_shape=jax.ShapeDtypeStruct((t_max * tm, d), f32),
        compiler_params=_params(("arbitrary", "arbitrary"), nbytes),
        name="moe_experts",
    )(tile_e, tile_rows, n_used, xs_sorted, w_gu,
      b_gu.reshape(n_layers, n_exp, 1, two_f), w_dn, b_dn.reshape(n_layers, n_exp, 1, d))


def _dispatch_kernel(dest_hbm, hp_ref, xs_hbm, dest_smem, dest_sem, row_sem, *, tb, n_steps):
    i = pl.program_id(0)
    slot = i % 2

    def dest_copy(step):
        return pltpu.make_async_copy(dest_hbm.at[step], dest_smem.at[step % 2], dest_sem)

    def row_copy(r, row):
        return pltpu.make_async_copy(hp_ref.at[pl.ds(r, 1), :], xs_hbm.at[pl.ds(row, 1), :], row_sem)

    @pl.when(i == 0)
    def _first():
        dest_copy(0).start()

    dest_copy(i).wait()

    @pl.when(i + 1 < n_steps)
    def _next():
        dest_copy(i + 1).start()

    def issue(r, c):
        for k in range(TOP_K):
            row_copy(r, dest_smem[slot, k * tb + r]).start()
        return c

    lax.fori_loop(0, tb, issue, 0, unroll=2)
    for _ in range(TOP_K):
        pltpu.make_async_copy(hp_ref, xs_hbm.at[pl.ds(0, tb), :], row_sem).wait()


def dispatch(hp, dest_blk, n_rows_out):
    n, half = hp.shape
    n_steps, per_step = dest_blk.shape
    tb = per_step // TOP_K
    return pl.pallas_call(
        functools.partial(_dispatch_kernel, tb=tb, n_steps=n_steps),
        grid=(n_steps,),
        in_specs=[pl.BlockSpec(memory_space=pl.ANY),
                  pl.BlockSpec((tb, half), lambda i: (i, 0))],
        out_specs=pl.BlockSpec(memory_space=pl.ANY),
        out_shape=jax.ShapeDtypeStruct((n_rows_out, half), jnp.uint32),
        scratch_shapes=[pltpu.SMEM((2, per_step), jnp.int32),
                        pltpu.SemaphoreType.DMA(()),
                        pltpu.SemaphoreType.DMA(())],
        compiler_params=_params(("arbitrary",), 2 * tb * half * 4),
        name="moe_dispatch",
    )(dest_blk, hp)


def _combine_kernel(dest_hbm, y_hbm, x_ref, gate_ref, gain_ref, *refs, tb, n_steps, final):
    if final:
        o_ref, dest_smem, ybuf, dest_sem, row_sem = refs
        xo_ref = None
    else:
        xo_ref, o_ref, dest_smem, ybuf, dest_sem, row_sem = refs
    i = pl.program_id(0)
    n_rows = TOP_K * tb

    def dest_copy(step):
        return pltpu.make_async_copy(dest_hbm.at[step], dest_smem.at[step % 2], dest_sem)

    def start_gather(step):
        for slot in range(2):
            @pl.when(step % 2 == slot)
            def _():
                def issue(r, c):
                    pltpu.make_async_copy(y_hbm.at[pl.ds(dest_smem[slot, r], 1), :], ybuf.at[slot, pl.ds(r, 1), :],
                                          row_sem.at[slot]).start()
                    return c

                lax.fori_loop(0, n_rows, issue, 0, unroll=8)

    @pl.when(i == 0)
    def _first():
        dest_copy(0).start()
        dest_copy(0).wait()
        start_gather(0)

        if n_steps > 1:
            dest_copy(1).start()

    @pl.when(i + 1 < n_steps)
    def _next():
        dest_copy(i + 1).wait()
        start_gather(i + 1)

        @pl.when(i + 2 < n_steps)
        def _():
            dest_copy(i + 2).start()

    pltpu.make_async_copy(y_hbm.at[pl.ds(0, n_rows), :], ybuf.at[i % 2], row_sem.at[i % 2]).wait()

    g = gate_ref[...]
    acc = x_ref[...]
    for k in range(TOP_K):
        acc = acc + g[:, k:k + 1] * ybuf[i % 2, k * tb:(k + 1) * tb, :]
    if not final:
        xo_ref[...] = acc
    inv = lax.rsqrt(jnp.mean(acc * acc, axis=-1, keepdims=True) + EPS)
    o_ref[...] = (acc * inv * gain_ref[...]).astype(o_ref.dtype)


def combine(x2d, y, dest_blk, gates_t, gain2d, final):
    n, d = x2d.shape
    n_steps, per_step = dest_blk.shape
    tb = per_step // TOP_K
    row_spec = pl.BlockSpec((tb, d), lambda i: (i, 0))
    if final:
        out_shape = jax.ShapeDtypeStruct((n, d), f32)
        out_specs = row_spec
    else:
        out_shape = [jax.ShapeDtypeStruct((n, d), f32), jax.ShapeDtypeStruct((n, d), bf16)]
        out_specs = [row_spec, row_spec]
    nbytes = 2 * TOP_K * tb * d * 4 + 6 * tb * d * 4
    return pl.pallas_call(
        functools.partial(_combine_kernel, tb=tb, n_steps=n_steps, final=final),
        grid=(n_steps,),
        in_specs=[pl.BlockSpec(memory_space=pl.ANY),
                  pl.BlockSpec(memory_space=pl.ANY),
                  row_spec,
                  pl.BlockSpec((tb, TOP_K), lambda i: (i, 0)),
                  pl.BlockSpec((1, d), lambda i: (0, 0))],
        out_specs=out_specs,
        out_shape=out_shape,
        scratch_shapes=[pltpu.SMEM((2, TOP_K * tb), jnp.int32),
                        pltpu.VMEM((2, TOP_K * tb, d), f32),
                        pltpu.SemaphoreType.DMA(()),
                        pltpu.SemaphoreType.DMA((2,))],
        compiler_params=_params(("arbitrary",), nbytes),
        name="moe_combine",
    )(dest_blk, y, x2d, gates_t, gain2d)


def _dispatch_plan(counts, idx, rank, tm, t_max, tb):
    n_exp = counts.shape[0]
    n = idx.shape[1]
    tiles_e = (counts + tm - 1) // tm
    tile_end = jnp.cumsum(tiles_e)
    tile_start = tile_end - tiles_e
    n_used = tile_end[-1]
    expert = jnp.arange(n_exp, dtype=jnp.int32)[:, None, None]
    dest = rank + jnp.sum(jnp.where(idx[None] == expert, (tile_start * tm)[:, None, None], 0), axis=0)
    tile = jnp.arange(t_max, dtype=jnp.int32)
    tile_c = jnp.minimum(tile, n_used - 1)
    tile_e = jnp.minimum(jnp.searchsorted(tile_end, tile_c, side="right"), n_exp - 1).astype(jnp.int32)
    rows = jnp.clip(counts[tile_e] - (tile_c - tile_start[tile_e]) * tm, 0, tm)
    tile_rows = jnp.where(tile < n_used, rows, 0).astype(jnp.int32)
    dest_blk = dest.reshape(TOP_K, n // tb, tb).transpose(1, 0, 2).reshape(n // tb, TOP_K * tb)
    return dest_blk, tile_e, tile_rows, n_used.reshape(1).astype(jnp.int32)


def kernel(x, norm_mix, w_in, attn_rel_bias, attn_out_gain, conv_w, conv_b, lru_w_a, lru_b_a, lru_w_i, lru_b_i,
           lru_lambda, rec_out_gain, w_out, norm_ffn, router_w, router_b, expert_w_gate_up, expert_b_gate_up,
           expert_w_down, expert_b_down, norm_final):
    batch, seq, d = x.shape
    n = batch * seq
    depth = norm_mix.shape[0]
    attn_width = attn_out_gain.shape[1]
    lru_width = rec_out_gain.shape[1]
    n_exp = router_w.shape[-1]
    tm = min(1024, n)
    sub = min(512, tm)
    t_max = (n * TOP_K) // tm + n_exp
    tb = min(128, n)

    x2d = x.reshape(n, d)
    h = rmsnorm_bf16(x2d, norm_mix, 0)
    out = None
    for l in range(depth):
        qkv = matmul_ws([h], w_in, l, [0], 0, 3 * attn_width, bf16, name="in_proj_qkv")
        rg = matmul_ws([h], w_in, l, [0], 3 * attn_width, 2 * lru_width, f32, name="in_proj_rec")
        attn = attention(qkv, attn_rel_bias[l], attn_out_gain, l, batch, seq, attn_width)
        rec = rglru_block(rg, conv_w, conv_b, lru_w_a, lru_b_a, lru_w_i, lru_b_i, lru_lambda, rec_out_gain,
                          l, batch, seq)
        assert attn_width == lru_width
        x2d = matmul_ws([attn, rec], w_out, l, [0, 1], 0, d, f32, residual=x2d, name="out_proj")
        hp, idx, gates, rank, cnt = router(x2d, norm_ffn, router_w, router_b, l)
        dest_blk, tile_e, tile_rows, n_used = _dispatch_plan(cnt[:, 0], idx, rank, tm, t_max, tb)
        xs_sorted = dispatch(hp, dest_blk, t_max * tm)
        y = moe_experts(xs_sorted, tile_e, tile_rows, n_used, expert_w_gate_up, expert_b_gate_up,
                        expert_w_down, expert_b_down, l, tm, sub)
        final = l == depth - 1
        gain = norm_final.reshape(1, d) if final else norm_mix[l + 1].reshape(1, d)
        res = combine(x2d, y, dest_blk, gates.T, gain, final)
        if final:
            out = res
        else:
            x2d, h = res
    return out.reshape(batch, seq, d)
```

```python
import functools
import math

import jax
import jax.numpy as jnp
from jax import lax
from jax.experimental import pallas as pl
from jax.experimental.pallas import tpu as pltpu

CHUNK = 64
LEFT_CHUNKS = 8
ATTN_HEAD_DIM = 128
REL_CLIP = 128
LRU_C = 8.0
TOP_K = 4
SWIGLU_LIMIT = 7.0
SWIGLU_ALPHA = 1.702
EPS = 1e-6

V7X_LANES = 128
V7X_SUBLANES = 8
V7X_VMEM_BYTES = 64 * 1024 * 1024

MASK_VALUE = -1e30

f32 = jnp.float32
bf16 = jnp.bfloat16


def _vmem_limit(nbytes):
    return int(min(nbytes + 12 * 1024 * 1024, V7X_VMEM_BYTES - 6 * 1024 * 1024))


def _params(sem, nbytes):
    return pltpu.CompilerParams(dimension_semantics=sem, vmem_limit_bytes=_vmem_limit(nbytes))


def _rmsnorm_kernel(x_ref, g_ref, o_ref):
    x = x_ref[...]
    inv = lax.rsqrt(jnp.mean(x * x, axis=-1, keepdims=True) + EPS)
    o_ref[...] = (x * inv * g_ref[...]).astype(o_ref.dtype)


def rmsnorm_bf16(x2d, gain_l, layer):
    n, d = x2d.shape
    tb = min(256, n)
    return pl.pallas_call(
        _rmsnorm_kernel,
        grid=(n // tb,),
        in_specs=[pl.BlockSpec((tb, d), lambda i: (i, 0)),
                  pl.BlockSpec((None, 1, d), lambda i: (layer, 0, 0))],
        out_specs=pl.BlockSpec((tb, d), lambda i: (i, 0)),
        out_shape=jax.ShapeDtypeStruct((n, d), bf16),
        compiler_params=_params(("arbitrary",), 2 * tb * d * 6),
        name="rmsnorm_bf16",
    )(x2d, gain_l.reshape(gain_l.shape[0], 1, d))


def _matmul_kernel(*refs, n_pairs, has_res):
    a_refs = refs[:n_pairs]
    w_refs = refs[n_pairs:2 * n_pairs]
    pos = 2 * n_pairs
    res_ref = refs[pos] if has_res else None
    pos += int(has_res)
    o_ref = refs[pos]
    wbf_refs = refs[pos + 1:pos + 1 + n_pairs]

    @pl.when(pl.program_id(1) == 0)
    def _cast_weights():
        for w_ref, wbf_ref in zip(w_refs, wbf_refs):
            wbf_ref[...] = w_ref[...].astype(bf16)

    acc = jnp.dot(a_refs[0][...], wbf_refs[0][...], preferred_element_type=f32)
    for a_ref, wbf_ref in zip(a_refs[1:], wbf_refs[1:]):
        acc += jnp.dot(a_ref[...], wbf_ref[...], preferred_element_type=f32)
    if has_res:
        acc += res_ref[...]
    o_ref[...] = acc.astype(o_ref.dtype)


def matmul_ws(a_list, w_stack, layer, k_blocks, col0, n_cols, out_dtype, residual=None, name="matmul_ws"):
    m = a_list[0].shape[0]
    tm = min(1024, m)
    tn = min(512, n_cols)
    assert m % tm == 0 and n_cols % tn == 0 and col0 % tn == 0
    cb0 = col0 // tn
    n_pairs = len(a_list)
    in_specs, scratch = [], []
    nbytes = 2 * tm * tn * 4 * (2 if residual is not None else 1)
    for a in a_list:
        kp = a.shape[1]
        in_specs.append(pl.BlockSpec((tm, kp), lambda j, i: (i, 0)))
        nbytes += 2 * tm * kp * 2
    for a, kb in zip(a_list, k_blocks):
        kp = a.shape[1]
        in_specs.append(pl.BlockSpec((None, kp, tn), functools.partial(lambda j, i, kb: (layer, kb, j + cb0), kb=kb)))
        scratch.append(pltpu.VMEM((kp, tn), bf16))
        nbytes += 2 * kp * tn * 4 + kp * tn * 2
    args = list(a_list) + [w_stack] * n_pairs
    if residual is not None:
        in_specs.append(pl.BlockSpec((tm, tn), lambda j, i: (i, j)))
        args.append(residual)
    return pl.pallas_call(
        functools.partial(_matmul_kernel, n_pairs=n_pairs, has_res=residual is not None),
        grid=(n_cols // tn, m // tm),
        in_specs=in_specs,
        out_specs=pl.BlockSpec((tm, tn), lambda j, i: (i, j)),
        out_shape=jax.ShapeDtypeStruct((m, n_cols), out_dtype),
        scratch_shapes=scratch,
        compiler_params=_params(("arbitrary", "arbitrary"), nbytes),
        name=name,
    )(*args)


def _attn_kernel(q_ref, *refs, n_kb, hb, tq, n_hg):
    k_refs = refs[:n_kb]
    v_refs = refs[n_kb:2 * n_kb]
    row0_ref, gain_ref, o_ref, raw_ref, bias_ref = refs[2 * n_kb:]
    i = pl.program_id(1)
    hg = pl.program_id(2)
    dh = ATTN_HEAD_DIM
    scale = 1.0 / math.sqrt(dh)
    n_keys = n_kb * tq

    @pl.when((pl.program_id(0) == 0) & (i == 0) & (hg == 0))
    def _build_bias():
        width = row0_ref.shape[-1]
        qc = ((n_kb - 1) * tq + lax.broadcasted_iota(jnp.int32, (tq, n_keys), 0)) // CHUNK
        kc = lax.broadcasted_iota(jnp.int32, (tq, n_keys), 1) // CHUNK
        valid = (kc <= qc) & (kc >= qc - LEFT_CHUNKS)
        for h in range(bias_ref.shape[0]):
            rows = pltpu.roll(jnp.broadcast_to(row0_ref[h], (tq, width)), 0, 1, stride=1, stride_axis=0)
            bias_ref[h] = jnp.where(valid, rows[:, :n_keys], MASK_VALUE)

    outs = []
    for hh in range(hb):
        sl = slice(hh * dh, (hh + 1) * dh)
        q = q_ref[:, sl]
        bias = bias_ref[hg * hb + hh]
        parts = []
        for j in range(n_kb - 1, -1, -1):
            s = lax.dot_general(q, k_refs[j][:, sl], (((1,), (1,)), ((), ())), preferred_element_type=f32)
            c0 = (n_kb - 1 - j) * tq
            s = s * scale + bias[:, c0:c0 + tq]
            if j > 0:
                s = s + jnp.where(i >= j, 0.0, MASK_VALUE)
            parts.append(s)
        m = parts[0].max(axis=-1, keepdims=True)
        for s in parts[1:]:
            m = jnp.maximum(m, s.max(axis=-1, keepdims=True))
        l = jnp.zeros_like(m)
        acc = jnp.zeros((tq, dh), f32)
        for idx, s in enumerate(parts):
            j = n_kb - 1 - idx
            p = jnp.exp(s - m)
            l = l + p.sum(axis=-1, keepdims=True)
            acc = acc + jnp.dot(p.astype(bf16), v_refs[j][:, sl], preferred_element_type=f32)
        outs.append(acc / l)
    raw_ref[hg] = jnp.concatenate(outs, axis=-1)

    @pl.when(hg == n_hg - 1)
    def _finish():
        ss = jnp.zeros((tq, 1), f32)
        for g in range(n_hg):
            r = raw_ref[g]
            ss = ss + jnp.sum(r * r, axis=-1, keepdims=True)
        inv = lax.rsqrt(ss / (n_hg * hb * dh) + EPS)
        for g in range(n_hg):
            w = hb * dh
            o_ref[:, g * w:(g + 1) * w] = (raw_ref[g] * inv * gain_ref[:, g * w:(g + 1) * w]).astype(o_ref.dtype)


def _attn_bias_row0(rel_bias_l, tq, n_kb):
    n_heads = rel_bias_l.shape[0]
    rb = rel_bias_l.astype(f32)
    far = (n_kb - 1) * tq - REL_CLIP
    near = n_kb * tq - far - (2 * REL_CLIP + 1)
    assert far >= 0 and near >= 0
    fill = lambda col, width: jnp.broadcast_to(rb[:, col:col + 1], (n_heads, width))
    row0 = jnp.concatenate([fill(2 * REL_CLIP, far), rb[:, ::-1], fill(0, near), fill(2 * REL_CLIP, tq)], axis=1)
    return row0.reshape(n_heads, 1, (n_kb + 1) * tq)


def attention(qkv, rel_bias_l, gain_l, layer, batch, seq, attn_width):
    n = qkv.shape[0]
    n_heads = attn_width // ATTN_HEAD_DIM
    tq = min(256, seq)
    band = LEFT_CHUNKS * CHUNK
    assert band % tq == 0 and seq % tq == 0 and tq % CHUNK == 0
    n_kb = band // tq + 1
    nq = seq // tq
    hb = min(4, n_heads)
    n_hg = n_heads // hb
    w = hb * ATTN_HEAD_DIM
    row0 = _attn_bias_row0(rel_bias_l, tq, n_kb)

    def kv_spec(j, which):
        return pl.BlockSpec((tq, w), lambda b, i, g: (b * nq + jnp.maximum(i - j, 0), which * n_hg + g))

    in_specs = [pl.BlockSpec((tq, w), lambda b, i, g: (b * nq + i, g))]
    in_specs += [kv_spec(j, 1) for j in range(n_kb)] + [kv_spec(j, 2) for j in range(n_kb)]
    in_specs += [pl.BlockSpec(row0.shape, lambda b, i, g: (0, 0, 0)),
                 pl.BlockSpec((None, 1, attn_width), lambda b, i, g: (layer, 0, 0))]
    nbytes = 2 * (2 * n_kb + 1) * tq * w * 2 + n_heads * tq * n_kb * tq * 4 + 3 * tq * attn_width * 4
    return pl.pallas_call(
        functools.partial(_attn_kernel, n_kb=n_kb, hb=hb, tq=tq, n_hg=n_hg),
        grid=(batch, nq, n_hg),
        in_specs=in_specs,
        out_specs=pl.BlockSpec((tq, attn_width), lambda b, i, g: (b * nq + i, 0)),
        out_shape=jax.ShapeDtypeStruct((n, attn_width), bf16),
        scratch_shapes=[pltpu.VMEM((n_hg, tq, w), f32), pltpu.VMEM((n_heads, tq, n_kb * tq), f32)],
        compiler_params=_params(("arbitrary", "arbitrary", "arbitrary"), nbytes),
        name="band_attention",
    )(*([qkv] * (2 * n_kb + 1)), row0, gain_l.reshape(gain_l.shape[0], 1, attn_width))


def _sigmoid(x):
    return 0.5 * jnp.tanh(0.5 * x) + 0.5


def _lru_kernel(xr_ref, xg_ref, cw_ref, cb_ref, wa_ref, ba_ref, wi_ref, bi_ref, lam_ref, gain_ref,
                o_ref, xe_ref, a_ref, b_ref, h_ref, carry_ref, *, t, n_blocks, conv_width):
    i = pl.program_id(1)
    hdr = V7X_SUBLANES
    bw = wa_ref.shape[-1]

    @pl.when(i == 0)
    def _reset():
        xe_ref[0:hdr, :] = jnp.zeros((hdr, xe_ref.shape[1]), f32)
        carry_ref[...] = jnp.zeros_like(carry_ref)

    x = xr_ref[...]
    xe_ref[hdr:hdr + t, :] = x
    u = cb_ref[...] + cw_ref[conv_width - 1:conv_width, :] * x
    for j in range(conv_width - 1):
        u = u + cw_ref[j:j + 1, :] * xe_ref[pl.ds(hdr - (conv_width - 1) + j, t), :]
    xe_ref[0:hdr, :] = x[t - hdr:t, :]

    lam = lam_ref[...]
    nsp = -LRU_C * (jnp.maximum(-lam, 0.0) + jnp.log1p(jnp.exp(-jnp.abs(lam))))
    for nb in range(n_blocks):
        sl = slice(nb * bw, (nb + 1) * bw)
        ub = u[:, sl]
        ubb = ub.astype(bf16)
        r = _sigmoid(jnp.dot(ubb, wa_ref[nb].astype(bf16), preferred_element_type=f32) + ba_ref[:, sl])
        ig = _sigmoid(jnp.dot(ubb, wi_ref[nb].astype(bf16), preferred_element_type=f32) + bi_ref[:, sl])
        log_a = r * nsp[:, sl]
        a = jnp.exp(log_a)
        a_ref[:, sl] = a
        b_ref[:, sl] = jnp.sqrt(1.0 - a * a) * (ig * ub)

    a = a_ref[...]
    b = b_ref[...]
    row = lax.broadcasted_iota(jnp.int32, a.shape, 0) % hdr
    s = 1
    while s < hdr:
        keep = row >= s
        a_sh = jnp.where(keep, pltpu.roll(a, s, 0), 1.0)
        b_sh = jnp.where(keep, pltpu.roll(b, s, 0), 0.0)
        b = a * b_sh + b
        a = a * a_sh
        s *= 2
    a_ref[...] = a
    b_ref[...] = b

    def tile_step(j, carry):
        r0 = pl.multiple_of(j * hdr, hdr)
        h = a_ref[pl.ds(r0, hdr), :] * carry + b_ref[pl.ds(r0, hdr), :]
        h_ref[pl.ds(r0, hdr), :] = h
        return h[hdr - 1:hdr, :]

    carry_ref[...] = lax.fori_loop(0, t // hdr, tile_step, carry_ref[...])

    y = h_ref[...] * jax.nn.gelu(xg_ref[...], approximate=True)
    inv = lax.rsqrt(jnp.mean(y * y, axis=-1, keepdims=True) + EPS)
    o_ref[...] = (y * inv * gain_ref[...]).astype(o_ref.dtype)


def rglru_block(rg, conv_w, conv_b, w_a, b_a, w_i, b_i, lam, gain, layer, batch, seq):
    n = rg.shape[0]
    n_layers, conv_width, wr = conv_w.shape
    n_blocks = w_a.shape[1]
    bw = w_a.shape[2]
    t = min(256, seq)
    ns = seq // t
    row3 = lambda v: v.reshape(n_layers, 1, wr)
    vec_spec = pl.BlockSpec((None, 1, wr), lambda b, i: (layer, 0, 0))
    wspec = pl.BlockSpec((None, n_blocks, bw, bw), lambda b, i: (layer, 0, 0, 0))
    nbytes = 2 * 2 * t * wr * 4 + 2 * t * wr * 2 + 4 * (t + 8) * wr * 4 + 4 * n_blocks * bw * bw * 4 + 8 * t * wr * 4
    return pl.pallas_call(
        functools.partial(_lru_kernel, t=t, n_blocks=n_blocks, conv_width=conv_width),
        grid=(batch, ns),
        in_specs=[pl.BlockSpec((t, wr), lambda b, i: (b * ns + i, 0)),
                  pl.BlockSpec((t, wr), lambda b, i: (b * ns + i, 1)),
                  pl.BlockSpec((None, conv_width, wr), lambda b, i: (layer, 0, 0)),
                  vec_spec, wspec, vec_spec, wspec, vec_spec, vec_spec, vec_spec],
        out_specs=pl.BlockSpec((t, wr), lambda b, i: (b * ns + i, 0)),
        out_shape=jax.ShapeDtypeStruct((n, wr), bf16),
        scratch_shapes=[pltpu.VMEM((t + V7X_SUBLANES, wr), f32), pltpu.VMEM((t, wr), f32),
                        pltpu.VMEM((t, wr), f32), pltpu.VMEM((t, wr), f32), pltpu.VMEM((1, wr), f32)],
        compiler_params=_params(("arbitrary", "arbitrary"), nbytes),
        name="conv_rglru",
    )(rg, rg, conv_w, row3(conv_b), w_a, row3(b_a), w_i, row3(b_i), row3(lam), row3(gain))


def _router_kernel(x_ref, g_ref, whi_ref, wlo_ref, rb_ref, hp_ref, idx_ref, gate_ref, rank_ref, cnt_ref, carry_ref,
                   *, tb, n_exp):
    step = pl.program_id(0)

    @pl.when(step == 0)
    def _reset():
        carry_ref[...] = jnp.zeros_like(carry_ref)

    x = x_ref[...]
    inv = lax.rsqrt(jnp.mean(x * x, axis=-1, keepdims=True) + EPS)
    h = x * inv * g_ref[...]
    half = h.shape[1] // 2
    h_hi = h.astype(bf16)
    h_hi32 = h_hi.astype(f32)
    bits = lax.bitcast_convert_type(h_hi32, jnp.uint32)
    hp_ref[...] = (bits[:, :half] >> 16) | bits[:, half:]

    h_lo = (h - h_hi32).astype(bf16)
    w_hi = whi_ref[...]
    acc = (jnp.dot(h_hi, w_hi, preferred_element_type=f32) + jnp.dot(h_lo, w_hi, preferred_element_type=f32)
           + jnp.dot(h_hi, wlo_ref[...], preferred_element_type=f32))
    logits = acc.T[:n_exp] + rb_ref[...]
    e_iota = lax.broadcasted_iota(jnp.int32, (n_exp, tb), 0)
    tri = (lax.broadcasted_iota(jnp.int32, (tb, tb), 0) < lax.broadcasted_iota(jnp.int32, (tb, tb), 1)).astype(bf16)
    base = carry_ref[...]
    vals, idxs, ranks = [], [], []
    l = logits
    for _ in range(TOP_K):
        m = l.max(axis=0, keepdims=True)
        idx = jnp.min(jnp.where(l == m, e_iota, n_exp), axis=0, keepdims=True)
        onehot = e_iota == idx
        ohf = onehot.astype(f32)
        before = jnp.dot(ohf.astype(bf16), tri, preferred_element_type=f32)
        ranks.append(jnp.sum(jnp.where(onehot, before + base, 0.0), axis=0, keepdims=True))
        base = base + jnp.sum(ohf, axis=1, keepdims=True)
        vals.append(m)
        idxs.append(idx)
        l = jnp.where(onehot, -jnp.inf, l)
    carry_ref[...] = base
    exps = [jnp.exp(v - vals[0]) for v in vals]
    denom = exps[0]
    for e in exps[1:]:
        denom = denom + e
    idx_ref[...] = jnp.concatenate(idxs, axis=0)
    gate_ref[...] = jnp.concatenate([e / denom for e in exps], axis=0)
    rank_ref[...] = jnp.concatenate(ranks, axis=0).astype(jnp.int32)
    cnt_ref[...] = jnp.broadcast_to(base, cnt_ref.shape).astype(jnp.int32)


def router(x2d, gain_l, router_w_l, router_b_l, layer):
    n, d = x2d.shape
    n_exp = router_w_l.shape[-1]
    tb = min(512, n)
    e_pad = -(-n_exp // V7X_LANES) * V7X_LANES
    rw = jnp.pad(router_w_l[layer].astype(f32), ((0, 0), (0, e_pad - n_exp)))
    w_hi = rw.astype(bf16)
    w_lo = (rw - w_hi.astype(f32)).astype(bf16)
    rb = router_b_l[layer].reshape(n_exp, 1)
    slot_spec = pl.BlockSpec((TOP_K, tb), lambda i: (0, i))
    nbytes = 2 * tb * d * 4 + 2 * tb * d * 2 + 4 * e_pad * d * 2 + 6 * tb * d * 4
    return pl.pallas_call(
        functools.partial(_router_kernel, tb=tb, n_exp=n_exp),
        grid=(n // tb,),
        in_specs=[pl.BlockSpec((tb, d), lambda i: (i, 0)),
                  pl.BlockSpec((None, 1, d), lambda i: (layer, 0, 0)),
                  pl.BlockSpec((d, e_pad), lambda i: (0, 0)),
                  pl.BlockSpec((d, e_pad), lambda i: (0, 0)),
                  pl.BlockSpec((n_exp, 1), lambda i: (0, 0))],
        out_specs=[pl.BlockSpec((tb, d // 2), lambda i: (i, 0)), slot_spec, slot_spec, slot_spec,
                   pl.BlockSpec((n_exp, V7X_LANES), lambda i: (0, 0))],
        out_shape=[jax.ShapeDtypeStruct((n, d // 2), jnp.uint32),
                   jax.ShapeDtypeStruct((TOP_K, n), jnp.int32),
                   jax.ShapeDtypeStruct((TOP_K, n), f32),
                   jax.ShapeDtypeStruct((TOP_K, n), jnp.int32),
                   jax.ShapeDtypeStruct((n_exp, V7X_LANES), jnp.int32)],
        scratch_shapes=[pltpu.VMEM((n_exp, 1), f32)],
        compiler_params=_params(("arbitrary",), nbytes),
        name="router_topk",
    )(x2d, gain_l.reshape(gain_l.shape[0], 1, d), w_hi, w_lo, rb)


def _moe_kernel(tile_e_ref, tile_rows_ref, n_used_ref,
                xs_hbm, wgu_ref, bgu_ref, wdn_ref, bdn_ref,
                y_ref,
                stage_ref, xs_ref, act_ref, wgu_bf, wdn_bf, raw_a, raw_b, row_sem,
                *, tm, sub, n_gu, n_dn, tgu, t_max):
    raw_refs = (raw_a, raw_b)
    t = pl.program_id(0)
    s = pl.program_id(1)
    n_used = n_used_ref[0]
    active = t < n_used
    half = stage_ref.shape[-1]
    kh_rows = wgu_bf.shape[0]
    th = tgu // 2
    n_sb = tm // sub
    n_gu_steps = 2 * n_gu
    rows = tile_rows_ref[t]
    ns = (rows + sub - 1) // sub

    def sub_copy(tile, rb):
        return pltpu.make_async_copy(xs_hbm.at[pl.ds(tile * tm + rb * sub, sub), :], stage_ref.at[rb % 2],
                                     row_sem.at[rb % 2])

    def start_tile(tile):
        n_next = (tile_rows_ref[tile] + sub - 1) // sub
        for rb in range(min(2, n_sb)):
            @pl.when(rb < n_next)
            def _():
                sub_copy(tile, rb).start()

    @pl.when(active & (s == 0))
    def _stage_rows():
        @pl.when(t == 0)
        def _first():
            start_tile(0)
            for ref in raw_refs:
                ref[...] = jnp.zeros_like(ref)

        for rb in range(n_sb):
            @pl.when(rb < ns)
            def _unpack():
                sub_copy(t, rb).wait()
                live = lax.broadcasted_iota(jnp.int32, (sub, half), 0) < rows - rb * sub
                w = jnp.where(live, stage_ref[rb % 2], jnp.uint32(0))
                xs_ref[rb * sub:(rb + 1) * sub, 0:half] = lax.bitcast_convert_type(w << 16, f32).astype(bf16)
                xs_ref[rb * sub:(rb + 1) * sub, half:2 * half] = lax.bitcast_convert_type(
                    w & jnp.uint32(0xFFFF0000), f32).astype(bf16)
                if rb + 2 < n_sb:
                    @pl.when(rb + 2 < ns)
                    def _():
                        sub_copy(t, rb + 2).start()

    @pl.when(active & (s == 1) & (t + 1 < n_used))
    def _prefetch_rows():
        start_tile(jnp.minimum(t + 1, t_max - 1))

    def activate(src_ref, rb_src, s_dst):
        sel = (lax.broadcasted_iota(jnp.int32, (tgu, th), 0) == 2 * lax.broadcasted_iota(jnp.int32, (tgu, th), 1)).astype(bf16)
        r_src = pl.multiple_of(rb_src * sub, sub)
        gu = src_ref[pl.ds(r_src, sub), :]
        glu = jnp.minimum(gu, SWIGLU_LIMIT)
        glu = glu * _sigmoid(SWIGLU_ALPHA * glu)
        lin = jnp.clip(gu, -SWIGLU_LIMIT, SWIGLU_LIMIT) + 1.0
        prod = glu * pltpu.roll(lin, tgu - 1, 1)
        act_ref[s_dst, pl.ds(r_src, sub), :] = jnp.dot(prod.astype(bf16), sel, preferred_element_type=f32).astype(bf16)

    n_tile = s // 2
    p_cur = n_tile % 2

    @pl.when(active & (s < n_gu_steps) & (s % 2 == 0))
    def _gate_up_first_half():
        wgu_bf[...] = wgu_ref[...].astype(bf16)

        for p in range(2):
            @pl.when(p_cur == p)
            def _():
                def body(rb, c):
                    r0 = pl.multiple_of(rb * sub, sub)
                    raw_refs[p][pl.ds(r0, sub), :] = jnp.dot(xs_ref[pl.ds(r0, sub), 0:kh_rows], wgu_bf[...],
                                                             preferred_element_type=f32) + bgu_ref[...]
                    activate(raw_refs[1 - p], rb, jnp.where(n_tile == 0, n_gu, n_tile - 1))
                    return c

                lax.fori_loop(0, ns, body, 0)

    @pl.when(active & (s < n_gu_steps) & (s % 2 == 1))
    def _gate_up_second_half():
        wgu_bf[...] = wgu_ref[...].astype(bf16)
        for p in range(2):
            @pl.when(p_cur == p)
            def _():
                def body(rb, c):
                    r0 = pl.multiple_of(rb * sub, sub)
                    raw_refs[p][pl.ds(r0, sub), :] += jnp.dot(xs_ref[pl.ds(r0, sub), kh_rows:2 * kh_rows], wgu_bf[...],
                                                              preferred_element_type=f32)
                    return c

                lax.fori_loop(0, ns, body, 0)

    @pl.when(active & (s == n_gu_steps))
    def _last_activation():
        def body(rb, c):
            activate(raw_refs[(n_gu - 1) % 2], rb, n_gu - 1)
            return c

        lax.fori_loop(0, ns, body, 0)

    @pl.when(active & (s >= n_gu_steps))
    def _down():
        wdn_bf[...] = wdn_ref[...].astype(bf16)

        def body(rb, c):
            r0 = pl.multiple_of(rb * sub, sub)
            a = jnp.concatenate([act_ref[g, pl.ds(r0, sub), :] for g in range(n_gu)], axis=1)
            y_ref[pl.ds(r0, sub), :] = jnp.dot(a, wdn_bf[...], preferred_element_type=f32) + bdn_ref[...]
            return c

        lax.fori_loop(0, ns, body, 0)

        def clear(rb, c):
            r0 = pl.multiple_of(rb * sub, sub)
            y_ref[pl.ds(r0, sub), :] = jnp.zeros((sub, y_ref.shape[1]), f32)
            return c

        lax.fori_loop(ns, n_sb, clear, 0)


def moe_experts(xs_sorted, tile_e, tile_rows, n_used, w_gu, b_gu, w_dn, b_dn, layer, tm, sub):
    half = xs_sorted.shape[1]
    d = 2 * half
    n_layers, n_exp, _, two_f = w_gu.shape
    ff = two_f // 2
    t_max = xs_sorted.shape[0] // tm
    tgu = min(512, two_f)
    tdn = min(512, d)
    n_gu, n_dn = two_f // tgu, d // tdn

    def eff(t, s, te, nu):
        live = t < nu[0]
        return jnp.where(live, t, jnp.maximum(nu[0] - 1, 0)), live

    n_gu_steps = 2 * n_gu

    def gu_idx(t, s, te, tr, nu):
        _, live = eff(t, s, te, nu)
        step = jnp.where(live, jnp.minimum(s, n_gu_steps - 1), n_gu_steps - 1)
        return (layer, te[t], step % 2, step // 2)

    def bgu_idx(t, s, te, tr, nu):
        _, live = eff(t, s, te, nu)
        return (layer, te[t], 0, jnp.where(live, jnp.minimum(s, n_gu_steps - 1), n_gu_steps - 1) // 2)

    def dn_idx(t, s, te, tr, nu):
        _, live = eff(t, s, te, nu)
        return (layer, te[t], 0, jnp.where(live, jnp.maximum(s - n_gu_steps, 0), n_dn - 1))

    def y_idx(t, s, te, tr, nu):
        tt, live = eff(t, s, te, nu)
        return (tt, jnp.where(live, jnp.maximum(s - n_gu_steps, 0), n_dn - 1))

    nbytes = (2 * sub * half * 4 + tm * d * 2 + (n_gu + 1) * tm * tgu + 2 * (d // 2) * tgu * 4 + (d // 2) * tgu * 2
              + 2 * ff * tdn * 4 + ff * tdn * 2 + 2 * tm * tgu * 4 + 2 * tm * tdn * 4)
    grid_spec = pltpu.PrefetchScalarGridSpec(
        num_scalar_prefetch=3,
        grid=(t_max, n_gu_steps + n_dn),
        in_specs=[pl.BlockSpec(memory_space=pl.ANY),
                  pl.BlockSpec((None, None, d // 2, tgu), gu_idx),
                  pl.BlockSpec((None, None, 1, tgu), bgu_idx),
                  pl.BlockSpec((None, None, ff, tdn), dn_idx),
                  pl.BlockSpec((None, None, 1, tdn), dn_idx)],
        out_specs=pl.BlockSpec((tm, tdn), y_idx),
        scratch_shapes=[pltpu.VMEM((2, sub, half), jnp.uint32),
                        pltpu.VMEM((tm, d), bf16),
                        pltpu.VMEM((n_gu + 1, tm, tgu // 2), bf16),
                        pltpu.VMEM((d // 2, tgu), bf16),
                        pltpu.VMEM((ff, tdn), bf16),
                        pltpu.VMEM((tm, tgu), f32),
                        pltpu.VMEM((tm, tgu), f32),
                        pltpu.SemaphoreType.DMA((2,))],
    )
    return pl.pallas_call(
        functools.partial(_moe_kernel, tm=tm, sub=sub, n_gu=n_gu, n_dn=n_dn, tgu=tgu, t_max=t_max),
        grid_spec=grid_spec,
        out_shape=jax.ShapeDtypeStruct((t_max * tm, d), f32),
        compiler_params=_params(("arbitrary", "arbitrary"), nbytes),
        name="moe_experts",
    )(tile_e, tile_rows, n_used, xs_sorted, w_gu,
      b_gu.reshape(n_layers, n_exp, 1, two_f), w_dn, b_dn.reshape(n_layers, n_exp, 1, d))


def _dispatch_kernel(dest_hbm, hp_ref, xs_hbm, dest_smem, dest_sem, row_sem, *, tb, n_steps):
    i = pl.program_id(0)
    slot = i % 2

    def dest_copy(step):
        return pltpu.make_async_copy(dest_hbm.at[step], dest_smem.at[step % 2], dest_sem)

    def row_copy(r, row):
        return pltpu.make_async_copy(hp_ref.at[pl.ds(r, 1), :], xs_hbm.at[pl.ds(row, 1), :], row_sem)

    @pl.when(i == 0)
    def _first():
        dest_copy(0).start()

    dest_copy(i).wait()

    @pl.when(i + 1 < n_steps)
    def _next():
        dest_copy(i + 1).start()

    def issue(r, c):
        for k in range(TOP_K):
            row_copy(r, dest_smem[slot, k * tb + r]).start()
        return c

    lax.fori_loop(0, tb, issue, 0, unroll=2)
    for _ in range(TOP_K):
        pltpu.make_async_copy(hp_ref, xs_hbm.at[pl.ds(0, tb), :], row_sem).wait()


def dispatch(hp, dest_blk, n_rows_out):
    n, half = hp.shape
    n_steps, per_step = dest_blk.shape
    tb = per_step // TOP_K
    return pl.pallas_call(
        functools.partial(_dispatch_kernel, tb=tb, n_steps=n_steps),
        grid=(n_steps,),
        in_specs=[pl.BlockSpec(memory_space=pl.ANY),
                  pl.BlockSpec((tb, half), lambda i: (i, 0))],
        out_specs=pl.BlockSpec(memory_space=pl.ANY),
        out_shape=jax.ShapeDtypeStruct((n_rows_out, half), jnp.uint32),
        scratch_shapes=[pltpu.SMEM((2, per_step), jnp.int32),
                        pltpu.SemaphoreType.DMA(()),
                        pltpu.SemaphoreType.DMA(())],
        compiler_params=_params(("arbitrary",), 2 * tb * half * 4),
        name="moe_dispatch",
    )(dest_blk, hp)


def _combine_kernel(dest_hbm, y_hbm, x_ref, gate_ref, gain_ref, *refs, tb, n_steps, final):
    if final:
        o_ref, dest_smem, ybuf, dest_sem, row_sem = refs
        xo_ref = None
    else:
        xo_ref, o_ref, dest_smem, ybuf, dest_sem, row_sem = refs
    i = pl.program_id(0)
    n_rows = TOP_K * tb

    def dest_copy(step):
        return pltpu.make_async_copy(dest_hbm.at[step], dest_smem.at[step % 2], dest_sem)

    def start_gather(step):
        for slot in range(2):
            @pl.when(step % 2 == slot)
            def _():
                def issue(r, c):
                    pltpu.make_async_copy(y_hbm.at[pl.ds(dest_smem[slot, r], 1), :], ybuf.at[slot, pl.ds(r, 1), :],
                                          row_sem.at[slot]).start()
                    return c

                lax.fori_loop(0, n_rows, issue, 0, unroll=8)

    @pl.when(i == 0)
    def _first():
        dest_copy(0).start()
        dest_copy(0).wait()
        start_gather(0)

        if n_steps > 1:
            dest_copy(1).start()

    @pl.when(i + 1 < n_steps)
    def _next():
        dest_copy(i + 1).wait()
        start_gather(i + 1)

        @pl.when(i + 2 < n_steps)
        def _():
            dest_copy(i + 2).start()

    pltpu.make_async_copy(y_hbm.at[pl.ds(0, n_rows), :], ybuf.at[i % 2], row_sem.at[i % 2]).wait()

    g = gate_ref[...]
    acc = x_ref[...]
    for k in range(TOP_K):
        acc = acc + g[:, k:k + 1] * ybuf[i % 2, k * tb:(k + 1) * tb, :]
    if not final:
        xo_ref[...] = acc
    inv = lax.rsqrt(jnp.mean(acc * acc, axis=-1, keepdims=True) + EPS)
    o_ref[...] = (acc * inv * gain_ref[...]).astype(o_ref.dtype)


def combine(x2d, y, dest_blk, gates_t, gain2d, final):
    n, d = x2d.shape
    n_steps, per_step = dest_blk.shape
    tb = per_step // TOP_K
    row_spec = pl.BlockSpec((tb, d), lambda i: (i, 0))
    if final:
        out_shape = jax.ShapeDtypeStruct((n, d), f32)
        out_specs = row_spec
    else:
        out_shape = [jax.ShapeDtypeStruct((n, d), f32), jax.ShapeDtypeStruct((n, d), bf16)]
        out_specs = [row_spec, row_spec]
    nbytes = 2 * TOP_K * tb * d * 4 + 6 * tb * d * 4
    return pl.pallas_call(
        functools.partial(_combine_kernel, tb=tb, n_steps=n_steps, final=final),
        grid=(n_steps,),
        in_specs=[pl.BlockSpec(memory_space=pl.ANY),
                  pl.BlockSpec(memory_space=pl.ANY),
                  row_spec,
                  pl.BlockSpec((tb, TOP_K), lambda i: (i, 0)),
                  pl.BlockSpec((1, d), lambda i: (0, 0))],
        out_specs=out_specs,
        out_shape=out_shape,
        scratch_shapes=[pltpu.SMEM((2, TOP_K * tb), jnp.int32),
                        pltpu.VMEM((2, TOP_K * tb, d), f32),
                        pltpu.SemaphoreType.DMA(()),
                        pltpu.SemaphoreType.DMA((2,))],
        compiler_params=_params(("arbitrary",), nbytes),
        name="moe_combine",
    )(dest_blk, y, x2d, gates_t, gain2d)


def _dispatch_plan(counts, idx, rank, tm, t_max, tb):
    n_exp = counts.shape[0]
    n = idx.shape[1]
    tiles_e = (counts + tm - 1) // tm
    tile_end = jnp.cumsum(tiles_e)
    tile_start = tile_end - tiles_e
    n_used = tile_end[-1]
    expert = jnp.arange(n_exp, dtype=jnp.int32)[:, None, None]
    dest = rank + jnp.sum(jnp.where(idx[None] == expert, (tile_start * tm)[:, None, None], 0), axis=0)
    tile = jnp.arange(t_max, dtype=jnp.int32)
    tile_c = jnp.minimum(tile, n_used - 1)
    tile_e = jnp.minimum(jnp.searchsorted(tile_end, tile_c, side="right"), n_exp - 1).astype(jnp.int32)
    rows = jnp.clip(counts[tile_e] - (tile_c - tile_start[tile_e]) * tm, 0, tm)
    tile_rows = jnp.where(tile < n_used, rows, 0).astype(jnp.int32)
    dest_blk = dest.reshape(TOP_K, n // tb, tb).transpose(1, 0, 2).reshape(n // tb, TOP_K * tb)
    return dest_blk, tile_e, tile_rows, n_used.reshape(1).astype(jnp.int32)


def kernel(x, norm_mix, w_in, attn_rel_bias, attn_out_gain, conv_w, conv_b, lru_w_a, lru_b_a, lru_w_i, lru_b_i,
           lru_lambda, rec_out_gain, w_out, norm_ffn, router_w, router_b, expert_w_gate_up, expert_b_gate_up,
           expert_w_down, expert_b_down, norm_final):
    batch, seq, d = x.shape
    n = batch * seq
    depth = norm_mix.shape[0]
    attn_width = attn_out_gain.shape[1]
    lru_width = rec_out_gain.shape[1]
    n_exp = router_w.shape[-1]
    sub = min(512, n)
    tm = sub * min(3, n // sub)
    t_max = (n * TOP_K) // tm + n_exp
    tb = min(128, n)

    x2d = x.reshape(n, d)
    h = rmsnorm_bf16(x2d, norm_mix, 0)
    out = None
    for l in range(depth):
        qkv = matmul_ws([h], w_in, l, [0], 0, 3 * attn_width, bf16, name="in_proj_qkv")
        rg = matmul_ws([h], w_in, l, [0], 3 * attn_width, 2 * lru_width, f32, name="in_proj_rec")
        attn = attention(qkv, attn_rel_bias[l], attn_out_gain, l, batch, seq, attn_width)
        rec = rglru_block(rg, conv_w, conv_b, lru_w_a, lru_b_a, lru_w_i, lru_b_i, lru_lambda, rec_out_gain,
                          l, batch, seq)
        assert attn_width == lru_width
        x2d = matmul_ws([attn, rec], w_out, l, [0, 1], 0, d, f32, residual=x2d, name="out_proj")
        hp, idx, gates, rank, cnt = router(x2d, norm_ffn, router_w, router_b, l)
        dest_blk, tile_e, tile_rows, n_used = _dispatch_plan(cnt[:, 0], idx, rank, tm, t_max, tb)
        xs_sorted = dispatch(hp, dest_blk, t_max * tm)
        y = moe_experts(xs_sorted, tile_e, tile_rows, n_used, expert_w_gate_up, expert_b_gate_up,
                        expert_w_down, expert_b_down, l, tm, sub)
        final = l == depth - 1
        gain = norm_final.reshape(1, d) if final else norm_mix[l + 1].reshape(1, d)
        res = combine(x2d, y, dest_blk, gates.T, gain, final)
        if final:
            out = res
        else:
            x2d, h = res
    return out.reshape(batch, seq, d)
```

```python
import functools
import math

import jax
import jax.numpy as jnp
from jax import lax
from jax.experimental import pallas as pl
from jax.experimental.pallas import tpu as pltpu

CHUNK = 64
LEFT_CHUNKS = 8
ATTN_HEAD_DIM = 128
REL_CLIP = 128
LRU_C = 8.0
TOP_K = 4
SWIGLU_LIMIT = 7.0
SWIGLU_ALPHA = 1.702
EPS = 1e-6

V7X_LANES = 128
V7X_SUBLANES = 8
V7X_VMEM_BYTES = 64 * 1024 * 1024

MASK_VALUE = -1e30

f32 = jnp.float32
bf16 = jnp.bfloat16


def _vmem_limit(nbytes):
    return int(min(nbytes + 12 * 1024 * 1024, V7X_VMEM_BYTES - 6 * 1024 * 1024))


def _params(sem, nbytes):
    return pltpu.CompilerParams(dimension_semantics=sem, vmem_limit_bytes=_vmem_limit(nbytes))


def _rmsnorm_kernel(x_ref, g_ref, o_ref):
    x = x_ref[...]
    inv = lax.rsqrt(jnp.mean(x * x, axis=-1, keepdims=True) + EPS)
    o_ref[...] = (x * inv * g_ref[...]).astype(o_ref.dtype)


def rmsnorm_bf16(x2d, gain_l, layer):
    n, d = x2d.shape
    tb = min(256, n)
    return pl.pallas_call(
        _rmsnorm_kernel,
        grid=(n // tb,),
        in_specs=[pl.BlockSpec((tb, d), lambda i: (i, 0)),
                  pl.BlockSpec((None, 1, d), lambda i: (layer, 0, 0))],
        out_specs=pl.BlockSpec((tb, d), lambda i: (i, 0)),
        out_shape=jax.ShapeDtypeStruct((n, d), bf16),
        compiler_params=_params(("arbitrary",), 2 * tb * d * 6),
        name="rmsnorm_bf16",
    )(x2d, gain_l.reshape(gain_l.shape[0], 1, d))


def _matmul_kernel(*refs, n_pairs, has_res):
    a_refs = refs[:n_pairs]
    w_refs = refs[n_pairs:2 * n_pairs]
    pos = 2 * n_pairs
    res_ref = refs[pos] if has_res else None
    pos += int(has_res)
    o_ref = refs[pos]
    wbf_refs = refs[pos + 1:pos + 1 + n_pairs]

    @pl.when(pl.program_id(1) == 0)
    def _cast_weights():
        for w_ref, wbf_ref in zip(w_refs, wbf_refs):
            wbf_ref[...] = w_ref[...].astype(bf16)

    acc = jnp.dot(a_refs[0][...], wbf_refs[0][...], preferred_element_type=f32)
    for a_ref, wbf_ref in zip(a_refs[1:], wbf_refs[1:]):
        acc += jnp.dot(a_ref[...], wbf_ref[...], preferred_element_type=f32)
    if has_res:
        acc += res_ref[...]
    o_ref[...] = acc.astype(o_ref.dtype)


def matmul_ws(a_list, w_stack, layer, k_blocks, col0, n_cols, out_dtype, residual=None, name="matmul_ws"):
    m = a_list[0].shape[0]
    tm = min(1024, m)
    tn = min(512, n_cols)
    assert m % tm == 0 and n_cols % tn == 0 and col0 % tn == 0
    cb0 = col0 // tn
    n_pairs = len(a_list)
    in_specs, scratch = [], []
    nbytes = 2 * tm * tn * 4 * (2 if residual is not None else 1)
    for a in a_list:
        kp = a.shape[1]
        in_specs.append(pl.BlockSpec((tm, kp), lambda j, i: (i, 0)))
        nbytes += 2 * tm * kp * 2
    for a, kb in zip(a_list, k_blocks):
        kp = a.shape[1]
        in_specs.append(pl.BlockSpec((None, kp, tn), functools.partial(lambda j, i, kb: (layer, kb, j + cb0), kb=kb)))
        scratch.append(pltpu.VMEM((kp, tn), bf16))
        nbytes += 2 * kp * tn * 4 + kp * tn * 2
    args = list(a_list) + [w_stack] * n_pairs
    if residual is not None:
        in_specs.append(pl.BlockSpec((tm, tn), lambda j, i: (i, j)))
        args.append(residual)
    return pl.pallas_call(
        functools.partial(_matmul_kernel, n_pairs=n_pairs, has_res=residual is not None),
        grid=(n_cols // tn, m // tm),
        in_specs=in_specs,
        out_specs=pl.BlockSpec((tm, tn), lambda j, i: (i, j)),
        out_shape=jax.ShapeDtypeStruct((m, n_cols), out_dtype),
        scratch_shapes=scratch,
        compiler_params=_params(("arbitrary", "arbitrary"), nbytes),
        name=name,
    )(*args)


def _attn_kernel(q_ref, *refs, n_kb, hb, tq, n_hg):
    k_refs = refs[:n_kb]
    v_refs = refs[n_kb:2 * n_kb]
    row0_ref, gain_ref, o_ref, raw_ref, bias_ref = refs[2 * n_kb:]
    i = pl.program_id(1)
    hg = pl.program_id(2)
    dh = ATTN_HEAD_DIM
    scale = 1.0 / math.sqrt(dh)
    n_keys = n_kb * tq

    @pl.when((pl.program_id(0) == 0) & (i == 0) & (hg == 0))
    def _build_bias():
        width = row0_ref.shape[-1]
        qc = ((n_kb - 1) * tq + lax.broadcasted_iota(jnp.int32, (tq, n_keys), 0)) // CHUNK
        kc = lax.broadcasted_iota(jnp.int32, (tq, n_keys), 1) // CHUNK
        valid = (kc <= qc) & (kc >= qc - LEFT_CHUNKS)
        for h in range(bias_ref.shape[0]):
            rows = pltpu.roll(jnp.broadcast_to(row0_ref[h], (tq, width)), 0, 1, stride=1, stride_axis=0)
            bias_ref[h] = jnp.where(valid, rows[:, :n_keys], MASK_VALUE)

    outs = []
    for hh in range(hb):
        sl = slice(hh * dh, (hh + 1) * dh)
        q = q_ref[:, sl]
        bias = bias_ref[hg * hb + hh]
        parts = []
        for j in range(n_kb - 1, -1, -1):
            s = lax.dot_general(q, k_refs[j][:, sl], (((1,), (1,)), ((), ())), preferred_element_type=f32)
            c0 = (n_kb - 1 - j) * tq
            s = s * scale + bias[:, c0:c0 + tq]
            if j > 0:
                s = s + jnp.where(i >= j, 0.0, MASK_VALUE)
            parts.append(s)
        m = parts[0].max(axis=-1, keepdims=True)
        for s in parts[1:]:
            m = jnp.maximum(m, s.max(axis=-1, keepdims=True))
        l = jnp.zeros_like(m)
        acc = jnp.zeros((tq, dh), f32)
        for idx, s in enumerate(parts):
            j = n_kb - 1 - idx
            p = jnp.exp(s - m)
            l = l + p.sum(axis=-1, keepdims=True)
            acc = acc + jnp.dot(p.astype(bf16), v_refs[j][:, sl], preferred_element_type=f32)
        outs.append(acc / l)
    raw_ref[hg] = jnp.concatenate(outs, axis=-1)

    @pl.when(hg == n_hg - 1)
    def _finish():
        ss = jnp.zeros((tq, 1), f32)
        for g in range(n_hg):
            r = raw_ref[g]
            ss = ss + jnp.sum(r * r, axis=-1, keepdims=True)
        inv = lax.rsqrt(ss / (n_hg * hb * dh) + EPS)
        for g in range(n_hg):
            w = hb * dh
            o_ref[:, g * w:(g + 1) * w] = (raw_ref[g] * inv * gain_ref[:, g * w:(g + 1) * w]).astype(o_ref.dtype)


def _attn_bias_row0(rel_bias_l, tq, n_kb):
    n_heads = rel_bias_l.shape[0]
    rb = rel_bias_l.astype(f32)
    far = (n_kb - 1) * tq - REL_CLIP
    near = n_kb * tq - far - (2 * REL_CLIP + 1)
    assert far >= 0 and near >= 0
    fill = lambda col, width: jnp.broadcast_to(rb[:, col:col + 1], (n_heads, width))
    row0 = jnp.concatenate([fill(2 * REL_CLIP, far), rb[:, ::-1], fill(0, near), fill(2 * REL_CLIP, tq)], axis=1)
    return row0.reshape(n_heads, 1, (n_kb + 1) * tq)


def attention(qkv, rel_bias_l, gain_l, layer, batch, seq, attn_width):
    n = qkv.shape[0]
    n_heads = attn_width // ATTN_HEAD_DIM
    tq = min(256, seq)
    band = LEFT_CHUNKS * CHUNK
    assert band % tq == 0 and seq % tq == 0 and tq % CHUNK == 0
    n_kb = band // tq + 1
    nq = seq // tq
    hb = min(4, n_heads)
    n_hg = n_heads // hb
    w = hb * ATTN_HEAD_DIM
    row0 = _attn_bias_row0(rel_bias_l, tq, n_kb)

    def kv_spec(j, which):
        return pl.BlockSpec((tq, w), lambda b, i, g: (b * nq + jnp.maximum(i - j, 0), which * n_hg + g))

    in_specs = [pl.BlockSpec((tq, w), lambda b, i, g: (b * nq + i, g))]
    in_specs += [kv_spec(j, 1) for j in range(n_kb)] + [kv_spec(j, 2) for j in range(n_kb)]
    in_specs += [pl.BlockSpec(row0.shape, lambda b, i, g: (0, 0, 0)),
                 pl.BlockSpec((None, 1, attn_width), lambda b, i, g: (layer, 0, 0))]
    nbytes = 2 * (2 * n_kb + 1) * tq * w * 2 + n_heads * tq * n_kb * tq * 4 + 3 * tq * attn_width * 4
    return pl.pallas_call(
        functools.partial(_attn_kernel, n_kb=n_kb, hb=hb, tq=tq, n_hg=n_hg),
        grid=(batch, nq, n_hg),
        in_specs=in_specs,
        out_specs=pl.BlockSpec((tq, attn_width), lambda b, i, g: (b * nq + i, 0)),
        out_shape=jax.ShapeDtypeStruct((n, attn_width), bf16),
        scratch_shapes=[pltpu.VMEM((n_hg, tq, w), f32), pltpu.VMEM((n_heads, tq, n_kb * tq), f32)],
        compiler_params=_params(("arbitrary", "arbitrary", "arbitrary"), nbytes),
        name="band_attention",
    )(*([qkv] * (2 * n_kb + 1)), row0, gain_l.reshape(gain_l.shape[0], 1, attn_width))


def _sigmoid(x):
    return 0.5 * jnp.tanh(0.5 * x) + 0.5


def _lru_kernel(xr_ref, xg_ref, cw_ref, cb_ref, wa_ref, ba_ref, wi_ref, bi_ref, lam_ref, gain_ref,
                o_ref, xe_ref, a_ref, b_ref, h_ref, carry_ref, *, t, n_blocks, conv_width):
    i = pl.program_id(1)
    hdr = V7X_SUBLANES
    bw = wa_ref.shape[-1]

    @pl.when(i == 0)
    def _reset():
        xe_ref[0:hdr, :] = jnp.zeros((hdr, xe_ref.shape[1]), f32)
        carry_ref[...] = jnp.zeros_like(carry_ref)

    x = xr_ref[...]
    xe_ref[hdr:hdr + t, :] = x
    u = cb_ref[...] + cw_ref[conv_width - 1:conv_width, :] * x
    for j in range(conv_width - 1):
        u = u + cw_ref[j:j + 1, :] * xe_ref[pl.ds(hdr - (conv_width - 1) + j, t), :]
    xe_ref[0:hdr, :] = x[t - hdr:t, :]

    lam = lam_ref[...]
    nsp = -LRU_C * (jnp.maximum(-lam, 0.0) + jnp.log1p(jnp.exp(-jnp.abs(lam))))
    for nb in range(n_blocks):
        sl = slice(nb * bw, (nb + 1) * bw)
        ub = u[:, sl]
        ubb = ub.astype(bf16)
        r = _sigmoid(jnp.dot(ubb, wa_ref[nb].astype(bf16), preferred_element_type=f32) + ba_ref[:, sl])
        ig = _sigmoid(jnp.dot(ubb, wi_ref[nb].astype(bf16), preferred_element_type=f32) + bi_ref[:, sl])
        log_a = r * nsp[:, sl]
        a = jnp.exp(log_a)
        a_ref[:, sl] = a
        b_ref[:, sl] = jnp.sqrt(1.0 - a * a) * (ig * ub)

    a = a_ref[...]
    b = b_ref[...]
    row = lax.broadcasted_iota(jnp.int32, a.shape, 0) % hdr
    s = 1
    while s < hdr:
        keep = row >= s
        a_sh = jnp.where(keep, pltpu.roll(a, s, 0), 1.0)
        b_sh = jnp.where(keep, pltpu.roll(b, s, 0), 0.0)
        b = a * b_sh + b
        a = a * a_sh
        s *= 2
    a_ref[...] = a
    b_ref[...] = b

    def tile_step(j, carry):
        r0 = pl.multiple_of(j * hdr, hdr)
        h = a_ref[pl.ds(r0, hdr), :] * carry + b_ref[pl.ds(r0, hdr), :]
        h_ref[pl.ds(r0, hdr), :] = h
        return h[hdr - 1:hdr, :]

    carry_ref[...] = lax.fori_loop(0, t // hdr, tile_step, carry_ref[...])

    y = h_ref[...] * jax.nn.gelu(xg_ref[...], approximate=True)
    inv = lax.rsqrt(jnp.mean(y * y, axis=-1, keepdims=True) + EPS)
    o_ref[...] = (y * inv * gain_ref[...]).astype(o_ref.dtype)


def rglru_block(rg, conv_w, conv_b, w_a, b_a, w_i, b_i, lam, gain, layer, batch, seq):
    n = rg.shape[0]
    n_layers, conv_width, wr = conv_w.shape
    n_blocks = w_a.shape[1]
    bw = w_a.shape[2]
    t = min(256, seq)
    ns = seq // t
    row3 = lambda v: v.reshape(n_layers, 1, wr)
    vec_spec = pl.BlockSpec((None, 1, wr), lambda b, i: (layer, 0, 0))
    wspec = pl.BlockSpec((None, n_blocks, bw, bw), lambda b, i: (layer, 0, 0, 0))
    nbytes = 2 * 2 * t * wr * 4 + 2 * t * wr * 2 + 4 * (t + 8) * wr * 4 + 4 * n_blocks * bw * bw * 4 + 8 * t * wr * 4
    return pl.pallas_call(
        functools.partial(_lru_kernel, t=t, n_blocks=n_blocks, conv_width=conv_width),
        grid=(batch, ns),
        in_specs=[pl.BlockSpec((t, wr), lambda b, i: (b * ns + i, 0)),
                  pl.BlockSpec((t, wr), lambda b, i: (b * ns + i, 1)),
                  pl.BlockSpec((None, conv_width, wr), lambda b, i: (layer, 0, 0)),
                  vec_spec, wspec, vec_spec, wspec, vec_spec, vec_spec, vec_spec],
        out_specs=pl.BlockSpec((t, wr), lambda b, i: (b * ns + i, 0)),
        out_shape=jax.ShapeDtypeStruct((n, wr), bf16),
        scratch_shapes=[pltpu.VMEM((t + V7X_SUBLANES, wr), f32), pltpu.VMEM((t, wr), f32),
                        pltpu.VMEM((t, wr), f32), pltpu.VMEM((t, wr), f32), pltpu.VMEM((1, wr), f32)],
        compiler_params=_params(("arbitrary", "arbitrary"), nbytes),
        name="conv_rglru",
    )(rg, rg, conv_w, row3(conv_b), w_a, row3(b_a), w_i, row3(b_i), row3(lam), row3(gain))


def _router_kernel(x_ref, g_ref, whi_ref, wlo_ref, rb_ref, hp_ref, idx_ref, gate_ref, rank_ref, cnt_ref, carry_ref,
                   *, tb, n_exp):
    step = pl.program_id(0)

    @pl.when(step == 0)
    def _reset():
        carry_ref[...] = jnp.zeros_like(carry_ref)

    x = x_ref[...]
    inv = lax.rsqrt(jnp.mean(x * x, axis=-1, keepdims=True) + EPS)
    h = x * inv * g_ref[...]
    half = h.shape[1] // 2
    h_hi = h.astype(bf16)
    h_hi32 = h_hi.astype(f32)
    bits = lax.bitcast_convert_type(h_hi32, jnp.uint32)
    hp_ref[...] = (bits[:, :half] >> 16) | bits[:, half:]

    h_lo = (h - h_hi32).astype(bf16)
    w_hi = whi_ref[...]
    acc = (jnp.dot(h_hi, w_hi, preferred_element_type=f32) + jnp.dot(h_lo, w_hi, preferred_element_type=f32)
           + jnp.dot(h_hi, wlo_ref[...], preferred_element_type=f32))
    logits = acc.T[:n_exp] + rb_ref[...]
    e_iota = lax.broadcasted_iota(jnp.int32, (n_exp, tb), 0)
    tri = (lax.broadcasted_iota(jnp.int32, (tb, tb), 0) < lax.broadcasted_iota(jnp.int32, (tb, tb), 1)).astype(bf16)
    base = carry_ref[...]
    vals, idxs, ranks = [], [], []
    l = logits
    for _ in range(TOP_K):
        m = l.max(axis=0, keepdims=True)
        idx = jnp.min(jnp.where(l == m, e_iota, n_exp), axis=0, keepdims=True)
        onehot = e_iota == idx
        ohf = onehot.astype(f32)
        before = jnp.dot(ohf.astype(bf16), tri, preferred_element_type=f32)
        ranks.append(jnp.sum(jnp.where(onehot, before + base, 0.0), axis=0, keepdims=True))
        base = base + jnp.sum(ohf, axis=1, keepdims=True)
        vals.append(m)
        idxs.append(idx)
        l = jnp.where(onehot, -jnp.inf, l)
    carry_ref[...] = base
    exps = [jnp.exp(v - vals[0]) for v in vals]
    denom = exps[0]
    for e in exps[1:]:
        denom = denom + e
    idx_ref[...] = jnp.concatenate(idxs, axis=0)
    gate_ref[...] = jnp.concatenate([e / denom for e in exps], axis=0)
    rank_ref[...] = jnp.concatenate(ranks, axis=0).astype(jnp.int32)
    cnt_ref[...] = jnp.broadcast_to(base, cnt_ref.shape).astype(jnp.int32)


def router(x2d, gain_l, router_w_l, router_b_l, layer):
    n, d = x2d.shape
    n_exp = router_w_l.shape[-1]
    tb = min(512, n)
    e_pad = -(-n_exp // V7X_LANES) * V7X_LANES
    rw = jnp.pad(router_w_l[layer].astype(f32), ((0, 0), (0, e_pad - n_exp)))
    w_hi = rw.astype(bf16)
    w_lo = (rw - w_hi.astype(f32)).astype(bf16)
    rb = router_b_l[layer].reshape(n_exp, 1)
    slot_spec = pl.BlockSpec((TOP_K, tb), lambda i: (0, i))
    nbytes = 2 * tb * d * 4 + 2 * tb * d * 2 + 4 * e_pad * d * 2 + 6 * tb * d * 4
    return pl.pallas_call(
        functools.partial(_router_kernel, tb=tb, n_exp=n_exp),
        grid=(n // tb,),
        in_specs=[pl.BlockSpec((tb, d), lambda i: (i, 0)),
                  pl.BlockSpec((None, 1, d), lambda i: (layer, 0, 0)),
                  pl.BlockSpec((d, e_pad), lambda i: (0, 0)),
                  pl.BlockSpec((d, e_pad), lambda i: (0, 0)),
                  pl.BlockSpec((n_exp, 1), lambda i: (0, 0))],
        out_specs=[pl.BlockSpec((tb, d // 2), lambda i: (i, 0)), slot_spec, slot_spec, slot_spec,
                   pl.BlockSpec((n_exp, V7X_LANES), lambda i: (0, 0))],
        out_shape=[jax.ShapeDtypeStruct((n, d // 2), jnp.uint32),
                   jax.ShapeDtypeStruct((TOP_K, n), jnp.int32),
                   jax.ShapeDtypeStruct((TOP_K, n), f32),
                   jax.ShapeDtypeStruct((TOP_K, n), jnp.int32),
                   jax.ShapeDtypeStruct((n_exp, V7X_LANES), jnp.int32)],
        scratch_shapes=[pltpu.VMEM((n_exp, 1), f32)],
        compiler_params=_params(("arbitrary",), nbytes),
        name="router_topk",
    )(x2d, gain_l.reshape(gain_l.shape[0], 1, d), w_hi, w_lo, rb)


def _moe_kernel(tile_e_ref, tile_rows_ref, n_used_ref,
                xs_hbm, wgu_ref, bgu_ref, wdn_ref, bdn_ref,
                y_ref,
                stage_ref, xs_ref, act_ref, wgu_bf, wdn_bf, raw_a, raw_b, row_sem,
                *, tm, sub, n_gu, n_dn, tgu):
    raw_refs = (raw_a, raw_b)
    t = pl.program_id(0)
    s = pl.program_id(1)
    n_used = n_used_ref[0]
    half = stage_ref.shape[-1]
    kh_rows = wgu_bf.shape[0]
    th = tgu // 2
    n_sb = tm // sub
    n_gu_steps = 2 * n_gu
    rows = tile_rows_ref[t]
    ns = (rows + sub - 1) // sub

    def sub_copy(tile, rb):
        return pltpu.make_async_copy(xs_hbm.at[pl.ds(tile * tm + rb * sub, sub), :], stage_ref.at[rb % 2],
                                     row_sem.at[rb % 2])

    def start_tile(tile):
        n_next = (tile_rows_ref[tile] + sub - 1) // sub
        for rb in range(min(2, n_sb)):
            @pl.when(rb < n_next)
            def _():
                sub_copy(tile, rb).start()

    @pl.when(s == 0)
    def _stage_rows():
        @pl.when(t == 0)
        def _first():
            start_tile(0)
            for ref in raw_refs:
                ref[...] = jnp.zeros_like(ref)

        for rb in range(n_sb):
            @pl.when(rb < ns)
            def _unpack():
                sub_copy(t, rb).wait()
                live = lax.broadcasted_iota(jnp.int32, (sub, half), 0) < rows - rb * sub
                w = jnp.where(live, stage_ref[rb % 2], jnp.uint32(0))
                xs_ref[rb * sub:(rb + 1) * sub, 0:half] = lax.bitcast_convert_type(w << 16, f32).astype(bf16)
                xs_ref[rb * sub:(rb + 1) * sub, half:2 * half] = lax.bitcast_convert_type(
                    w & jnp.uint32(0xFFFF0000), f32).astype(bf16)
                if rb + 2 < n_sb:
                    @pl.when(rb + 2 < ns)
                    def _():
                        sub_copy(t, rb + 2).start()

    @pl.when((s == 1) & (t + 1 < n_used))
    def _prefetch_rows():
        start_tile(t + 1)

    def activate(src_ref, rb_src, s_dst):
        sel = (lax.broadcasted_iota(jnp.int32, (tgu, th), 0) == 2 * lax.broadcasted_iota(jnp.int32, (tgu, th), 1)).astype(bf16)
        r_src = pl.multiple_of(rb_src * sub, sub)
        gu = src_ref[pl.ds(r_src, sub), :]
        glu = jnp.minimum(gu, SWIGLU_LIMIT)
        glu = glu * _sigmoid(SWIGLU_ALPHA * glu)
        lin = jnp.clip(gu, -SWIGLU_LIMIT, SWIGLU_LIMIT) + 1.0
        prod = glu * pltpu.roll(lin, tgu - 1, 1)
        act_ref[s_dst, pl.ds(r_src, sub), :] = jnp.dot(prod.astype(bf16), sel, preferred_element_type=f32).astype(bf16)

    n_tile = s // 2
    p_cur = n_tile % 2

    @pl.when((s < n_gu_steps) & (s % 2 == 0))
    def _gate_up_first_half():
        wgu_bf[...] = wgu_ref[...].astype(bf16)

        for p in range(2):
            @pl.when(p_cur == p)
            def _():
                def body(rb, c):
                    r0 = pl.multiple_of(rb * sub, sub)
                    raw_refs[p][pl.ds(r0, sub), :] = jnp.dot(xs_ref[pl.ds(r0, sub), 0:kh_rows], wgu_bf[...],
                                                             preferred_element_type=f32) + bgu_ref[...]
                    activate(raw_refs[1 - p], rb, jnp.where(n_tile == 0, n_gu, n_tile - 1))
                    return c

                lax.fori_loop(0, ns, body, 0)

    @pl.when((s < n_gu_steps) & (s % 2 == 1))
    def _gate_up_second_half():
        wgu_bf[...] = wgu_ref[...].astype(bf16)
        for p in range(2):
            @pl.when(p_cur == p)
            def _():
                def body(rb, c):
                    r0 = pl.multiple_of(rb * sub, sub)
                    raw_refs[p][pl.ds(r0, sub), :] += jnp.dot(xs_ref[pl.ds(r0, sub), kh_rows:2 * kh_rows], wgu_bf[...],
                                                              preferred_element_type=f32)
                    return c

                lax.fori_loop(0, ns, body, 0)

    @pl.when(s == n_gu_steps)
    def _last_activation():
        def body(rb, c):
            activate(raw_refs[(n_gu - 1) % 2], rb, n_gu - 1)
            return c

        lax.fori_loop(0, ns, body, 0)

    @pl.when(s >= n_gu_steps)
    def _down():
        wdn_bf[...] = wdn_ref[...].astype(bf16)

        def body(rb, c):
            r0 = pl.multiple_of(rb * sub, sub)
            a = jnp.concatenate([act_ref[g, pl.ds(r0, sub), :] for g in range(n_gu)], axis=1)
            y_ref[pl.ds(r0, sub), :] = jnp.dot(a, wdn_bf[...], preferred_element_type=f32) + bdn_ref[...]
            return c

        lax.fori_loop(0, ns, body, 0)

        def clear(rb, c):
            r0 = pl.multiple_of(rb * sub, sub)
            y_ref[pl.ds(r0, sub), :] = jnp.zeros((sub, y_ref.shape[1]), f32)
            return c

        lax.fori_loop(ns, n_sb, clear, 0)


def moe_experts(xs_sorted, tile_e, tile_rows, n_used, w_gu, b_gu, w_dn, b_dn, layer, tm, sub):
    half = xs_sorted.shape[1]
    d = 2 * half
    n_layers, n_exp, _, two_f = w_gu.shape
    ff = two_f // 2
    t_max = xs_sorted.shape[0] // tm
    tgu = min(512, two_f)
    tdn = min(512, d)
    n_gu, n_dn = two_f // tgu, d // tdn

    n_gu_steps = 2 * n_gu

    def gu_idx(t, s, te, tr, nu):
        step = jnp.minimum(s, n_gu_steps - 1)
        return (layer, te[t], step % 2, step // 2)

    def bgu_idx(t, s, te, tr, nu):
        return (layer, te[t], 0, jnp.minimum(s, n_gu_steps - 1) // 2)

    def dn_idx(t, s, te, tr, nu):
        return (layer, te[t], 0, jnp.maximum(s - n_gu_steps, 0))

    def y_idx(t, s, te, tr, nu):
        return (t, jnp.maximum(s - n_gu_steps, 0))

    nbytes = (2 * sub * half * 4 + tm * d * 2 + (n_gu + 1) * tm * tgu + 2 * (d // 2) * tgu * 4 + (d // 2) * tgu * 2
              + 2 * ff * tdn * 4 + ff * tdn * 2 + 2 * tm * tgu * 4 + 2 * tm * tdn * 4)
    grid_spec = pltpu.PrefetchScalarGridSpec(
        num_scalar_prefetch=3,
        grid=(n_used[0], n_gu_steps + n_dn),
        in_specs=[pl.BlockSpec(memory_space=pl.ANY),
                  pl.BlockSpec((None, None, d // 2, tgu), gu_idx),
                  pl.BlockSpec((None, None, 1, tgu), bgu_idx),
                  pl.BlockSpec((None, None, ff, tdn), dn_idx),
                  pl.BlockSpec((None, None, 1, tdn), dn_idx)],
        out_specs=pl.BlockSpec((tm, tdn), y_idx),
        scratch_shapes=[pltpu.VMEM((2, sub, half), jnp.uint32),
                        pltpu.VMEM((tm, d), bf16),
                        pltpu.VMEM((n_gu + 1, tm, tgu // 2), bf16),
                        pltpu.VMEM((d // 2, tgu), bf16),
                        pltpu.VMEM((ff, tdn), bf16),
                        pltpu.VMEM((tm, tgu), f32),
                        pltpu.VMEM((tm, tgu), f32),
                        pltpu.SemaphoreType.DMA((2,))],
    )
    return pl.pallas_call(
        functools.partial(_moe_kernel, tm=tm, sub=sub, n_gu=n_gu, n_dn=n_dn, tgu=tgu),
        grid_spec=grid_spec,
        out_shape=jax.ShapeDtypeStruct((t_max * tm, d), f32),
        compiler_params=_params(("arbitrary", "arbitrary"), nbytes),
        name="moe_experts",
    )(tile_e, tile_rows, n_used, xs_sorted, w_gu,
      b_gu.reshape(n_layers, n_exp, 1, two_f), w_dn, b_dn.reshape(n_layers, n_exp, 1, d))


def _dispatch_kernel(dest_hbm, hp_ref, xs_hbm, dest_smem, dest_sem, row_sem, *, tb, n_steps):
    i = pl.program_id(0)
    slot = i % 2

    def dest_copy(step):
        return pltpu.make_async_copy(dest_hbm.at[step], dest_smem.at[step % 2], dest_sem)

    def row_copy(r, row):
        return pltpu.make_async_copy(hp_ref.at[pl.ds(r, 1), :], xs_hbm.at[pl.ds(row, 1), :], row_sem)

    @pl.when(i == 0)
    def _first():
        dest_copy(0).start()

    dest_copy(i).wait()

    @pl.when(i + 1 < n_steps)
    def _next():
        dest_copy(i + 1).start()

    def issue(r, c):
        for k in range(TOP_K):
            row_copy(r, dest_smem[slot, k * tb + r]).start()
        return c

    lax.fori_loop(0, tb, issue, 0, unroll=2)
    for _ in range(TOP_K):
        pltpu.make_async_copy(hp_ref, xs_hbm.at[pl.ds(0, tb), :], row_sem).wait()


def dispatch(hp, dest_blk, n_rows_out):
    n, half = hp.shape
    n_steps, per_step = dest_blk.shape
    tb = per_step // TOP_K
    return pl.pallas_call(
        functools.partial(_dispatch_kernel, tb=tb, n_steps=n_steps),
        grid=(n_steps,),
        in_specs=[pl.BlockSpec(memory_space=pl.ANY),
                  pl.BlockSpec((tb, half), lambda i: (i, 0))],
        out_specs=pl.BlockSpec(memory_space=pl.ANY),
        out_shape=jax.ShapeDtypeStruct((n_rows_out, half), jnp.uint32),
        scratch_shapes=[pltpu.SMEM((2, per_step), jnp.int32),
                        pltpu.SemaphoreType.DMA(()),
                        pltpu.SemaphoreType.DMA(())],
        compiler_params=_params(("arbitrary",), 2 * tb * half * 4),
        name="moe_dispatch",
    )(dest_blk, hp)


def _combine_kernel(dest_hbm, y_hbm, x_ref, gate_ref, gain_ref, *refs, tb, n_steps, final):
    if final:
        o_ref, dest_smem, ybuf, dest_sem, row_sem = refs
        xo_ref = None
    else:
        xo_ref, o_ref, dest_smem, ybuf, dest_sem, row_sem = refs
    i = pl.program_id(0)
    n_rows = TOP_K * tb

    def dest_copy(step, slot):
        return pltpu.make_async_copy(dest_hbm.at[step], dest_smem.at[slot], dest_sem)

    def row_copy(slot, r):
        return pltpu.make_async_copy(y_hbm.at[pl.ds(dest_smem[slot, r], 1), :], ybuf.at[slot, pl.ds(r, 1), :],
                                     row_sem.at[slot])

    def wait_slot(slot):
        pltpu.make_async_copy(y_hbm.at[pl.ds(0, n_rows), :], ybuf.at[slot], row_sem.at[slot]).wait()

    @pl.when(i == 0)
    def _first():
        dest_copy(0, 0).start()
        dest_copy(0, 0).wait()

        def issue(r, c):
            row_copy(0, r).start()
            return c

        lax.fori_loop(0, n_rows, issue, 0, unroll=8)
        dest_copy(1, 1).start()

    dest_copy(0, 0).wait()
    for p in range(2):
        @pl.when(i % 2 == p)
        def _():
            for r in range(n_rows):
                row_copy(1 - p, r).start()
            wait_slot(p)
            g = gate_ref[...]
            acc = x_ref[...]
            for k in range(TOP_K):
                acc = acc + g[:, k:k + 1] * ybuf[p, k * tb:(k + 1) * tb, :]
            if not final:
                xo_ref[...] = acc
            inv = lax.rsqrt(jnp.mean(acc * acc, axis=-1, keepdims=True) + EPS)
            o_ref[...] = (acc * inv * gain_ref[...]).astype(o_ref.dtype)

            @pl.when(i + 2 <= n_steps)
            def _():
                dest_copy(i + 2, p).start()

            @pl.when(i == n_steps - 1)
            def _():
                wait_slot(1 - p)


def combine(x2d, y, dest_blk, gates_t, gain2d, final):
    n, d = x2d.shape
    n_steps, per_step = dest_blk.shape
    tb = per_step // TOP_K
    dest_blk = jnp.concatenate([dest_blk, jnp.zeros((1, per_step), dest_blk.dtype)], axis=0)
    row_spec = pl.BlockSpec((tb, d), lambda i: (i, 0))
    if final:
        out_shape = jax.ShapeDtypeStruct((n, d), f32)
        out_specs = row_spec
    else:
        out_shape = [jax.ShapeDtypeStruct((n, d), f32), jax.ShapeDtypeStruct((n, d), bf16)]
        out_specs = [row_spec, row_spec]
    nbytes = 2 * TOP_K * tb * d * 4 + 6 * tb * d * 4
    return pl.pallas_call(
        functools.partial(_combine_kernel, tb=tb, n_steps=n_steps, final=final),
        grid=(n_steps,),
        in_specs=[pl.BlockSpec(memory_space=pl.ANY),
                  pl.BlockSpec(memory_space=pl.ANY),
                  row_spec,
                  pl.BlockSpec((tb, TOP_K), lambda i: (i, 0)),
                  pl.BlockSpec((1, d), lambda i: (0, 0))],
        out_specs=out_specs,
        out_shape=out_shape,
        scratch_shapes=[pltpu.SMEM((2, TOP_K * tb), jnp.int32),
                        pltpu.VMEM((2, TOP_K * tb, d), f32),
                        pltpu.SemaphoreType.DMA(()),
                        pltpu.SemaphoreType.DMA((2,))],
        compiler_params=_params(("arbitrary",), nbytes),
        name="moe_combine",
    )(dest_blk, y, x2d, gates_t, gain2d)


def _dispatch_plan(counts, idx, rank, tm, t_max, tb):
    n_exp = counts.shape[0]
    n = idx.shape[1]
    tiles_e = (counts + tm - 1) // tm
    tile_end = jnp.cumsum(tiles_e)
    tile_start = tile_end - tiles_e
    n_used = tile_end[-1]
    expert = jnp.arange(n_exp, dtype=jnp.int32)[:, None, None]
    dest = rank + jnp.sum(jnp.where(idx[None] == expert, (tile_start * tm)[:, None, None], 0), axis=0)
    tile = jnp.arange(t_max, dtype=jnp.int32)
    tile_c = jnp.minimum(tile, n_used - 1)
    tile_e = jnp.minimum(jnp.searchsorted(tile_end, tile_c, side="right"), n_exp - 1).astype(jnp.int32)
    rows = jnp.clip(counts[tile_e] - (tile_c - tile_start[tile_e]) * tm, 0, tm)
    tile_rows = jnp.where(tile < n_used, rows, 0).astype(jnp.int32)
    dest_blk = dest.reshape(TOP_K, n // tb, tb).transpose(1, 0, 2).reshape(n // tb, TOP_K * tb)
    return dest_blk, tile_e, tile_rows, n_used.reshape(1).astype(jnp.int32)


def kernel(x, norm_mix, w_in, attn_rel_bias, attn_out_gain, conv_w, conv_b, lru_w_a, lru_b_a, lru_w_i, lru_b_i,
           lru_lambda, rec_out_gain, w_out, norm_ffn, router_w, router_b, expert_w_gate_up, expert_b_gate_up,
           expert_w_down, expert_b_down, norm_final):
    batch, seq, d = x.shape
    n = batch * seq
    depth = norm_mix.shape[0]
    attn_width = attn_out_gain.shape[1]
    lru_width = rec_out_gain.shape[1]
    n_exp = router_w.shape[-1]
    sub = min(512, n)
    tm = sub * min(3, n // sub)
    t_max = (n * TOP_K) // tm + n_exp
    tb = min(128, n)

    x2d = x.reshape(n, d)
    h = rmsnorm_bf16(x2d, norm_mix, 0)
    out = None
    for l in range(depth):
        qkv = matmul_ws([h], w_in, l, [0], 0, 3 * attn_width, bf16, name="in_proj_qkv")
        rg = matmul_ws([h], w_in, l, [0], 3 * attn_width, 2 * lru_width, f32, name="in_proj_rec")
        attn = attention(qkv, attn_rel_bias[l], attn_out_gain, l, batch, seq, attn_width)
        rec = rglru_block(rg, conv_w, conv_b, lru_w_a, lru_b_a, lru_w_i, lru_b_i, lru_lambda, rec_out_gain,
                          l, batch, seq)
        assert attn_width == lru_width
        x2d = matmul_ws([attn, rec], w_out, l, [0, 1], 0, d, f32, residual=x2d, name="out_proj")
        hp, idx, gates, rank, cnt = router(x2d, norm_ffn, router_w, router_b, l)
        dest_blk, tile_e, tile_rows, n_used = _dispatch_plan(cnt[:, 0], idx, rank, tm, t_max, tb)
        xs_sorted = dispatch(hp, dest_blk, t_max * tm)
        y = moe_experts(xs_sorted, tile_e, tile_rows, n_used, expert_w_gate_up, expert_b_gate_up,
                        expert_w_down, expert_b_down, l, tm, sub)
        final = l == depth - 1
        gain = norm_final.reshape(1, d) if final else norm_mix[l + 1].reshape(1, d)
        res = combine(x2d, y, dest_blk, gates.T, gain, final)
        if final:
            out = res
        else:
            x2d, h = res
    return out.reshape(batch, seq, d)
```

```python
import functools
import math

import jax
import jax.numpy as jnp
from jax import lax
from jax.experimental import pallas as pl
from jax.experimental.pallas import tpu as pltpu

CHUNK = 64
LEFT_CHUNKS = 8
ATTN_HEAD_DIM = 128
REL_CLIP = 128
LRU_C = 8.0
TOP_K = 4
SWIGLU_LIMIT = 7.0
SWIGLU_ALPHA = 1.702
EPS = 1e-6

V7X_LANES = 128
V7X_SUBLANES = 8
V7X_VMEM_BYTES = 64 * 1024 * 1024

MASK_VALUE = -1e30

f32 = jnp.float32
bf16 = jnp.bfloat16


def _vmem_limit(nbytes):
    return int(min(nbytes + 12 * 1024 * 1024, V7X_VMEM_BYTES - 6 * 1024 * 1024))


def _params(sem, nbytes):
    return pltpu.CompilerParams(dimension_semantics=sem, vmem_limit_bytes=_vmem_limit(nbytes))


def _rmsnorm_kernel(x_ref, g_ref, o_ref):
    x = x_ref[...]
    inv = lax.rsqrt(jnp.mean(x * x, axis=-1, keepdims=True) + EPS)
    o_ref[...] = (x * inv * g_ref[...]).astype(o_ref.dtype)


def rmsnorm_bf16(x2d, gain_l, layer):
    n, d = x2d.shape
    tb = min(256, n)
    return pl.pallas_call(
        _rmsnorm_kernel,
        grid=(n // tb,),
        in_specs=[pl.BlockSpec((tb, d), lambda i: (i, 0)),
                  pl.BlockSpec((None, 1, d), lambda i: (layer, 0, 0))],
        out_specs=pl.BlockSpec((tb, d), lambda i: (i, 0)),
        out_shape=jax.ShapeDtypeStruct((n, d), bf16),
        compiler_params=_params(("arbitrary",), 2 * tb * d * 6),
        name="rmsnorm_bf16",
    )(x2d, gain_l.reshape(gain_l.shape[0], 1, d))


def _matmul_kernel(*refs, n_pairs, has_res):
    a_refs = refs[:n_pairs]
    w_refs = refs[n_pairs:2 * n_pairs]
    pos = 2 * n_pairs
    res_ref = refs[pos] if has_res else None
    pos += int(has_res)
    o_ref = refs[pos]
    wbf_refs = refs[pos + 1:pos + 1 + n_pairs]

    @pl.when(pl.program_id(1) == 0)
    def _cast_weights():
        for w_ref, wbf_ref in zip(w_refs, wbf_refs):
            wbf_ref[...] = w_ref[...].astype(bf16)

    acc = jnp.dot(a_refs[0][...], wbf_refs[0][...], preferred_element_type=f32)
    for a_ref, wbf_ref in zip(a_refs[1:], wbf_refs[1:]):
        acc += jnp.dot(a_ref[...], wbf_ref[...], preferred_element_type=f32)
    if has_res:
        acc += res_ref[...]
    o_ref[...] = acc.astype(o_ref.dtype)


def matmul_ws(a_list, w_stack, layer, k_blocks, col0, n_cols, out_dtype, residual=None, name="matmul_ws"):
    m = a_list[0].shape[0]
    tm = min(1024, m)
    tn = min(512, n_cols)
    assert m % tm == 0 and n_cols % tn == 0 and col0 % tn == 0
    cb0 = col0 // tn
    n_pairs = len(a_list)
    in_specs, scratch = [], []
    nbytes = 2 * tm * tn * 4 * (2 if residual is not None else 1)
    for a in a_list:
        kp = a.shape[1]
        in_specs.append(pl.BlockSpec((tm, kp), lambda j, i: (i, 0)))
        nbytes += 2 * tm * kp * 2
    for a, kb in zip(a_list, k_blocks):
        kp = a.shape[1]
        in_specs.append(pl.BlockSpec((None, kp, tn), functools.partial(lambda j, i, kb: (layer, kb, j + cb0), kb=kb)))
        scratch.append(pltpu.VMEM((kp, tn), bf16))
        nbytes += 2 * kp * tn * 4 + kp * tn * 2
    args = list(a_list) + [w_stack] * n_pairs
    if residual is not None:
        in_specs.append(pl.BlockSpec((tm, tn), lambda j, i: (i, j)))
        args.append(residual)
    return pl.pallas_call(
        functools.partial(_matmul_kernel, n_pairs=n_pairs, has_res=residual is not None),
        grid=(n_cols // tn, m // tm),
        in_specs=in_specs,
        out_specs=pl.BlockSpec((tm, tn), lambda j, i: (i, j)),
        out_shape=jax.ShapeDtypeStruct((m, n_cols), out_dtype),
        scratch_shapes=scratch,
        compiler_params=_params(("arbitrary", "arbitrary"), nbytes),
        name=name,
    )(*args)


def _attn_kernel(q_ref, *refs, n_kb, hb, tq, n_hg):
    k_refs = refs[:n_kb]
    v_refs = refs[n_kb:2 * n_kb]
    row0_ref, gain_ref, o_ref, raw_ref, bias_ref = refs[2 * n_kb:]
    i = pl.program_id(1)
    hg = pl.program_id(2)
    dh = ATTN_HEAD_DIM
    scale = 1.0 / math.sqrt(dh)
    n_keys = n_kb * tq

    @pl.when((pl.program_id(0) == 0) & (i == 0) & (hg == 0))
    def _build_bias():
        width = row0_ref.shape[-1]
        qc = ((n_kb - 1) * tq + lax.broadcasted_iota(jnp.int32, (tq, n_keys), 0)) // CHUNK
        kc = lax.broadcasted_iota(jnp.int32, (tq, n_keys), 1) // CHUNK
        valid = (kc <= qc) & (kc >= qc - LEFT_CHUNKS)
        for h in range(bias_ref.shape[0]):
            rows = pltpu.roll(jnp.broadcast_to(row0_ref[h], (tq, width)), 0, 1, stride=1, stride_axis=0)
            bias_ref[h] = jnp.where(valid, rows[:, :n_keys] * (1.0 / scale), MASK_VALUE)

    outs = []
    for hh in range(hb):
        sl = slice(hh * dh, (hh + 1) * dh)
        q = q_ref[:, sl]
        bias = bias_ref[hg * hb + hh]
        parts = []
        for j in range(n_kb - 1, -1, -1):
            s = lax.dot_general(q, k_refs[j][:, sl], (((1,), (1,)), ((), ())), preferred_element_type=f32)
            c0 = (n_kb - 1 - j) * tq
            s = s + bias[:, c0:c0 + tq]
            if j > 0:
                s = s + jnp.where(i >= j, 0.0, MASK_VALUE)
            parts.append(s)
        m = parts[0].max(axis=-1, keepdims=True)
        for s in parts[1:]:
            m = jnp.maximum(m, s.max(axis=-1, keepdims=True))
        l = jnp.zeros_like(m)
        acc = jnp.zeros((tq, dh), f32)
        for idx, s in enumerate(parts):
            j = n_kb - 1 - idx
            p = jnp.exp2((s - m) * (scale * math.log2(math.e)))
            l = l + p.sum(axis=-1, keepdims=True)
            acc = acc + jnp.dot(p.astype(bf16), v_refs[j][:, sl], preferred_element_type=f32)
        outs.append(acc / l)
    raw_ref[hg] = jnp.concatenate(outs, axis=-1)

    @pl.when(hg == n_hg - 1)
    def _finish():
        ss = jnp.zeros((tq, 1), f32)
        for g in range(n_hg):
            r = raw_ref[g]
            ss = ss + jnp.sum(r * r, axis=-1, keepdims=True)
        inv = lax.rsqrt(ss / (n_hg * hb * dh) + EPS)
        for g in range(n_hg):
            w = hb * dh
            o_ref[:, g * w:(g + 1) * w] = (raw_ref[g] * inv * gain_ref[:, g * w:(g + 1) * w]).astype(o_ref.dtype)


def _attn_bias_row0(rel_bias_l, tq, n_kb):
    n_heads = rel_bias_l.shape[0]
    rb = rel_bias_l.astype(f32)
    far = (n_kb - 1) * tq - REL_CLIP
    near = n_kb * tq - far - (2 * REL_CLIP + 1)
    assert far >= 0 and near >= 0
    fill = lambda col, width: jnp.broadcast_to(rb[:, col:col + 1], (n_heads, width))
    row0 = jnp.concatenate([fill(2 * REL_CLIP, far), rb[:, ::-1], fill(0, near), fill(2 * REL_CLIP, tq)], axis=1)
    return row0.reshape(n_heads, 1, (n_kb + 1) * tq)


def attention(qkv, rel_bias_l, gain_l, layer, batch, seq, attn_width):
    n = qkv.shape[0]
    n_heads = attn_width // ATTN_HEAD_DIM
    tq = min(256, seq)
    band = LEFT_CHUNKS * CHUNK
    assert band % tq == 0 and seq % tq == 0 and tq % CHUNK == 0
    n_kb = band // tq + 1
    nq = seq // tq
    hb = min(8, n_heads)
    n_hg = n_heads // hb
    w = hb * ATTN_HEAD_DIM
    row0 = _attn_bias_row0(rel_bias_l, tq, n_kb)

    def kv_spec(j, which):
        return pl.BlockSpec((tq, w), lambda b, i, g: (b * nq + jnp.maximum(i - j, 0), which * n_hg + g))

    in_specs = [pl.BlockSpec((tq, w), lambda b, i, g: (b * nq + i, g))]
    in_specs += [kv_spec(j, 1) for j in range(n_kb)] + [kv_spec(j, 2) for j in range(n_kb)]
    in_specs += [pl.BlockSpec(row0.shape, lambda b, i, g: (0, 0, 0)),
                 pl.BlockSpec((None, 1, attn_width), lambda b, i, g: (layer, 0, 0))]
    nbytes = 2 * (2 * n_kb + 1) * tq * w * 2 + n_heads * tq * n_kb * tq * 4 + 3 * tq * attn_width * 4
    return pl.pallas_call(
        functools.partial(_attn_kernel, n_kb=n_kb, hb=hb, tq=tq, n_hg=n_hg),
        grid=(batch, nq, n_hg),
        in_specs=in_specs,
        out_specs=pl.BlockSpec((tq, attn_width), lambda b, i, g: (b * nq + i, 0)),
        out_shape=jax.ShapeDtypeStruct((n, attn_width), bf16),
        scratch_shapes=[pltpu.VMEM((n_hg, tq, w), f32), pltpu.VMEM((n_heads, tq, n_kb * tq), f32)],
        compiler_params=_params(("arbitrary", "arbitrary", "arbitrary"), nbytes),
        name="band_attention",
    )(*([qkv] * (2 * n_kb + 1)), row0, gain_l.reshape(gain_l.shape[0], 1, attn_width))


def _sigmoid(x):
    return 0.5 * jnp.tanh(0.5 * x) + 0.5


def _lru_kernel(xr_ref, xg_ref, cw_ref, cb_ref, wa_ref, ba_ref, wi_ref, bi_ref, lam_ref, gain_ref,
                o_ref, xe_ref, a_ref, b_ref, h_ref, carry_ref, *, t, n_blocks, conv_width):
    i = pl.program_id(1)
    hdr = V7X_SUBLANES
    bw = wa_ref.shape[-1]

    @pl.when(i == 0)
    def _reset():
        xe_ref[0:hdr, :] = jnp.zeros((hdr, xe_ref.shape[1]), f32)
        carry_ref[...] = jnp.zeros_like(carry_ref)

    x = xr_ref[...]
    xe_ref[hdr:hdr + t, :] = x
    u = cb_ref[...] + cw_ref[conv_width - 1:conv_width, :] * x
    for j in range(conv_width - 1):
        u = u + cw_ref[j:j + 1, :] * xe_ref[pl.ds(hdr - (conv_width - 1) + j, t), :]
    xe_ref[0:hdr, :] = x[t - hdr:t, :]

    lam = lam_ref[...]
    nsp = -LRU_C * (jnp.maximum(-lam, 0.0) + jnp.log1p(jnp.exp(-jnp.abs(lam))))
    for nb in range(n_blocks):
        sl = slice(nb * bw, (nb + 1) * bw)
        ub = u[:, sl]
        ubb = ub.astype(bf16)
        r = _sigmoid(jnp.dot(ubb, wa_ref[nb].astype(bf16), preferred_element_type=f32) + ba_ref[:, sl])
        ig = _sigmoid(jnp.dot(ubb, wi_ref[nb].astype(bf16), preferred_element_type=f32) + bi_ref[:, sl])
        log_a = r * nsp[:, sl]
        a = jnp.exp(log_a)
        a_ref[:, sl] = a
        b_ref[:, sl] = jnp.sqrt(1.0 - a * a) * (ig * ub)

    a = a_ref[...]
    b = b_ref[...]
    row = lax.broadcasted_iota(jnp.int32, a.shape, 0) % hdr
    s = 1
    while s < hdr:
        keep = row >= s
        a_sh = jnp.where(keep, pltpu.roll(a, s, 0), 1.0)
        b_sh = jnp.where(keep, pltpu.roll(b, s, 0), 0.0)
        b = a * b_sh + b
        a = a * a_sh
        s *= 2
    a_ref[...] = a
    b_ref[...] = b

    def tile_step(j, carry):
        r0 = pl.multiple_of(j * hdr, hdr)
        h = a_ref[pl.ds(r0, hdr), :] * carry + b_ref[pl.ds(r0, hdr), :]
        h_ref[pl.ds(r0, hdr), :] = h
        return h[hdr - 1:hdr, :]

    carry_ref[...] = lax.fori_loop(0, t // hdr, tile_step, carry_ref[...])

    y = h_ref[...] * jax.nn.gelu(xg_ref[...], approximate=True)
    inv = lax.rsqrt(jnp.mean(y * y, axis=-1, keepdims=True) + EPS)
    o_ref[...] = (y * inv * gain_ref[...]).astype(o_ref.dtype)


def rglru_block(rg, conv_w, conv_b, w_a, b_a, w_i, b_i, lam, gain, layer, batch, seq):
    n = rg.shape[0]
    n_layers, conv_width, wr = conv_w.shape
    n_blocks = w_a.shape[1]
    bw = w_a.shape[2]
    t = min(256, seq)
    ns = seq // t
    row3 = lambda v: v.reshape(n_layers, 1, wr)
    vec_spec = pl.BlockSpec((None, 1, wr), lambda b, i: (layer, 0, 0))
    wspec = pl.BlockSpec((None, n_blocks, bw, bw), lambda b, i: (layer, 0, 0, 0))
    nbytes = 2 * 2 * t * wr * 4 + 2 * t * wr * 2 + 4 * (t + 8) * wr * 4 + 4 * n_blocks * bw * bw * 4 + 8 * t * wr * 4
    return pl.pallas_call(
        functools.partial(_lru_kernel, t=t, n_blocks=n_blocks, conv_width=conv_width),
        grid=(batch, ns),
        in_specs=[pl.BlockSpec((t, wr), lambda b, i: (b * ns + i, 0)),
                  pl.BlockSpec((t, wr), lambda b, i: (b * ns + i, 1)),
                  pl.BlockSpec((None, conv_width, wr), lambda b, i: (layer, 0, 0)),
                  vec_spec, wspec, vec_spec, wspec, vec_spec, vec_spec, vec_spec],
        out_specs=pl.BlockSpec((t, wr), lambda b, i: (b * ns + i, 0)),
        out_shape=jax.ShapeDtypeStruct((n, wr), bf16),
        scratch_shapes=[pltpu.VMEM((t + V7X_SUBLANES, wr), f32), pltpu.VMEM((t, wr), f32),
                        pltpu.VMEM((t, wr), f32), pltpu.VMEM((t, wr), f32), pltpu.VMEM((1, wr), f32)],
        compiler_params=_params(("arbitrary", "arbitrary"), nbytes),
        name="conv_rglru",
    )(rg, rg, conv_w, row3(conv_b), w_a, row3(b_a), w_i, row3(b_i), row3(lam), row3(gain))


def _router_kernel(x_ref, g_ref, whi_ref, wlo_ref, rb_ref, hp_ref, idx_ref, gate_ref, rank_ref, cnt_ref, carry_ref,
                   *, tb, n_exp):
    step = pl.program_id(0)

    @pl.when(step == 0)
    def _reset():
        carry_ref[...] = jnp.zeros_like(carry_ref)

    x = x_ref[...]
    inv = lax.rsqrt(jnp.mean(x * x, axis=-1, keepdims=True) + EPS)
    h = x * inv * g_ref[...]
    half = h.shape[1] // 2
    h_hi = h.astype(bf16)
    h_hi32 = h_hi.astype(f32)
    bits = lax.bitcast_convert_type(h_hi32, jnp.uint32)
    hp_ref[...] = (bits[:, :half] >> 16) | bits[:, half:]

    h_lo = (h - h_hi32).astype(bf16)
    w_hi = whi_ref[...]
    acc = (jnp.dot(h_hi, w_hi, preferred_element_type=f32) + jnp.dot(h_lo, w_hi, preferred_element_type=f32)
           + jnp.dot(h_hi, wlo_ref[...], preferred_element_type=f32))
    logits = acc.T[:n_exp] + rb_ref[...]
    e_iota = lax.broadcasted_iota(jnp.int32, (n_exp, tb), 0)
    tri = (lax.broadcasted_iota(jnp.int32, (tb, tb), 0) < lax.broadcasted_iota(jnp.int32, (tb, tb), 1)).astype(bf16)
    base = carry_ref[...]
    vals, idxs, ranks = [], [], []
    l = logits
    for _ in range(TOP_K):
        m = l.max(axis=0, keepdims=True)
        idx = jnp.min(jnp.where(l == m, e_iota, n_exp), axis=0, keepdims=True)
        onehot = e_iota == idx
        ohf = onehot.astype(f32)
        before = jnp.dot(ohf.astype(bf16), tri, preferred_element_type=f32)
        ranks.append(jnp.sum(jnp.where(onehot, before + base, 0.0), axis=0, keepdims=True))
        base = base + jnp.sum(ohf, axis=1, keepdims=True)
        vals.append(m)
        idxs.append(idx)
        l = jnp.where(onehot, -jnp.inf, l)
    carry_ref[...] = base
    exps = [jnp.exp(v - vals[0]) for v in vals]
    denom = exps[0]
    for e in exps[1:]:
        denom = denom + e
    idx_ref[...] = jnp.concatenate(idxs, axis=0)
    gate_ref[...] = jnp.concatenate([e / denom for e in exps], axis=0)
    rank_ref[...] = jnp.concatenate(ranks, axis=0).astype(jnp.int32)
    cnt_ref[...] = jnp.broadcast_to(base, cnt_ref.shape).astype(jnp.int32)


def router(x2d, gain_l, router_w_l, router_b_l, layer):
    n, d = x2d.shape
    n_exp = router_w_l.shape[-1]
    tb = min(512, n)
    e_pad = -(-n_exp // V7X_LANES) * V7X_LANES
    rw = jnp.pad(router_w_l[layer].astype(f32), ((0, 0), (0, e_pad - n_exp)))
    w_hi = rw.astype(bf16)
    w_lo = (rw - w_hi.astype(f32)).astype(bf16)
    rb = router_b_l[layer].reshape(n_exp, 1)
    slot_spec = pl.BlockSpec((TOP_K, tb), lambda i: (0, i))
    nbytes = 2 * tb * d * 4 + 2 * tb * d * 2 + 4 * e_pad * d * 2 + 6 * tb * d * 4
    return pl.pallas_call(
        functools.partial(_router_kernel, tb=tb, n_exp=n_exp),
        grid=(n // tb,),
        in_specs=[pl.BlockSpec((tb, d), lambda i: (i, 0)),
                  pl.BlockSpec((None, 1, d), lambda i: (layer, 0, 0)),
                  pl.BlockSpec((d, e_pad), lambda i: (0, 0)),
                  pl.BlockSpec((d, e_pad), lambda i: (0, 0)),
                  pl.BlockSpec((n_exp, 1), lambda i: (0, 0))],
        out_specs=[pl.BlockSpec((tb, d // 2), lambda i: (i, 0)), slot_spec, slot_spec, slot_spec,
                   pl.BlockSpec((n_exp, V7X_LANES), lambda i: (0, 0))],
        out_shape=[jax.ShapeDtypeStruct((n, d // 2), jnp.uint32),
                   jax.ShapeDtypeStruct((TOP_K, n), jnp.int32),
                   jax.ShapeDtypeStruct((TOP_K, n), f32),
                   jax.ShapeDtypeStruct((TOP_K, n), jnp.int32),
                   jax.ShapeDtypeStruct((n_exp, V7X_LANES), jnp.int32)],
        scratch_shapes=[pltpu.VMEM((n_exp, 1), f32)],
        compiler_params=_params(("arbitrary",), nbytes),
        name="router_topk",
    )(x2d, gain_l.reshape(gain_l.shape[0], 1, d), w_hi, w_lo, rb)


def _moe_kernel(tile_e_ref, tile_rows_ref, n_used_ref,
                xs_hbm, wgu_ref, bgu_ref, wdn_ref, bdn_ref,
                y_ref,
                stage_ref, xs_ref, act_ref, wgu_bf, wdn_bf, raw_a, raw_b, row_sem,
                *, tm, sub, n_gu, n_dn, tgu):
    raw_refs = (raw_a, raw_b)
    t = pl.program_id(0)
    s = pl.program_id(1)
    n_used = n_used_ref[0]
    half = stage_ref.shape[-1]
    kh_rows = wgu_bf.shape[0]
    th = tgu // 2
    n_sb = tm // sub
    n_gu_steps = 2 * n_gu
    rows = tile_rows_ref[t]
    ns = (rows + sub - 1) // sub

    def sub_copy(tile, rb):
        return pltpu.make_async_copy(xs_hbm.at[pl.ds(tile * tm + rb * sub, sub), :], stage_ref.at[rb % 2],
                                     row_sem.at[rb % 2])

    def start_tile(tile):
        n_next = (tile_rows_ref[tile] + sub - 1) // sub
        for rb in range(min(2, n_sb)):
            @pl.when(rb < n_next)
            def _():
                sub_copy(tile, rb).start()

    @pl.when(s == 0)
    def _stage_rows():
        @pl.when(t == 0)
        def _first():
            start_tile(0)
            for ref in raw_refs:
                ref[...] = jnp.zeros_like(ref)

        for rb in range(n_sb):
            @pl.when(rb < ns)
            def _unpack():
                sub_copy(t, rb).wait()
                live = lax.broadcasted_iota(jnp.int32, (sub, half), 0) < rows - rb * sub
                w = jnp.where(live, stage_ref[rb % 2], jnp.uint32(0))
                xs_ref[rb * sub:(rb + 1) * sub, 0:half] = lax.bitcast_convert_type(w << 16, f32).astype(bf16)
                xs_ref[rb * sub:(rb + 1) * sub, half:2 * half] = lax.bitcast_convert_type(
                    w & jnp.uint32(0xFFFF0000), f32).astype(bf16)
                if rb + 2 < n_sb:
                    @pl.when(rb + 2 < ns)
                    def _():
                        sub_copy(t, rb + 2).start()

    @pl.when((s == 1) & (t + 1 < n_used))
    def _prefetch_rows():
        start_tile(t + 1)

    def activate(src_ref, rb_src, s_dst):
        sel = (lax.broadcasted_iota(jnp.int32, (tgu, th), 0) == 2 * lax.broadcasted_iota(jnp.int32, (tgu, th), 1)).astype(bf16)
        r_src = pl.multiple_of(rb_src * sub, sub)
        gu = src_ref[pl.ds(r_src, sub), :]
        glu = jnp.minimum(gu, SWIGLU_LIMIT)
        glu = glu * _sigmoid(SWIGLU_ALPHA * glu)
        lin = jnp.clip(gu, -SWIGLU_LIMIT, SWIGLU_LIMIT) + 1.0
        prod = glu * pltpu.roll(lin, tgu - 1, 1)
        act_ref[s_dst, pl.ds(r_src, sub), :] = jnp.dot(prod.astype(bf16), sel, preferred_element_type=f32).astype(bf16)

    n_tile = s // 2
    p_cur = n_tile % 2

    @pl.when((s < n_gu_steps) & (s % 2 == 0))
    def _gate_up_first_half():
        wgu_bf[...] = wgu_ref[...].astype(bf16)

        for p in range(2):
            @pl.when(p_cur == p)
            def _():
                def body(rb, c):
                    r0 = pl.multiple_of(rb * sub, sub)
                    raw_refs[p][pl.ds(r0, sub), :] = jnp.dot(xs_ref[pl.ds(r0, sub), 0:kh_rows], wgu_bf[...],
                                                             preferred_element_type=f32) + bgu_ref[...]
                    activate(raw_refs[1 - p], rb, jnp.where(n_tile == 0, n_gu, n_tile - 1))
                    return c

                lax.fori_loop(0, ns, body, 0)

    @pl.when((s < n_gu_steps) & (s % 2 == 1))
    def _gate_up_second_half():
        wgu_bf[...] = wgu_ref[...].astype(bf16)
        for p in range(2):
            @pl.when(p_cur == p)
            def _():
                def body(rb, c):
                    r0 = pl.multiple_of(rb * sub, sub)
                    raw_refs[p][pl.ds(r0, sub), :] += jnp.dot(xs_ref[pl.ds(r0, sub), kh_rows:2 * kh_rows], wgu_bf[...],
                                                              preferred_element_type=f32)
                    return c

                lax.fori_loop(0, ns, body, 0)

    @pl.when(s == n_gu_steps)
    def _last_activation():
        def body(rb, c):
            activate(raw_refs[(n_gu - 1) % 2], rb, n_gu - 1)
            return c

        lax.fori_loop(0, ns, body, 0)

    @pl.when(s >= n_gu_steps)
    def _down():
        wdn_bf[...] = wdn_ref[...].astype(bf16)

        def body(rb, c):
            r0 = pl.multiple_of(rb * sub, sub)
            a = jnp.concatenate([act_ref[g, pl.ds(r0, sub), :] for g in range(n_gu)], axis=1)
            y = jnp.dot(a, wdn_bf[...], preferred_element_type=f32) + bdn_ref[...]
            hw = y.shape[1] // 2
            bits = lax.bitcast_convert_type(y.astype(bf16).astype(f32), jnp.uint32)
            y_ref[pl.ds(r0, sub), :] = (bits[:, :hw] >> 16) | bits[:, hw:]
            return c

        lax.fori_loop(0, ns, body, 0)

        def clear(rb, c):
            r0 = pl.multiple_of(rb * sub, sub)
            y_ref[pl.ds(r0, sub), :] = jnp.zeros((sub, y_ref.shape[1]), jnp.uint32)
            return c

        lax.fori_loop(ns, n_sb, clear, 0)


def _down_tile(d):
    return min(512, d)


def moe_experts(xs_sorted, tile_e, tile_rows, n_used, w_gu, b_gu, w_dn, b_dn, layer, tm, sub):
    half = xs_sorted.shape[1]
    d = 2 * half
    n_layers, n_exp, _, two_f = w_gu.shape
    ff = two_f // 2
    t_max = xs_sorted.shape[0] // tm
    tgu = min(512, two_f)
    tdn = _down_tile(d)
    n_gu, n_dn = two_f // tgu, d // tdn

    n_gu_steps = 2 * n_gu

    def gu_idx(t, s, te, tr, nu):
        step = jnp.minimum(s, n_gu_steps - 1)
        return (layer, te[t], step % 2, step // 2)

    def bgu_idx(t, s, te, tr, nu):
        return (layer, te[t], 0, jnp.minimum(s, n_gu_steps - 1) // 2)

    def dn_idx(t, s, te, tr, nu):
        return (layer, te[t], 0, jnp.maximum(s - n_gu_steps, 0))

    def y_idx(t, s, te, tr, nu):
        return (t, jnp.maximum(s - n_gu_steps, 0))

    nbytes = (2 * sub * half * 4 + tm * d * 2 + (n_gu + 1) * tm * tgu + 2 * (d // 2) * tgu * 4 + (d // 2) * tgu * 2
              + 2 * ff * tdn * 4 + ff * tdn * 2 + 2 * tm * tgu * 4 + 2 * tm * tdn * 4)
    grid_spec = pltpu.PrefetchScalarGridSpec(
        num_scalar_prefetch=3,
        grid=(n_used[0], n_gu_steps + n_dn),
        in_specs=[pl.BlockSpec(memory_space=pl.ANY),
                  pl.BlockSpec((None, None, d // 2, tgu), gu_idx),
                  pl.BlockSpec((None, None, 1, tgu), bgu_idx),
                  pl.BlockSpec((None, None, ff, tdn), dn_idx),
                  pl.BlockSpec((None, None, 1, tdn), dn_idx)],
        out_specs=pl.BlockSpec((tm, tdn // 2), y_idx),
        scratch_shapes=[pltpu.VMEM((2, sub, half), jnp.uint32),
                        pltpu.VMEM((tm, d), bf16),
                        pltpu.VMEM((n_gu + 1, tm, tgu // 2), bf16),
                        pltpu.VMEM((d // 2, tgu), bf16),
                        pltpu.VMEM((ff, tdn), bf16),
                        pltpu.VMEM((tm, tgu), f32),
                        pltpu.VMEM((tm, tgu), f32),
                        pltpu.SemaphoreType.DMA((2,))],
    )
    return pl.pallas_call(
        functools.partial(_moe_kernel, tm=tm, sub=sub, n_gu=n_gu, n_dn=n_dn, tgu=tgu),
        grid_spec=grid_spec,
        out_shape=jax.ShapeDtypeStruct((t_max * tm, d // 2), jnp.uint32),
        compiler_params=_params(("arbitrary", "arbitrary"), nbytes),
        name="moe_experts",
    )(tile_e, tile_rows, n_used, xs_sorted, w_gu,
      b_gu.reshape(n_layers, n_exp, 1, two_f), w_dn, b_dn.reshape(n_layers, n_exp, 1, d))


def _dispatch_kernel(dest_hbm, hp_ref, xs_hbm, dest_smem, dest_sem, row_sem, *, tb, n_steps):
    i = pl.program_id(0)
    slot = i % 2

    def dest_copy(step):
        return pltpu.make_async_copy(dest_hbm.at[step], dest_smem.at[step % 2], dest_sem)

    def row_copy(r, row):
        return pltpu.make_async_copy(hp_ref.at[pl.ds(r, 1), :], xs_hbm.at[pl.ds(row, 1), :], row_sem)

    @pl.when(i == 0)
    def _first():
        dest_copy(0).start()

    dest_copy(i).wait()

    @pl.when(i + 1 < n_steps)
    def _next():
        dest_copy(i + 1).start()

    def issue(r, c):
        for k in range(TOP_K):
            row_copy(r, dest_smem[slot, k * tb + r]).start()
        return c

    lax.fori_loop(0, tb, issue, 0, unroll=2)
    for _ in range(TOP_K):
        pltpu.make_async_copy(hp_ref, xs_hbm.at[pl.ds(0, tb), :], row_sem).wait()


def dispatch(hp, dest_blk, n_rows_out):
    n, half = hp.shape
    n_steps, per_step = dest_blk.shape
    tb = per_step // TOP_K
    return pl.pallas_call(
        functools.partial(_dispatch_kernel, tb=tb, n_steps=n_steps),
        grid=(n_steps,),
        in_specs=[pl.BlockSpec(memory_space=pl.ANY),
                  pl.BlockSpec((tb, half), lambda i: (i, 0))],
        out_specs=pl.BlockSpec(memory_space=pl.ANY),
        out_shape=jax.ShapeDtypeStruct((n_rows_out, half), jnp.uint32),
        scratch_shapes=[pltpu.SMEM((2, per_step), jnp.int32),
                        pltpu.SemaphoreType.DMA(()),
                        pltpu.SemaphoreType.DMA(())],
        compiler_params=_params(("arbitrary",), 2 * tb * half * 4),
        name="moe_dispatch",
    )(dest_blk, hp)


def _combine_kernel(dest_hbm, y_hbm, x_ref, gate_ref, gain_ref, *refs, tb, n_steps, final, pack):
    if final:
        o_ref, dest_smem, ybuf, dest_sem, row_sem = refs
        xo_ref = None
    else:
        xo_ref, o_ref, dest_smem, ybuf, dest_sem, row_sem = refs
    i = pl.program_id(0)
    n_rows = TOP_K * tb

    def dest_copy(step, slot):
        return pltpu.make_async_copy(dest_hbm.at[step], dest_smem.at[slot], dest_sem)

    def row_copy(slot, r):
        return pltpu.make_async_copy(y_hbm.at[pl.ds(dest_smem[slot, r], 1), :], ybuf.at[slot, pl.ds(r, 1), :],
                                     row_sem.at[slot])

    def wait_slot(slot):
        pltpu.make_async_copy(y_hbm.at[pl.ds(0, n_rows), :], ybuf.at[slot], row_sem.at[slot]).wait()

    @pl.when(i == 0)
    def _first():
        dest_copy(0, 0).start()
        dest_copy(0, 0).wait()

        def issue(r, c):
            row_copy(0, r).start()
            return c

        lax.fori_loop(0, n_rows, issue, 0, unroll=8)
        dest_copy(1, 1).start()

    dest_copy(0, 0).wait()
    for p in range(2):
        @pl.when(i % 2 == p)
        def _():
            for r in range(n_rows):
                row_copy(1 - p, r).start()
            wait_slot(p)
            g = gate_ref[...]
            acc = x_ref[...]
            for k in range(TOP_K):
                w = ybuf[p, k * tb:(k + 1) * tb, :]
                lo = lax.bitcast_convert_type(w << 16, f32)
                hi = lax.bitcast_convert_type(w & jnp.uint32(0xFFFF0000), f32)
                pieces = []
                for c0 in range(0, w.shape[1], pack):
                    pieces += [lo[:, c0:c0 + pack], hi[:, c0:c0 + pack]]
                acc = acc + g[:, k:k + 1] * jnp.concatenate(pieces, axis=1)
            if not final:
                xo_ref[...] = acc
            inv = lax.rsqrt(jnp.mean(acc * acc, axis=-1, keepdims=True) + EPS)
            o_ref[...] = (acc * inv * gain_ref[...]).astype(o_ref.dtype)

            @pl.when(i + 2 <= n_steps)
            def _():
                dest_copy(i + 2, p).start()

            @pl.when(i == n_steps - 1)
            def _():
                wait_slot(1 - p)


def combine(x2d, y, dest_blk, gates_t, gain2d, final):
    n, d = x2d.shape
    n_steps, per_step = dest_blk.shape
    tb = per_step // TOP_K
    pack = _down_tile(d) // 2
    dest_blk = jnp.concatenate([dest_blk, jnp.zeros((1, per_step), dest_blk.dtype)], axis=0)
    row_spec = pl.BlockSpec((tb, d), lambda i: (i, 0))
    if final:
        out_shape = jax.ShapeDtypeStruct((n, d), f32)
        out_specs = row_spec
    else:
        out_shape = [jax.ShapeDtypeStruct((n, d), f32), jax.ShapeDtypeStruct((n, d), bf16)]
        out_specs = [row_spec, row_spec]
    nbytes = 2 * TOP_K * tb * d * 2 + 8 * tb * d * 4
    return pl.pallas_call(
        functools.partial(_combine_kernel, tb=tb, n_steps=n_steps, final=final, pack=pack),
        grid=(n_steps,),
        in_specs=[pl.BlockSpec(memory_space=pl.ANY),
                  pl.BlockSpec(memory_space=pl.ANY),
                  row_spec,
                  pl.BlockSpec((tb, TOP_K), lambda i: (i, 0)),
                  pl.BlockSpec((1, d), lambda i: (0, 0))],
        out_specs=out_specs,
        out_shape=out_shape,
        scratch_shapes=[pltpu.SMEM((2, TOP_K * tb), jnp.int32),
                        pltpu.VMEM((2, TOP_K * tb, d // 2), jnp.uint32),
                        pltpu.SemaphoreType.DMA(()),
                        pltpu.SemaphoreType.DMA((2,))],
        compiler_params=_params(("arbitrary",), nbytes),
        name="moe_combine",
    )(dest_blk, y, x2d, gates_t, gain2d)


def _dispatch_plan(counts, idx, rank, tm, t_max, tb):
    n_exp = counts.shape[0]
    n = idx.shape[1]
    tiles_e = (counts + tm - 1) // tm
    tile_end = jnp.cumsum(tiles_e)
    tile_start = tile_end - tiles_e
    n_used = tile_end[-1]
    expert = jnp.arange(n_exp, dtype=jnp.int32)[:, None, None]
    dest = rank + jnp.sum(jnp.where(idx[None] == expert, (tile_start * tm)[:, None, None], 0), axis=0)
    tile = jnp.arange(t_max, dtype=jnp.int32)
    tile_c = jnp.minimum(tile, n_used - 1)
    tile_e = jnp.minimum(jnp.searchsorted(tile_end, tile_c, side="right"), n_exp - 1).astype(jnp.int32)
    rows = jnp.clip(counts[tile_e] - (tile_c - tile_start[tile_e]) * tm, 0, tm)
    tile_rows = jnp.where(tile < n_used, rows, 0).astype(jnp.int32)
    dest_blk = dest.reshape(TOP_K, n // tb, tb).transpose(1, 0, 2).reshape(n // tb, TOP_K * tb)
    return dest_blk, tile_e, tile_rows, n_used.reshape(1).astype(jnp.int32)


def kernel(x, norm_mix, w_in, attn_rel_bias, attn_out_gain, conv_w, conv_b, lru_w_a, lru_b_a, lru_w_i, lru_b_i,
           lru_lambda, rec_out_gain, w_out, norm_ffn, router_w, router_b, expert_w_gate_up, expert_b_gate_up,
           expert_w_down, expert_b_down, norm_final):
    batch, seq, d = x.shape
    n = batch * seq
    depth = norm_mix.shape[0]
    attn_width = attn_out_gain.shape[1]
    lru_width = rec_out_gain.shape[1]
    n_exp = router_w.shape[-1]
    sub = min(512, n)
    tm = sub * min(3, n // sub)
    t_max = (n * TOP_K) // tm + n_exp
    tb = min(128, n)

    x2d = x.reshape(n, d)
    h = rmsnorm_bf16(x2d, norm_mix, 0)
    out = None
    for l in range(depth):
        qkv = matmul_ws([h], w_in, l, [0], 0, 3 * attn_width, bf16, name="in_proj_qkv")
        rg = matmul_ws([h], w_in, l, [0], 3 * attn_width, 2 * lru_width, f32, name="in_proj_rec")
        attn = attention(qkv, attn_rel_bias[l], attn_out_gain, l, batch, seq, attn_width)
        rec = rglru_block(rg, conv_w, conv_b, lru_w_a, lru_b_a, lru_w_i, lru_b_i, lru_lambda, rec_out_gain,
                          l, batch, seq)
        assert attn_width == lru_width
        x2d = matmul_ws([attn, rec], w_out, l, [0, 1], 0, d, f32, residual=x2d, name="out_proj")
        hp, idx, gates, rank, cnt = router(x2d, norm_ffn, router_w, router_b, l)
        dest_blk, tile_e, tile_rows, n_used = _dispatch_plan(cnt[:, 0], idx, rank, tm, t_max, tb)
        xs_sorted = dispatch(hp, dest_blk, t_max * tm)
        y = moe_experts(xs_sorted, tile_e, tile_rows, n_used, expert_w_gate_up, expert_b_gate_up,
                        expert_w_down, expert_b_down, l, tm, sub)
        final = l == depth - 1
        gain = norm_final.reshape(1, d) if final else norm_mix[l + 1].reshape(1, d)
        res = combine(x2d, y, dest_blk, gates.T, gain, final)
        if final:
            out = res
        else:
            x2d, h = res
    return out.reshape(batch, seq, d)
```

```python
import functools
import math

import jax
import jax.numpy as jnp
from jax import lax
from jax.experimental import pallas as pl
from jax.experimental.pallas import tpu as pltpu

CHUNK = 64
LEFT_CHUNKS = 8
ATTN_HEAD_DIM = 128
REL_CLIP = 128
LRU_C = 8.0
TOP_K = 4
SWIGLU_LIMIT = 7.0
SWIGLU_ALPHA = 1.702
EPS = 1e-6

V7X_LANES = 128
V7X_SUBLANES = 8
V7X_VMEM_BYTES = 64 * 1024 * 1024

MASK_VALUE = -1e30

f32 = jnp.float32
bf16 = jnp.bfloat16


def _vmem_limit(nbytes):
    return int(min(nbytes + 12 * 1024 * 1024, V7X_VMEM_BYTES - 6 * 1024 * 1024))


def _params(sem, nbytes):
    return pltpu.CompilerParams(dimension_semantics=sem, vmem_limit_bytes=_vmem_limit(nbytes))


def _rmsnorm_kernel(x_ref, g_ref, o_ref):
    x = x_ref[...]
    inv = lax.rsqrt(jnp.mean(x * x, axis=-1, keepdims=True) + EPS)
    o_ref[...] = (x * inv * g_ref[...]).astype(o_ref.dtype)


def rmsnorm_bf16(x2d, gain_l, layer):
    n, d = x2d.shape
    tb = min(256, n)
    return pl.pallas_call(
        _rmsnorm_kernel,
        grid=(n // tb,),
        in_specs=[pl.BlockSpec((tb, d), lambda i: (i, 0)),
                  pl.BlockSpec((None, 1, d), lambda i: (layer, 0, 0))],
        out_specs=pl.BlockSpec((tb, d), lambda i: (i, 0)),
        out_shape=jax.ShapeDtypeStruct((n, d), bf16),
        compiler_params=_params(("arbitrary",), 2 * tb * d * 6),
        name="rmsnorm_bf16",
    )(x2d, gain_l.reshape(gain_l.shape[0], 1, d))


def _matmul_kernel(*refs, n_pairs, has_res):
    a_refs = refs[:n_pairs]
    w_refs = refs[n_pairs:2 * n_pairs]
    pos = 2 * n_pairs
    res_ref = refs[pos] if has_res else None
    pos += int(has_res)
    o_ref = refs[pos]
    wbf_refs = refs[pos + 1:pos + 1 + n_pairs]

    @pl.when(pl.program_id(1) == 0)
    def _cast_weights():
        for w_ref, wbf_ref in zip(w_refs, wbf_refs):
            wbf_ref[...] = w_ref[...].astype(bf16)

    acc = jnp.dot(a_refs[0][...], wbf_refs[0][...], preferred_element_type=f32)
    for a_ref, wbf_ref in zip(a_refs[1:], wbf_refs[1:]):
        acc += jnp.dot(a_ref[...], wbf_ref[...], preferred_element_type=f32)
    if has_res:
        acc += res_ref[...]
    o_ref[...] = acc.astype(o_ref.dtype)


def matmul_ws(a_list, w_stack, layer, k_blocks, col0, n_cols, out_dtype, residual=None, name="matmul_ws"):
    m = a_list[0].shape[0]
    tm = min(1024, m)
    tn = min(512, n_cols)
    assert m % tm == 0 and n_cols % tn == 0 and col0 % tn == 0
    cb0 = col0 // tn
    n_pairs = len(a_list)
    in_specs, scratch = [], []
    nbytes = 2 * tm * tn * 4 * (2 if residual is not None else 1)
    for a in a_list:
        kp = a.shape[1]
        in_specs.append(pl.BlockSpec((tm, kp), lambda j, i: (i, 0)))
        nbytes += 2 * tm * kp * 2
    for a, kb in zip(a_list, k_blocks):
        kp = a.shape[1]
        in_specs.append(pl.BlockSpec((None, kp, tn), functools.partial(lambda j, i, kb: (layer, kb, j + cb0), kb=kb)))
        scratch.append(pltpu.VMEM((kp, tn), bf16))
        nbytes += 2 * kp * tn * 4 + kp * tn * 2
    args = list(a_list) + [w_stack] * n_pairs
    if residual is not None:
        in_specs.append(pl.BlockSpec((tm, tn), lambda j, i: (i, j)))
        args.append(residual)
    return pl.pallas_call(
        functools.partial(_matmul_kernel, n_pairs=n_pairs, has_res=residual is not None),
        grid=(n_cols // tn, m // tm),
        in_specs=in_specs,
        out_specs=pl.BlockSpec((tm, tn), lambda j, i: (i, j)),
        out_shape=jax.ShapeDtypeStruct((m, n_cols), out_dtype),
        scratch_shapes=scratch,
        compiler_params=_params(("arbitrary", "arbitrary"), nbytes),
        name=name,
    )(*args)


def _attn_kernel(q_ref, *refs, n_kb, hb, tq, n_hg):
    k_refs = refs[:n_kb]
    v_refs = refs[n_kb:2 * n_kb]
    row0_ref, gain_ref, o_ref, raw_ref, bias_ref = refs[2 * n_kb:]
    i = pl.program_id(1)
    hg = pl.program_id(2)
    dh = ATTN_HEAD_DIM
    scale = 1.0 / math.sqrt(dh)
    n_keys = n_kb * tq

    @pl.when((pl.program_id(0) == 0) & (i == 0) & (hg == 0))
    def _build_bias():
        width = row0_ref.shape[-1]
        qc = ((n_kb - 1) * tq + lax.broadcasted_iota(jnp.int32, (tq, n_keys), 0)) // CHUNK
        kc = lax.broadcasted_iota(jnp.int32, (tq, n_keys), 1) // CHUNK
        valid = (kc <= qc) & (kc >= qc - LEFT_CHUNKS)
        for h in range(bias_ref.shape[0]):
            rows = pltpu.roll(jnp.broadcast_to(row0_ref[h], (tq, width)), 0, 1, stride=1, stride_axis=0)
            bias_ref[h] = jnp.where(valid, rows[:, :n_keys] * (1.0 / scale), MASK_VALUE)

    outs = []
    for hh in range(hb):
        sl = slice(hh * dh, (hh + 1) * dh)
        q = q_ref[:, sl]
        bias = bias_ref[hg * hb + hh]
        parts = []
        for j in range(n_kb - 1, -1, -1):
            s = lax.dot_general(q, k_refs[j][:, sl], (((1,), (1,)), ((), ())), preferred_element_type=f32)
            c0 = (n_kb - 1 - j) * tq
            s = s + bias[:, c0:c0 + tq]
            if j > 0:
                s = s + jnp.where(i >= j, 0.0, MASK_VALUE)
            parts.append(s)
        m = parts[0].max(axis=-1, keepdims=True)
        for s in parts[1:]:
            m = jnp.maximum(m, s.max(axis=-1, keepdims=True))
        l = jnp.zeros_like(m)
        acc = jnp.zeros((tq, dh), f32)
        for idx, s in enumerate(parts):
            j = n_kb - 1 - idx
            p = jnp.exp2((s - m) * (scale * math.log2(math.e)))
            l = l + p.sum(axis=-1, keepdims=True)
            acc = acc + jnp.dot(p.astype(bf16), v_refs[j][:, sl], preferred_element_type=f32)
        outs.append(acc / l)
    raw_ref[hg] = jnp.concatenate(outs, axis=-1)

    @pl.when(hg == n_hg - 1)
    def _finish():
        ss = jnp.zeros((tq, 1), f32)
        for g in range(n_hg):
            r = raw_ref[g]
            ss = ss + jnp.sum(r * r, axis=-1, keepdims=True)
        inv = lax.rsqrt(ss / (n_hg * hb * dh) + EPS)
        for g in range(n_hg):
            w = hb * dh
            o_ref[:, g * w:(g + 1) * w] = (raw_ref[g] * inv * gain_ref[:, g * w:(g + 1) * w]).astype(o_ref.dtype)


def _attn_bias_row0(rel_bias_l, tq, n_kb):
    n_heads = rel_bias_l.shape[0]
    rb = rel_bias_l.astype(f32)
    far = (n_kb - 1) * tq - REL_CLIP
    near = n_kb * tq - far - (2 * REL_CLIP + 1)
    assert far >= 0 and near >= 0
    fill = lambda col, width: jnp.broadcast_to(rb[:, col:col + 1], (n_heads, width))
    row0 = jnp.concatenate([fill(2 * REL_CLIP, far), rb[:, ::-1], fill(0, near), fill(2 * REL_CLIP, tq)], axis=1)
    return row0.reshape(n_heads, 1, (n_kb + 1) * tq)


def attention(qkv, rel_bias_l, gain_l, layer, batch, seq, attn_width):
    n = qkv.shape[0]
    n_heads = attn_width // ATTN_HEAD_DIM
    tq = min(256, seq)
    band = LEFT_CHUNKS * CHUNK
    assert band % tq == 0 and seq % tq == 0 and tq % CHUNK == 0
    n_kb = band // tq + 1
    nq = seq // tq
    hb = min(8, n_heads)
    n_hg = n_heads // hb
    w = hb * ATTN_HEAD_DIM
    row0 = _attn_bias_row0(rel_bias_l, tq, n_kb)

    def kv_spec(j, which):
        return pl.BlockSpec((tq, w), lambda b, i, g: (b * nq + jnp.maximum(i - j, 0), which * n_hg + g))

    in_specs = [pl.BlockSpec((tq, w), lambda b, i, g: (b * nq + i, g))]
    in_specs += [kv_spec(j, 1) for j in range(n_kb)] + [kv_spec(j, 2) for j in range(n_kb)]
    in_specs += [pl.BlockSpec(row0.shape, lambda b, i, g: (0, 0, 0)),
                 pl.BlockSpec((None, 1, attn_width), lambda b, i, g: (layer, 0, 0))]
    nbytes = 2 * (2 * n_kb + 1) * tq * w * 2 + n_heads * tq * n_kb * tq * 4 + 3 * tq * attn_width * 4
    return pl.pallas_call(
        functools.partial(_attn_kernel, n_kb=n_kb, hb=hb, tq=tq, n_hg=n_hg),
        grid=(batch, nq, n_hg),
        in_specs=in_specs,
        out_specs=pl.BlockSpec((tq, attn_width), lambda b, i, g: (b * nq + i, 0)),
        out_shape=jax.ShapeDtypeStruct((n, attn_width), bf16),
        scratch_shapes=[pltpu.VMEM((n_hg, tq, w), f32), pltpu.VMEM((n_heads, tq, n_kb * tq), f32)],
        compiler_params=_params(("arbitrary", "arbitrary", "arbitrary"), nbytes),
        name="band_attention",
    )(*([qkv] * (2 * n_kb + 1)), row0, gain_l.reshape(gain_l.shape[0], 1, attn_width))


def _sigmoid(x):
    return 0.5 * jnp.tanh(0.5 * x) + 0.5


def _lru_kernel(xr_ref, xg_ref, cw_ref, cb_ref, wa_ref, ba_ref, wi_ref, bi_ref, lam_ref, gain_ref,
                o_ref, xe_ref, a_ref, b_ref, h_ref, carry_ref, *, t, n_blocks, conv_width):
    i = pl.program_id(1)
    hdr = V7X_SUBLANES
    bw = wa_ref.shape[-1]

    @pl.when(i == 0)
    def _reset():
        xe_ref[0:hdr, :] = jnp.zeros((hdr, xe_ref.shape[1]), f32)
        carry_ref[...] = jnp.zeros_like(carry_ref)

    x = xr_ref[...]
    xe_ref[hdr:hdr + t, :] = x
    u = cb_ref[...] + cw_ref[conv_width - 1:conv_width, :] * x
    for j in range(conv_width - 1):
        u = u + cw_ref[j:j + 1, :] * xe_ref[pl.ds(hdr - (conv_width - 1) + j, t), :]
    xe_ref[0:hdr, :] = x[t - hdr:t, :]

    lam = lam_ref[...]
    nsp = -LRU_C * (jnp.maximum(-lam, 0.0) + jnp.log1p(jnp.exp(-jnp.abs(lam))))
    for nb in range(n_blocks):
        sl = slice(nb * bw, (nb + 1) * bw)
        ub = u[:, sl]
        ubb = ub.astype(bf16)
        r = _sigmoid(jnp.dot(ubb, wa_ref[nb].astype(bf16), preferred_element_type=f32) + ba_ref[:, sl])
        ig = _sigmoid(jnp.dot(ubb, wi_ref[nb].astype(bf16), preferred_element_type=f32) + bi_ref[:, sl])
        log_a = r * nsp[:, sl]
        a = jnp.exp(log_a)
        a_ref[:, sl] = a
        b_ref[:, sl] = jnp.sqrt(1.0 - a * a) * (ig * ub)

    a = a_ref[...]
    b = b_ref[...]
    row = lax.broadcasted_iota(jnp.int32, a.shape, 0) % hdr
    s = 1
    while s < hdr:
        keep = row >= s
        a_sh = jnp.where(keep, pltpu.roll(a, s, 0), 1.0)
        b_sh = jnp.where(keep, pltpu.roll(b, s, 0), 0.0)
        b = a * b_sh + b
        a = a * a_sh
        s *= 2
    a_ref[...] = a
    b_ref[...] = b

    def tile_step(j, carry):
        r0 = pl.multiple_of(j * hdr, hdr)
        h = a_ref[pl.ds(r0, hdr), :] * carry + b_ref[pl.ds(r0, hdr), :]
        h_ref[pl.ds(r0, hdr), :] = h
        return h[hdr - 1:hdr, :]

    carry_ref[...] = lax.fori_loop(0, t // hdr, tile_step, carry_ref[...])

    y = h_ref[...] * jax.nn.gelu(xg_ref[...], approximate=True)
    inv = lax.rsqrt(jnp.mean(y * y, axis=-1, keepdims=True) + EPS)
    o_ref[...] = (y * inv * gain_ref[...]).astype(o_ref.dtype)


def rglru_block(rg, conv_w, conv_b, w_a, b_a, w_i, b_i, lam, gain, layer, batch, seq):
    n = rg.shape[0]
    n_layers, conv_width, wr = conv_w.shape
    n_blocks = w_a.shape[1]
    bw = w_a.shape[2]
    t = min(256, seq)
    ns = seq // t
    row3 = lambda v: v.reshape(n_layers, 1, wr)
    vec_spec = pl.BlockSpec((None, 1, wr), lambda b, i: (layer, 0, 0))
    wspec = pl.BlockSpec((None, n_blocks, bw, bw), lambda b, i: (layer, 0, 0, 0))
    nbytes = 2 * 2 * t * wr * 4 + 2 * t * wr * 2 + 4 * (t + 8) * wr * 4 + 4 * n_blocks * bw * bw * 4 + 8 * t * wr * 4
    return pl.pallas_call(
        functools.partial(_lru_kernel, t=t, n_blocks=n_blocks, conv_width=conv_width),
        grid=(batch, ns),
        in_specs=[pl.BlockSpec((t, wr), lambda b, i: (b * ns + i, 0)),
                  pl.BlockSpec((t, wr), lambda b, i: (b * ns + i, 1)),
                  pl.BlockSpec((None, conv_width, wr), lambda b, i: (layer, 0, 0)),
                  vec_spec, wspec, vec_spec, wspec, vec_spec, vec_spec, vec_spec],
        out_specs=pl.BlockSpec((t, wr), lambda b, i: (b * ns + i, 0)),
        out_shape=jax.ShapeDtypeStruct((n, wr), bf16),
        scratch_shapes=[pltpu.VMEM((t + V7X_SUBLANES, wr), f32), pltpu.VMEM((t, wr), f32),
                        pltpu.VMEM((t, wr), f32), pltpu.VMEM((t, wr), f32), pltpu.VMEM((1, wr), f32)],
        compiler_params=_params(("arbitrary", "arbitrary"), nbytes),
        name="conv_rglru",
    )(rg, rg, conv_w, row3(conv_b), w_a, row3(b_a), w_i, row3(b_i), row3(lam), row3(gain))


def _router_kernel(x_ref, g_ref, whi_ref, wlo_ref, rb_ref, hp_ref, idx_ref, gate_ref, rank_ref, cnt_ref, carry_ref,
                   *, tb, n_exp):
    step = pl.program_id(0)

    @pl.when(step == 0)
    def _reset():
        carry_ref[...] = jnp.zeros_like(carry_ref)

    x = x_ref[...]
    inv = lax.rsqrt(jnp.mean(x * x, axis=-1, keepdims=True) + EPS)
    h = x * inv * g_ref[...]
    half = h.shape[1] // 2
    h_hi = h.astype(bf16)
    h_hi32 = h_hi.astype(f32)
    bits = lax.bitcast_convert_type(h_hi32, jnp.uint32)
    hp_ref[...] = (bits[:, :half] >> 16) | bits[:, half:]

    h_lo = (h - h_hi32).astype(bf16)
    w_hi = whi_ref[...]
    acc = (jnp.dot(h_hi, w_hi, preferred_element_type=f32) + jnp.dot(h_lo, w_hi, preferred_element_type=f32)
           + jnp.dot(h_hi, wlo_ref[...], preferred_element_type=f32))
    logits = acc.T[:n_exp] + rb_ref[...]
    e_iota = lax.broadcasted_iota(jnp.int32, (n_exp, tb), 0)
    tri = (lax.broadcasted_iota(jnp.int32, (tb, tb), 0) < lax.broadcasted_iota(jnp.int32, (tb, tb), 1)).astype(bf16)
    base = carry_ref[...]
    vals, idxs, ranks = [], [], []
    l = logits
    for _ in range(TOP_K):
        m = l.max(axis=0, keepdims=True)
        idx = jnp.min(jnp.where(l == m, e_iota, n_exp), axis=0, keepdims=True)
        onehot = e_iota == idx
        ohf = onehot.astype(f32)
        before = jnp.dot(ohf.astype(bf16), tri, preferred_element_type=f32)
        ranks.append(jnp.sum(jnp.where(onehot, before + base, 0.0), axis=0, keepdims=True))
        base = base + jnp.sum(ohf, axis=1, keepdims=True)
        vals.append(m)
        idxs.append(idx)
        l = jnp.where(onehot, -jnp.inf, l)
    carry_ref[...] = base
    exps = [jnp.exp(v - vals[0]) for v in vals]
    denom = exps[0]
    for e in exps[1:]:
        denom = denom + e
    idx_ref[...] = jnp.concatenate(idxs, axis=0)
    gate_ref[...] = jnp.concatenate([e / denom for e in exps], axis=0)
    rank_ref[...] = jnp.concatenate(ranks, axis=0).astype(jnp.int32)
    cnt_ref[...] = jnp.broadcast_to(base, cnt_ref.shape).astype(jnp.int32)


def router(x2d, gain_l, router_w_l, router_b_l, layer):
    n, d = x2d.shape
    n_exp = router_w_l.shape[-1]
    tb = min(512, n)
    e_pad = -(-n_exp // V7X_LANES) * V7X_LANES
    rw = jnp.pad(router_w_l[layer].astype(f32), ((0, 0), (0, e_pad - n_exp)))
    w_hi = rw.astype(bf16)
    w_lo = (rw - w_hi.astype(f32)).astype(bf16)
    rb = router_b_l[layer].reshape(n_exp, 1)
    slot_spec = pl.BlockSpec((TOP_K, tb), lambda i: (0, i))
    nbytes = 2 * tb * d * 4 + 2 * tb * d * 2 + 4 * e_pad * d * 2 + 6 * tb * d * 4
    return pl.pallas_call(
        functools.partial(_router_kernel, tb=tb, n_exp=n_exp),
        grid=(n // tb,),
        in_specs=[pl.BlockSpec((tb, d), lambda i: (i, 0)),
                  pl.BlockSpec((None, 1, d), lambda i: (layer, 0, 0)),
                  pl.BlockSpec((d, e_pad), lambda i: (0, 0)),
                  pl.BlockSpec((d, e_pad), lambda i: (0, 0)),
                  pl.BlockSpec((n_exp, 1), lambda i: (0, 0))],
        out_specs=[pl.BlockSpec((tb, d // 2), lambda i: (i, 0)), slot_spec, slot_spec, slot_spec,
                   pl.BlockSpec((n_exp, V7X_LANES), lambda i: (0, 0))],
        out_shape=[jax.ShapeDtypeStruct((n, d // 2), jnp.uint32),
                   jax.ShapeDtypeStruct((TOP_K, n), jnp.int32),
                   jax.ShapeDtypeStruct((TOP_K, n), f32),
                   jax.ShapeDtypeStruct((TOP_K, n), jnp.int32),
                   jax.ShapeDtypeStruct((n_exp, V7X_LANES), jnp.int32)],
        scratch_shapes=[pltpu.VMEM((n_exp, 1), f32)],
        compiler_params=_params(("arbitrary",), nbytes),
        name="router_topk",
    )(x2d, gain_l.reshape(gain_l.shape[0], 1, d), w_hi, w_lo, rb)


def _moe_kernel(tile_e_ref, tile_rows_ref, n_used_ref,
                xs_hbm, wgu_ref, bgu_ref, wdn_ref, bdn_ref,
                y_ref,
                stage_ref, xs_ref, act_ref, wgu_bf, wdn_bf, raw_a, raw_b, row_sem,
                *, tm, sub, n_gu, n_dn, tgu):
    raw_refs = (raw_a, raw_b)
    t = pl.program_id(0)
    s = pl.program_id(1)
    n_used = n_used_ref[0]
    half = stage_ref.shape[-1]
    kh_rows = wgu_bf.shape[0]
    th = tgu // 2
    n_sb = tm // sub
    n_gu_steps = 2 * n_gu
    rows = tile_rows_ref[t]
    ns = (rows + sub - 1) // sub

    piece = stage_ref.shape[1]
    n_pieces_max = tm // piece

    def n_pieces(tile):
        return (tile_rows_ref[tile] + sub - 1) // sub * (sub // piece)

    def piece_copy(tile, q):
        return pltpu.make_async_copy(xs_hbm.at[pl.ds(tile * tm + q * piece, piece), :], stage_ref.at[q % 2],
                                     row_sem.at[q % 2])

    def start_tile(tile):
        for q in range(2):
            @pl.when(q < n_pieces(tile))
            def _():
                piece_copy(tile, q).start()

    @pl.when(s == 0)
    def _stage_rows():
        @pl.when(t == 0)
        def _first():
            start_tile(0)
            for ref in raw_refs:
                ref[...] = jnp.zeros_like(ref)

        for q in range(n_pieces_max):
            @pl.when(q < n_pieces(t))
            def _unpack():
                piece_copy(t, q).wait()
                live = lax.broadcasted_iota(jnp.int32, (piece, half), 0) < rows - q * piece
                w = jnp.where(live, stage_ref[q % 2], jnp.uint32(0))
                xs_ref[q * piece:(q + 1) * piece, 0:half] = lax.bitcast_convert_type(w << 16, f32).astype(bf16)
                xs_ref[q * piece:(q + 1) * piece, half:2 * half] = lax.bitcast_convert_type(
                    w & jnp.uint32(0xFFFF0000), f32).astype(bf16)
                if q + 2 < n_pieces_max:
                    @pl.when(q + 2 < n_pieces(t))
                    def _():
                        piece_copy(t, q + 2).start()

    @pl.when((s == 1) & (t + 1 < n_used))
    def _prefetch_rows():
        start_tile(t + 1)

    def activate(src_ref, rb_src, s_dst):
        sel = (lax.broadcasted_iota(jnp.int32, (tgu, th), 0) == 2 * lax.broadcasted_iota(jnp.int32, (tgu, th), 1)).astype(bf16)
        r_src = pl.multiple_of(rb_src * sub, sub)
        gu = src_ref[pl.ds(r_src, sub), :]
        glu = jnp.minimum(gu, SWIGLU_LIMIT)
        glu = glu * _sigmoid(SWIGLU_ALPHA * glu)
        lin = jnp.clip(gu, -SWIGLU_LIMIT, SWIGLU_LIMIT) + 1.0
        prod = glu * pltpu.roll(lin, tgu - 1, 1)
        act_ref[s_dst, pl.ds(r_src, sub), :] = jnp.dot(prod.astype(bf16), sel, preferred_element_type=f32).astype(bf16)

    n_tile = s // 2
    p_cur = n_tile % 2

    @pl.when((s < n_gu_steps) & (s % 2 == 0))
    def _gate_up_first_half():
        wgu_bf[...] = wgu_ref[...].astype(bf16)

        for p in range(2):
            @pl.when(p_cur == p)
            def _():
                def body(rb, c):
                    r0 = pl.multiple_of(rb * sub, sub)
                    raw_refs[p][pl.ds(r0, sub), :] = jnp.dot(xs_ref[pl.ds(r0, sub), 0:kh_rows], wgu_bf[...],
                                                             preferred_element_type=f32) + bgu_ref[...]
                    activate(raw_refs[1 - p], rb, jnp.where(n_tile == 0, n_gu, n_tile - 1))
                    return c

                lax.fori_loop(0, ns, body, 0)

    @pl.when((s < n_gu_steps) & (s % 2 == 1))
    def _gate_up_second_half():
        wgu_bf[...] = wgu_ref[...].astype(bf16)
        for p in range(2):
            @pl.when(p_cur == p)
            def _():
                def body(rb, c):
                    r0 = pl.multiple_of(rb * sub, sub)
                    raw_refs[p][pl.ds(r0, sub), :] += jnp.dot(xs_ref[pl.ds(r0, sub), kh_rows:2 * kh_rows], wgu_bf[...],
                                                              preferred_element_type=f32)
                    return c

                lax.fori_loop(0, ns, body, 0)

    @pl.when(s == n_gu_steps)
    def _last_activation():
        def body(rb, c):
            activate(raw_refs[(n_gu - 1) % 2], rb, n_gu - 1)
            return c

        lax.fori_loop(0, ns, body, 0)

    @pl.when(s >= n_gu_steps)
    def _down():
        wdn_bf[...] = wdn_ref[...].astype(bf16)

        def body(rb, c):
            r0 = pl.multiple_of(rb * sub, sub)
            a = jnp.concatenate([act_ref[g, pl.ds(r0, sub), :] for g in range(n_gu)], axis=1)
            y = jnp.dot(a, wdn_bf[...], preferred_element_type=f32) + bdn_ref[...]
            hw = y.shape[1] // 2
            bits = lax.bitcast_convert_type(y.astype(bf16).astype(f32), jnp.uint32)
            y_ref[pl.ds(r0, sub), :] = (bits[:, :hw] >> 16) | bits[:, hw:]
            return c

        lax.fori_loop(0, ns, body, 0)

        def clear(rb, c):
            r0 = pl.multiple_of(rb * sub, sub)
            y_ref[pl.ds(r0, sub), :] = jnp.zeros((sub, y_ref.shape[1]), jnp.uint32)
            return c

        lax.fori_loop(ns, n_sb, clear, 0)


def _down_tile(d):
    return min(1024, d)


def moe_experts(xs_sorted, tile_e, tile_rows, n_used, w_gu, b_gu, w_dn, b_dn, layer, tm, sub):
    half = xs_sorted.shape[1]
    d = 2 * half
    n_layers, n_exp, _, two_f = w_gu.shape
    ff = two_f // 2
    t_max = xs_sorted.shape[0] // tm
    tgu = min(512, two_f)
    tdn = _down_tile(d)
    n_gu, n_dn = two_f // tgu, d // tdn

    n_gu_steps = 2 * n_gu

    def gu_idx(t, s, te, tr, nu):
        step = jnp.minimum(s, n_gu_steps - 1)
        return (layer, te[t], step % 2, step // 2)

    def bgu_idx(t, s, te, tr, nu):
        return (layer, te[t], 0, jnp.minimum(s, n_gu_steps - 1) // 2)

    def dn_idx(t, s, te, tr, nu):
        return (layer, te[t], 0, jnp.maximum(s - n_gu_steps, 0))

    def y_idx(t, s, te, tr, nu):
        return (t, jnp.maximum(s - n_gu_steps, 0))

    nbytes = (2 * sub * half * 4 + tm * d * 2 + (n_gu + 1) * tm * tgu + 2 * (d // 2) * tgu * 4 + (d // 2) * tgu * 2
              + 2 * ff * tdn * 4 + ff * tdn * 2 + 2 * tm * tgu * 4 + 2 * tm * tdn * 4)
    grid_spec = pltpu.PrefetchScalarGridSpec(
        num_scalar_prefetch=3,
        grid=(n_used[0], n_gu_steps + n_dn),
        in_specs=[pl.BlockSpec(memory_space=pl.ANY),
                  pl.BlockSpec((None, None, d // 2, tgu), gu_idx),
                  pl.BlockSpec((None, None, 1, tgu), bgu_idx),
                  pl.BlockSpec((None, None, ff, tdn), dn_idx),
                  pl.BlockSpec((None, None, 1, tdn), dn_idx)],
        out_specs=pl.BlockSpec((tm, tdn // 2), y_idx),
        scratch_shapes=[pltpu.VMEM((2, max(sub // 2, V7X_SUBLANES), half), jnp.uint32),
                        pltpu.VMEM((tm, d), bf16),
                        pltpu.VMEM((n_gu + 1, tm, tgu // 2), bf16),
                        pltpu.VMEM((d // 2, tgu), bf16),
                        pltpu.VMEM((ff, tdn), bf16),
                        pltpu.VMEM((tm, tgu), f32),
                        pltpu.VMEM((tm, tgu), f32),
                        pltpu.SemaphoreType.DMA((2,))],
    )
    return pl.pallas_call(
        functools.partial(_moe_kernel, tm=tm, sub=sub, n_gu=n_gu, n_dn=n_dn, tgu=tgu),
        grid_spec=grid_spec,
        out_shape=jax.ShapeDtypeStruct((t_max * tm, d // 2), jnp.uint32),
        compiler_params=_params(("arbitrary", "arbitrary"), nbytes),
        name="moe_experts",
    )(tile_e, tile_rows, n_used, xs_sorted, w_gu,
      b_gu.reshape(n_layers, n_exp, 1, two_f), w_dn, b_dn.reshape(n_layers, n_exp, 1, d))


def _dispatch_kernel(dest_hbm, hp_ref, xs_hbm, dest_smem, dest_sem, row_sem, *, tb, n_steps):
    i = pl.program_id(0)
    slot = i % 2

    def dest_copy(step):
        return pltpu.make_async_copy(dest_hbm.at[step], dest_smem.at[step % 2], dest_sem)

    def row_copy(r, row):
        return pltpu.make_async_copy(hp_ref.at[pl.ds(r, 1), :], xs_hbm.at[pl.ds(row, 1), :], row_sem)

    @pl.when(i == 0)
    def _first():
        dest_copy(0).start()

    dest_copy(i).wait()

    @pl.when(i + 1 < n_steps)
    def _next():
        dest_copy(i + 1).start()

    def issue(r, c):
        for k in range(TOP_K):
            row_copy(r, dest_smem[slot, k * tb + r]).start()
        return c

    lax.fori_loop(0, tb, issue, 0, unroll=2)
    for _ in range(TOP_K):
        pltpu.make_async_copy(hp_ref, xs_hbm.at[pl.ds(0, tb), :], row_sem).wait()


def dispatch(hp, dest_blk, n_rows_out):
    n, half = hp.shape
    n_steps, per_step = dest_blk.shape
    tb = per_step // TOP_K
    return pl.pallas_call(
        functools.partial(_dispatch_kernel, tb=tb, n_steps=n_steps),
        grid=(n_steps,),
        in_specs=[pl.BlockSpec(memory_space=pl.ANY),
                  pl.BlockSpec((tb, half), lambda i: (i, 0))],
        out_specs=pl.BlockSpec(memory_space=pl.ANY),
        out_shape=jax.ShapeDtypeStruct((n_rows_out, half), jnp.uint32),
        scratch_shapes=[pltpu.SMEM((2, per_step), jnp.int32),
                        pltpu.SemaphoreType.DMA(()),
                        pltpu.SemaphoreType.DMA(())],
        compiler_params=_params(("arbitrary",), 2 * tb * half * 4),
        name="moe_dispatch",
    )(dest_blk, hp)


def _combine_kernel(dest_hbm, y_hbm, x_ref, gate_ref, gain_ref, *refs, tb, n_steps, final, pack):
    if final:
        o_ref, dest_smem, ybuf, dest_sem, row_sem = refs
        xo_ref = None
    else:
        xo_ref, o_ref, dest_smem, ybuf, dest_sem, row_sem = refs
    i = pl.program_id(0)
    n_rows = TOP_K * tb

    def dest_copy(step, slot):
        return pltpu.make_async_copy(dest_hbm.at[step], dest_smem.at[slot], dest_sem)

    def row_copy(slot, r):
        return pltpu.make_async_copy(y_hbm.at[pl.ds(dest_smem[slot, r], 1), :], ybuf.at[slot, pl.ds(r, 1), :],
                                     row_sem.at[slot])

    def wait_slot(slot):
        pltpu.make_async_copy(y_hbm.at[pl.ds(0, n_rows), :], ybuf.at[slot], row_sem.at[slot]).wait()

    @pl.when(i == 0)
    def _first():
        dest_copy(0, 0).start()
        dest_copy(0, 0).wait()

        def issue(r, c):
            row_copy(0, r).start()
            return c

        lax.fori_loop(0, n_rows, issue, 0, unroll=8)
        dest_copy(1, 1).start()

    dest_copy(0, 0).wait()
    for p in range(2):
        @pl.when(i % 2 == p)
        def _():
            for r in range(n_rows):
                row_copy(1 - p, r).start()
            wait_slot(p)
            g = gate_ref[...]
            acc = x_ref[...]
            for k in range(TOP_K):
                w = ybuf[p, k * tb:(k + 1) * tb, :]
                lo = lax.bitcast_convert_type(w << 16, f32)
                hi = lax.bitcast_convert_type(w & jnp.uint32(0xFFFF0000), f32)
                pieces = []
                for c0 in range(0, w.shape[1], pack):
                    pieces += [lo[:, c0:c0 + pack], hi[:, c0:c0 + pack]]
                acc = acc + g[:, k:k + 1] * jnp.concatenate(pieces, axis=1)
            if not final:
                xo_ref[...] = acc
            inv = lax.rsqrt(jnp.mean(acc * acc, axis=-1, keepdims=True) + EPS)
            o_ref[...] = (acc * inv * gain_ref[...]).astype(o_ref.dtype)

            @pl.when(i + 2 <= n_steps)
            def _():
                dest_copy(i + 2, p).start()

            @pl.when(i == n_steps - 1)
            def _():
                wait_slot(1 - p)


def combine(x2d, y, dest_blk, gates_t, gain2d, final):
    n, d = x2d.shape
    n_steps, per_step = dest_blk.shape
    tb = per_step // TOP_K
    pack = _down_tile(d) // 2
    dest_blk = jnp.concatenate([dest_blk, jnp.zeros((1, per_step), dest_blk.dtype)], axis=0)
    row_spec = pl.BlockSpec((tb, d), lambda i: (i, 0))
    if final:
        out_shape = jax.ShapeDtypeStruct((n, d), f32)
        out_specs = row_spec
    else:
        out_shape = [jax.ShapeDtypeStruct((n, d), f32), jax.ShapeDtypeStruct((n, d), bf16)]
        out_specs = [row_spec, row_spec]
    nbytes = 2 * TOP_K * tb * d * 2 + 8 * tb * d * 4
    return pl.pallas_call(
        functools.partial(_combine_kernel, tb=tb, n_steps=n_steps, final=final, pack=pack),
        grid=(n_steps,),
        in_specs=[pl.BlockSpec(memory_space=pl.ANY),
                  pl.BlockSpec(memory_space=pl.ANY),
                  row_spec,
                  pl.BlockSpec((tb, TOP_K), lambda i: (i, 0)),
                  pl.BlockSpec((1, d), lambda i: (0, 0))],
        out_specs=out_specs,
        out_shape=out_shape,
        scratch_shapes=[pltpu.SMEM((2, TOP_K * tb), jnp.int32),
                        pltpu.VMEM((2, TOP_K * tb, d // 2), jnp.uint32),
                        pltpu.SemaphoreType.DMA(()),
                        pltpu.SemaphoreType.DMA((2,))],
        compiler_params=_params(("arbitrary",), nbytes),
        name="moe_combine",
    )(dest_blk, y, x2d, gates_t, gain2d)


def _dispatch_plan(counts, idx, rank, tm, t_max, tb):
    n_exp = counts.shape[0]
    n = idx.shape[1]
    tiles_e = (counts + tm - 1) // tm
    tile_end = jnp.cumsum(tiles_e)
    tile_start = tile_end - tiles_e
    n_used = tile_end[-1]
    expert = jnp.arange(n_exp, dtype=jnp.int32)[:, None, None]
    dest = rank + jnp.sum(jnp.where(idx[None] == expert, (tile_start * tm)[:, None, None], 0), axis=0)
    tile = jnp.arange(t_max, dtype=jnp.int32)
    tile_c = jnp.minimum(tile, n_used - 1)
    tile_e = jnp.minimum(jnp.searchsorted(tile_end, tile_c, side="right"), n_exp - 1).astype(jnp.int32)
    rows = jnp.clip(counts[tile_e] - (tile_c - tile_start[tile_e]) * tm, 0, tm)
    tile_rows = jnp.where(tile < n_used, rows, 0).astype(jnp.int32)
    dest_blk = dest.reshape(TOP_K, n // tb, tb).transpose(1, 0, 2).reshape(n // tb, TOP_K * tb)
    return dest_blk, tile_e, tile_rows, n_used.reshape(1).astype(jnp.int32)


def kernel(x, norm_mix, w_in, attn_rel_bias, attn_out_gain, conv_w, conv_b, lru_w_a, lru_b_a, lru_w_i, lru_b_i,
           lru_lambda, rec_out_gain, w_out, norm_ffn, router_w, router_b, expert_w_gate_up, expert_b_gate_up,
           expert_w_down, expert_b_down, norm_final):
    batch, seq, d = x.shape
    n = batch * seq
    depth = norm_mix.shape[0]
    attn_width = attn_out_gain.shape[1]
    lru_width = rec_out_gain.shape[1]
    n_exp = router_w.shape[-1]
    sub = min(512, n)
    tm = sub * min(3, n // sub)
    t_max = (n * TOP_K) // tm + n_exp
    tb = min(128, n)

    x2d = x.reshape(n, d)
    h = rmsnorm_bf16(x2d, norm_mix, 0)
    out = None
    for l in range(depth):
        qkv = matmul_ws([h], w_in, l, [0], 0, 3 * attn_width, bf16, name="in_proj_qkv")
        rg = matmul_ws([h], w_in, l, [0], 3 * attn_width, 2 * lru_width, f32, name="in_proj_rec")
        attn = attention(qkv, attn_rel_bias[l], attn_out_gain, l, batch, seq, attn_width)
        rec = rglru_block(rg, conv_w, conv_b, lru_w_a, lru_b_a, lru_w_i, lru_b_i, lru_lambda, rec_out_gain,
                          l, batch, seq)
        assert attn_width == lru_width
        x2d = matmul_ws([attn, rec], w_out, l, [0, 1], 0, d, f32, residual=x2d, name="out_proj")
        hp, idx, gates, rank, cnt = router(x2d, norm_ffn, router_w, router_b, l)
        dest_blk, tile_e, tile_rows, n_used = _dispatch_plan(cnt[:, 0], idx, rank, tm, t_max, tb)
        xs_sorted = dispatch(hp, dest_blk, t_max * tm)
        y = moe_experts(xs_sorted, tile_e, tile_rows, n_used, expert_w_gate_up, expert_b_gate_up,
                        expert_w_down, expert_b_down, l, tm, sub)
        final = l == depth - 1
        gain = norm_final.reshape(1, d) if final else norm_mix[l + 1].reshape(1, d)
        res = combine(x2d, y, dest_blk, gates.T, gain, final)
        if final:
            out = res
        else:
            x2d, h = res
    return out.reshape(batch, seq, d)
```

```python
import functools
import math

import jax
import jax.numpy as jnp
from jax import lax
from jax.experimental import pallas as pl
from jax.experimental.pallas import tpu as pltpu

CHUNK = 64
LEFT_CHUNKS = 8
ATTN_HEAD_DIM = 128
REL_CLIP = 128
LRU_C = 8.0
TOP_K = 4
SWIGLU_LIMIT = 7.0
SWIGLU_ALPHA = 1.702
EPS = 1e-6

V7X_LANES = 128
V7X_SUBLANES = 8
V7X_VMEM_BYTES = 64 * 1024 * 1024

MASK_VALUE = -1e30

f32 = jnp.float32
bf16 = jnp.bfloat16


def _vmem_limit(nbytes):
    return int(min(nbytes + 12 * 1024 * 1024, V7X_VMEM_BYTES - 6 * 1024 * 1024))


def _params(sem, nbytes):
    return pltpu.CompilerParams(dimension_semantics=sem, vmem_limit_bytes=_vmem_limit(nbytes))


def _rmsnorm_kernel(x_ref, g_ref, o_ref):
    x = x_ref[...]
    inv = lax.rsqrt(jnp.mean(x * x, axis=-1, keepdims=True) + EPS)
    o_ref[...] = (x * inv * g_ref[...]).astype(o_ref.dtype)


def rmsnorm_bf16(x2d, gain_l, layer):
    n, d = x2d.shape
    tb = min(256, n)
    return pl.pallas_call(
        _rmsnorm_kernel,
        grid=(n // tb,),
        in_specs=[pl.BlockSpec((tb, d), lambda i: (i, 0)),
                  pl.BlockSpec((None, 1, d), lambda i: (layer, 0, 0))],
        out_specs=pl.BlockSpec((tb, d), lambda i: (i, 0)),
        out_shape=jax.ShapeDtypeStruct((n, d), bf16),
        compiler_params=_params(("arbitrary",), 2 * tb * d * 6),
        name="rmsnorm_bf16",
    )(x2d, gain_l.reshape(gain_l.shape[0], 1, d))


def _matmul_kernel(*refs, n_pairs, has_res):
    a_refs = refs[:n_pairs]
    w_refs = refs[n_pairs:2 * n_pairs]
    pos = 2 * n_pairs
    res_ref = refs[pos] if has_res else None
    pos += int(has_res)
    o_ref = refs[pos]
    wbf_refs = refs[pos + 1:pos + 1 + n_pairs]

    @pl.when(pl.program_id(1) == 0)
    def _cast_weights():
        for w_ref, wbf_ref in zip(w_refs, wbf_refs):
            wbf_ref[...] = w_ref[...].astype(bf16)

    acc = jnp.dot(a_refs[0][...], wbf_refs[0][...], preferred_element_type=f32)
    for a_ref, wbf_ref in zip(a_refs[1:], wbf_refs[1:]):
        acc += jnp.dot(a_ref[...], wbf_ref[...], preferred_element_type=f32)
    if has_res:
        acc += res_ref[...]
    o_ref[...] = acc.astype(o_ref.dtype)


def matmul_ws(a_list, w_stack, layer, k_blocks, col0, n_cols, out_dtype, residual=None, name="matmul_ws"):
    m = a_list[0].shape[0]
    tm = min(1024, m)
    tn = min(512, n_cols)
    assert m % tm == 0 and n_cols % tn == 0 and col0 % tn == 0
    cb0 = col0 // tn
    n_pairs = len(a_list)
    in_specs, scratch = [], []
    nbytes = 2 * tm * tn * 4 * (2 if residual is not None else 1)
    for a in a_list:
        kp = a.shape[1]
        in_specs.append(pl.BlockSpec((tm, kp), lambda j, i: (i, 0)))
        nbytes += 2 * tm * kp * 2
    for a, kb in zip(a_list, k_blocks):
        kp = a.shape[1]
        in_specs.append(pl.BlockSpec((None, kp, tn), functools.partial(lambda j, i, kb: (layer, kb, j + cb0), kb=kb)))
        scratch.append(pltpu.VMEM((kp, tn), bf16))
        nbytes += 2 * kp * tn * 4 + kp * tn * 2
    args = list(a_list) + [w_stack] * n_pairs
    if residual is not None:
        in_specs.append(pl.BlockSpec((tm, tn), lambda j, i: (i, j)))
        args.append(residual)
    return pl.pallas_call(
        functools.partial(_matmul_kernel, n_pairs=n_pairs, has_res=residual is not None),
        grid=(n_cols // tn, m // tm),
        in_specs=in_specs,
        out_specs=pl.BlockSpec((tm, tn), lambda j, i: (i, j)),
        out_shape=jax.ShapeDtypeStruct((m, n_cols), out_dtype),
        scratch_shapes=scratch,
        compiler_params=_params(("arbitrary", "arbitrary"), nbytes),
        name=name,
    )(*args)


def _attn_kernel(q_ref, *refs, n_kb, hb, tq, n_hg):
    k_refs = refs[:n_kb]
    v_refs = refs[n_kb:2 * n_kb]
    row0_ref, gain_ref, o_ref, raw_ref, bias_ref = refs[2 * n_kb:]
    i = pl.program_id(1)
    hg = pl.program_id(2)
    dh = ATTN_HEAD_DIM
    scale = 1.0 / math.sqrt(dh)
    n_keys = n_kb * tq

    @pl.when((pl.program_id(0) == 0) & (i == 0) & (hg == 0))
    def _build_bias():
        width = row0_ref.shape[-1]
        qc = ((n_kb - 1) * tq + lax.broadcasted_iota(jnp.int32, (tq, n_keys), 0)) // CHUNK
        kc = lax.broadcasted_iota(jnp.int32, (tq, n_keys), 1) // CHUNK
        valid = (kc <= qc) & (kc >= qc - LEFT_CHUNKS)
        for h in range(bias_ref.shape[0]):
            rows = pltpu.roll(jnp.broadcast_to(row0_ref[h], (tq, width)), 0, 1, stride=1, stride_axis=0)
            bias_ref[h] = jnp.where(valid, rows[:, :n_keys] * (1.0 / scale), MASK_VALUE)

    outs = []
    for hh in range(hb):
        sl = slice(hh * dh, (hh + 1) * dh)
        q = q_ref[:, sl]
        bias = bias_ref[hg * hb + hh]
        parts = []
        for j in range(n_kb - 1, -1, -1):
            s = lax.dot_general(q, k_refs[j][:, sl], (((1,), (1,)), ((), ())), preferred_element_type=f32)
            c0 = (n_kb - 1 - j) * tq
            s = s + bias[:, c0:c0 + tq]
            if j > 0:
                s = s + jnp.where(i >= j, 0.0, MASK_VALUE)
            parts.append(s)
        m = parts[0].max(axis=-1, keepdims=True)
        for s in parts[1:]:
            m = jnp.maximum(m, s.max(axis=-1, keepdims=True))
        l = jnp.zeros_like(m)
        acc = jnp.zeros((tq, dh), f32)
        for idx, s in enumerate(parts):
            j = n_kb - 1 - idx
            p = jnp.exp2((s - m) * (scale * math.log2(math.e)))
            l = l + p.sum(axis=-1, keepdims=True)
            acc = acc + jnp.dot(p.astype(bf16), v_refs[j][:, sl], preferred_element_type=f32)
        outs.append(acc / l)
    raw_ref[hg] = jnp.concatenate(outs, axis=-1)

    @pl.when(hg == n_hg - 1)
    def _finish():
        ss = jnp.zeros((tq, 1), f32)
        for g in range(n_hg):
            r = raw_ref[g]
            ss = ss + jnp.sum(r * r, axis=-1, keepdims=True)
        inv = lax.rsqrt(ss / (n_hg * hb * dh) + EPS)
        for g in range(n_hg):
            w = hb * dh
            o_ref[:, g * w:(g + 1) * w] = (raw_ref[g] * inv * gain_ref[:, g * w:(g + 1) * w]).astype(o_ref.dtype)


def _attn_bias_row0(rel_bias_l, tq, n_kb):
    n_heads = rel_bias_l.shape[0]
    rb = rel_bias_l.astype(f32)
    far = (n_kb - 1) * tq - REL_CLIP
    near = n_kb * tq - far - (2 * REL_CLIP + 1)
    assert far >= 0 and near >= 0
    fill = lambda col, width: jnp.broadcast_to(rb[:, col:col + 1], (n_heads, width))
    row0 = jnp.concatenate([fill(2 * REL_CLIP, far), rb[:, ::-1], fill(0, near), fill(2 * REL_CLIP, tq)], axis=1)
    return row0.reshape(n_heads, 1, (n_kb + 1) * tq)


def attention(qkv, rel_bias_l, gain_l, layer, batch, seq, attn_width):
    n = qkv.shape[0]
    n_heads = attn_width // ATTN_HEAD_DIM
    tq = min(256, seq)
    band = LEFT_CHUNKS * CHUNK
    assert band % tq == 0 and seq % tq == 0 and tq % CHUNK == 0
    n_kb = band // tq + 1
    nq = seq // tq
    hb = min(16, n_heads)
    n_hg = n_heads // hb
    w = hb * ATTN_HEAD_DIM
    row0 = _attn_bias_row0(rel_bias_l, tq, n_kb)

    def kv_spec(j, which):
        return pl.BlockSpec((tq, w), lambda b, i, g: (b * nq + jnp.maximum(i - j, 0), which * n_hg + g))

    in_specs = [pl.BlockSpec((tq, w), lambda b, i, g: (b * nq + i, g))]
    in_specs += [kv_spec(j, 1) for j in range(n_kb)] + [kv_spec(j, 2) for j in range(n_kb)]
    in_specs += [pl.BlockSpec(row0.shape, lambda b, i, g: (0, 0, 0)),
                 pl.BlockSpec((None, 1, attn_width), lambda b, i, g: (layer, 0, 0))]
    nbytes = 2 * (2 * n_kb + 1) * tq * w * 2 + n_heads * tq * n_kb * tq * 4 + 3 * tq * attn_width * 4
    return pl.pallas_call(
        functools.partial(_attn_kernel, n_kb=n_kb, hb=hb, tq=tq, n_hg=n_hg),
        grid=(batch, nq, n_hg),
        in_specs=in_specs,
        out_specs=pl.BlockSpec((tq, attn_width), lambda b, i, g: (b * nq + i, 0)),
        out_shape=jax.ShapeDtypeStruct((n, attn_width), bf16),
        scratch_shapes=[pltpu.VMEM((n_hg, tq, w), f32), pltpu.VMEM((n_heads, tq, n_kb * tq), f32)],
        compiler_params=_params(("arbitrary", "arbitrary", "arbitrary"), nbytes),
        name="band_attention",
    )(*([qkv] * (2 * n_kb + 1)), row0, gain_l.reshape(gain_l.shape[0], 1, attn_width))


def _sigmoid(x):
    return 0.5 * jnp.tanh(0.5 * x) + 0.5


def _lru_kernel(xr_ref, xg_ref, cw_ref, cb_ref, wa_ref, ba_ref, wi_ref, bi_ref, lam_ref, gain_ref,
                o_ref, xe_ref, a_ref, b_ref, h_ref, carry_ref, *, t, n_blocks, conv_width):
    i = pl.program_id(1)
    hdr = V7X_SUBLANES
    bw = wa_ref.shape[-1]

    @pl.when(i == 0)
    def _reset():
        xe_ref[0:hdr, :] = jnp.zeros((hdr, xe_ref.shape[1]), f32)
        carry_ref[...] = jnp.zeros_like(carry_ref)

    x = xr_ref[...]
    xe_ref[hdr:hdr + t, :] = x
    u = cb_ref[...] + cw_ref[conv_width - 1:conv_width, :] * x
    for j in range(conv_width - 1):
        u = u + cw_ref[j:j + 1, :] * xe_ref[pl.ds(hdr - (conv_width - 1) + j, t), :]
    xe_ref[0:hdr, :] = x[t - hdr:t, :]

    lam = lam_ref[...]
    nsp = -LRU_C * (jnp.maximum(-lam, 0.0) + jnp.log1p(jnp.exp(-jnp.abs(lam))))
    for nb in range(n_blocks):
        sl = slice(nb * bw, (nb + 1) * bw)
        ub = u[:, sl]
        ubb = ub.astype(bf16)
        r = _sigmoid(jnp.dot(ubb, wa_ref[nb].astype(bf16), preferred_element_type=f32) + ba_ref[:, sl])
        ig = _sigmoid(jnp.dot(ubb, wi_ref[nb].astype(bf16), preferred_element_type=f32) + bi_ref[:, sl])
        log_a = r * nsp[:, sl]
        a = jnp.exp(log_a)
        a_ref[:, sl] = a
        b_ref[:, sl] = jnp.sqrt(1.0 - a * a) * (ig * ub)

    a = a_ref[...]
    b = b_ref[...]
    row = lax.broadcasted_iota(jnp.int32, a.shape, 0) % hdr
    s = 1
    while s < hdr:
        keep = row >= s
        a_sh = jnp.where(keep, pltpu.roll(a, s, 0), 1.0)
        b_sh = jnp.where(keep, pltpu.roll(b, s, 0), 0.0)
        b = a * b_sh + b
        a = a * a_sh
        s *= 2
    a_ref[...] = a
    b_ref[...] = b

    def tile_step(j, carry):
        r0 = pl.multiple_of(j * hdr, hdr)
        h = a_ref[pl.ds(r0, hdr), :] * carry + b_ref[pl.ds(r0, hdr), :]
        h_ref[pl.ds(r0, hdr), :] = h
        return h[hdr - 1:hdr, :]

    carry_ref[...] = lax.fori_loop(0, t // hdr, tile_step, carry_ref[...])

    y = h_ref[...] * jax.nn.gelu(xg_ref[...], approximate=True)
    inv = lax.rsqrt(jnp.mean(y * y, axis=-1, keepdims=True) + EPS)
    o_ref[...] = (y * inv * gain_ref[...]).astype(o_ref.dtype)


def rglru_block(rg, conv_w, conv_b, w_a, b_a, w_i, b_i, lam, gain, layer, batch, seq):
    n = rg.shape[0]
    n_layers, conv_width, wr = conv_w.shape
    n_blocks = w_a.shape[1]
    bw = w_a.shape[2]
    t = min(512, seq)
    ns = seq // t
    row3 = lambda v: v.reshape(n_layers, 1, wr)
    vec_spec = pl.BlockSpec((None, 1, wr), lambda b, i: (layer, 0, 0))
    wspec = pl.BlockSpec((None, n_blocks, bw, bw), lambda b, i: (layer, 0, 0, 0))
    nbytes = 2 * 2 * t * wr * 4 + 2 * t * wr * 2 + 4 * (t + 8) * wr * 4 + 4 * n_blocks * bw * bw * 4 + 8 * t * wr * 4
    return pl.pallas_call(
        functools.partial(_lru_kernel, t=t, n_blocks=n_blocks, conv_width=conv_width),
        grid=(batch, ns),
        in_specs=[pl.BlockSpec((t, wr), lambda b, i: (b * ns + i, 0)),
                  pl.BlockSpec((t, wr), lambda b, i: (b * ns + i, 1)),
                  pl.BlockSpec((None, conv_width, wr), lambda b, i: (layer, 0, 0)),
                  vec_spec, wspec, vec_spec, wspec, vec_spec, vec_spec, vec_spec],
        out_specs=pl.BlockSpec((t, wr), lambda b, i: (b * ns + i, 0)),
        out_shape=jax.ShapeDtypeStruct((n, wr), bf16),
        scratch_shapes=[pltpu.VMEM((t + V7X_SUBLANES, wr), f32), pltpu.VMEM((t, wr), f32),
                        pltpu.VMEM((t, wr), f32), pltpu.VMEM((t, wr), f32), pltpu.VMEM((1, wr), f32)],
        compiler_params=_params(("arbitrary", "arbitrary"), nbytes),
        name="conv_rglru",
    )(rg, rg, conv_w, row3(conv_b), w_a, row3(b_a), w_i, row3(b_i), row3(lam), row3(gain))


def _router_kernel(x_ref, g_ref, whi_ref, wlo_ref, rb_ref, hp_ref, idx_ref, gate_ref, rank_ref, cnt_ref, carry_ref,
                   *, tb, n_exp):
    step = pl.program_id(0)

    @pl.when(step == 0)
    def _reset():
        carry_ref[...] = jnp.zeros_like(carry_ref)

    x = x_ref[...]
    inv = lax.rsqrt(jnp.mean(x * x, axis=-1, keepdims=True) + EPS)
    h = x * inv * g_ref[...]
    half = h.shape[1] // 2
    h_hi = h.astype(bf16)
    h_hi32 = h_hi.astype(f32)
    bits = lax.bitcast_convert_type(h_hi32, jnp.uint32)
    hp_ref[...] = (bits[:, :half] >> 16) | bits[:, half:]

    h_lo = (h - h_hi32).astype(bf16)
    w_hi = whi_ref[...]
    acc = (jnp.dot(h_hi, w_hi, preferred_element_type=f32) + jnp.dot(h_lo, w_hi, preferred_element_type=f32)
           + jnp.dot(h_hi, wlo_ref[...], preferred_element_type=f32))
    logits = acc.T[:n_exp] + rb_ref[...]
    e_iota = lax.broadcasted_iota(jnp.int32, (n_exp, tb), 0)
    tri = (lax.broadcasted_iota(jnp.int32, (tb, tb), 0) < lax.broadcasted_iota(jnp.int32, (tb, tb), 1)).astype(bf16)
    base = carry_ref[...]
    vals, idxs, ranks = [], [], []
    l = logits
    for _ in range(TOP_K):
        m = l.max(axis=0, keepdims=True)
        idx = jnp.min(jnp.where(l == m, e_iota, n_exp), axis=0, keepdims=True)
        onehot = e_iota == idx
        ohf = onehot.astype(f32)
        before = jnp.dot(ohf.astype(bf16), tri, preferred_element_type=f32)
        ranks.append(jnp.sum(jnp.where(onehot, before + base, 0.0), axis=0, keepdims=True))
        base = base + jnp.sum(ohf, axis=1, keepdims=True)
        vals.append(m)
        idxs.append(idx)
        l = jnp.where(onehot, -jnp.inf, l)
    carry_ref[...] = base
    exps = [jnp.exp(v - vals[0]) for v in vals]
    denom = exps[0]
    for e in exps[1:]:
        denom = denom + e
    idx_ref[...] = jnp.concatenate(idxs, axis=0)
    gate_ref[...] = jnp.concatenate([e / denom for e in exps], axis=0)
    rank_ref[...] = jnp.concatenate(ranks, axis=0).astype(jnp.int32)
    cnt_ref[...] = jnp.broadcast_to(base, cnt_ref.shape).astype(jnp.int32)


def router(x2d, gain_l, router_w_l, router_b_l, layer):
    n, d = x2d.shape
    n_exp = router_w_l.shape[-1]
    tb = min(512, n)
    e_pad = -(-n_exp // V7X_LANES) * V7X_LANES
    rw = jnp.pad(router_w_l[layer].astype(f32), ((0, 0), (0, e_pad - n_exp)))
    w_hi = rw.astype(bf16)
    w_lo = (rw - w_hi.astype(f32)).astype(bf16)
    rb = router_b_l[layer].reshape(n_exp, 1)
    slot_spec = pl.BlockSpec((TOP_K, tb), lambda i: (0, i))
    nbytes = 2 * tb * d * 4 + 2 * tb * d * 2 + 4 * e_pad * d * 2 + 6 * tb * d * 4
    return pl.pallas_call(
        functools.partial(_router_kernel, tb=tb, n_exp=n_exp),
        grid=(n // tb,),
        in_specs=[pl.BlockSpec((tb, d), lambda i: (i, 0)),
                  pl.BlockSpec((None, 1, d), lambda i: (layer, 0, 0)),
                  pl.BlockSpec((d, e_pad), lambda i: (0, 0)),
                  pl.BlockSpec((d, e_pad), lambda i: (0, 0)),
                  pl.BlockSpec((n_exp, 1), lambda i: (0, 0))],
        out_specs=[pl.BlockSpec((tb, d // 2), lambda i: (i, 0)), slot_spec, slot_spec, slot_spec,
                   pl.BlockSpec((n_exp, V7X_LANES), lambda i: (0, 0))],
        out_shape=[jax.ShapeDtypeStruct((n, d // 2), jnp.uint32),
                   jax.ShapeDtypeStruct((TOP_K, n), jnp.int32),
                   jax.ShapeDtypeStruct((TOP_K, n), f32),
                   jax.ShapeDtypeStruct((TOP_K, n), jnp.int32),
                   jax.ShapeDtypeStruct((n_exp, V7X_LANES), jnp.int32)],
        scratch_shapes=[pltpu.VMEM((n_exp, 1), f32)],
        compiler_params=_params(("arbitrary",), nbytes),
        name="router_topk",
    )(x2d, gain_l.reshape(gain_l.shape[0], 1, d), w_hi, w_lo, rb)


def _moe_kernel(tile_e_ref, tile_rows_ref, n_used_ref,
                xs_hbm, wgu_ref, bgu_ref, wdn_ref, bdn_ref,
                y_ref,
                stage_ref, xs_ref, act_ref, wgu_bf, wdn_bf, raw_a, raw_b, row_sem,
                *, tm, sub, n_gu, n_dn, tgu):
    raw_refs = (raw_a, raw_b)
    t = pl.program_id(0)
    s = pl.program_id(1)
    n_used = n_used_ref[0]
    half = stage_ref.shape[-1]
    kh_rows = wgu_bf.shape[0]
    th = tgu // 2
    n_sb = tm // sub
    n_gu_steps = 2 * n_gu
    rows = tile_rows_ref[t]
    ns = (rows + sub - 1) // sub

    piece = stage_ref.shape[1]
    n_pieces_max = tm // piece

    def n_pieces(tile):
        return (tile_rows_ref[tile] + sub - 1) // sub * (sub // piece)

    def piece_copy(tile, q):
        return pltpu.make_async_copy(xs_hbm.at[pl.ds(tile * tm + q * piece, piece), :], stage_ref.at[q % 2],
                                     row_sem.at[q % 2])

    def start_tile(tile):
        for q in range(2):
            @pl.when(q < n_pieces(tile))
            def _():
                piece_copy(tile, q).start()

    @pl.when(s == 0)
    def _stage_rows():
        @pl.when(t == 0)
        def _first():
            start_tile(0)
            for ref in raw_refs:
                ref[...] = jnp.zeros_like(ref)

        for q in range(n_pieces_max):
            @pl.when(q < n_pieces(t))
            def _unpack():
                piece_copy(t, q).wait()
                live = lax.broadcasted_iota(jnp.int32, (piece, half), 0) < rows - q * piece
                w = jnp.where(live, stage_ref[q % 2], jnp.uint32(0))
                xs_ref[q * piece:(q + 1) * piece, 0:half] = lax.bitcast_convert_type(w << 16, f32).astype(bf16)
                xs_ref[q * piece:(q + 1) * piece, half:2 * half] = lax.bitcast_convert_type(
                    w & jnp.uint32(0xFFFF0000), f32).astype(bf16)
                if q + 2 < n_pieces_max:
                    @pl.when(q + 2 < n_pieces(t))
                    def _():
                        piece_copy(t, q + 2).start()

    @pl.when((s == 1) & (t + 1 < n_used))
    def _prefetch_rows():
        start_tile(t + 1)

    def activate(src_ref, rb_src, s_dst):
        sel = (lax.broadcasted_iota(jnp.int32, (tgu, th), 0) == 2 * lax.broadcasted_iota(jnp.int32, (tgu, th), 1)).astype(bf16)
        r_src = pl.multiple_of(rb_src * sub, sub)
        gu = src_ref[pl.ds(r_src, sub), :]
        glu = jnp.minimum(gu, SWIGLU_LIMIT)
        glu = glu * _sigmoid(SWIGLU_ALPHA * glu)
        lin = jnp.clip(gu, -SWIGLU_LIMIT, SWIGLU_LIMIT) + 1.0
        prod = glu * pltpu.roll(lin, tgu - 1, 1)
        act_ref[s_dst, pl.ds(r_src, sub), :] = jnp.dot(prod.astype(bf16), sel, preferred_element_type=f32).astype(bf16)

    n_tile = s // 2
    p_cur = n_tile % 2

    @pl.when((s < n_gu_steps) & (s % 2 == 0))
    def _gate_up_first_half():
        wgu_bf[...] = wgu_ref[...].astype(bf16)

        for p in range(2):
            @pl.when(p_cur == p)
            def _():
                def body(rb, c):
                    r0 = pl.multiple_of(rb * sub, sub)
                    raw_refs[p][pl.ds(r0, sub), :] = jnp.dot(xs_ref[pl.ds(r0, sub), 0:kh_rows], wgu_bf[...],
                                                             preferred_element_type=f32) + bgu_ref[...]
                    activate(raw_refs[1 - p], rb, jnp.where(n_tile == 0, n_gu, n_tile - 1))
                    return c

                lax.fori_loop(0, ns, body, 0)

    @pl.when((s < n_gu_steps) & (s % 2 == 1))
    def _gate_up_second_half():
        wgu_bf[...] = wgu_ref[...].astype(bf16)
        for p in range(2):
            @pl.when(p_cur == p)
            def _():
                def body(rb, c):
                    r0 = pl.multiple_of(rb * sub, sub)
                    raw_refs[p][pl.ds(r0, sub), :] += jnp.dot(xs_ref[pl.ds(r0, sub), kh_rows:2 * kh_rows], wgu_bf[...],
                                                              preferred_element_type=f32)
                    return c

                lax.fori_loop(0, ns, body, 0)

    @pl.when(s == n_gu_steps)
    def _last_activation():
        def body(rb, c):
            activate(raw_refs[(n_gu - 1) % 2], rb, n_gu - 1)
            return c

        lax.fori_loop(0, ns, body, 0)

    @pl.when(s >= n_gu_steps)
    def _down():
        wdn_bf[...] = wdn_ref[...].astype(bf16)

        def body(rb, c):
            r0 = pl.multiple_of(rb * sub, sub)
            a = jnp.concatenate([act_ref[g, pl.ds(r0, sub), :] for g in range(n_gu)], axis=1)
            y = jnp.dot(a, wdn_bf[...], preferred_element_type=f32) + bdn_ref[...]
            hw = y.shape[1] // 2
            bits = lax.bitcast_convert_type(y.astype(bf16).astype(f32), jnp.uint32)
            y_ref[pl.ds(r0, sub), :] = (bits[:, :hw] >> 16) | bits[:, hw:]
            return c

        lax.fori_loop(0, ns, body, 0)

        def clear(rb, c):
            r0 = pl.multiple_of(rb * sub, sub)
            y_ref[pl.ds(r0, sub), :] = jnp.zeros((sub, y_ref.shape[1]), jnp.uint32)
            return c

        lax.fori_loop(ns, n_sb, clear, 0)


def _down_tile(d):
    return min(1024, d)


def moe_experts(xs_sorted, tile_e, tile_rows, n_used, w_gu, b_gu, w_dn, b_dn, layer, tm, sub):
    half = xs_sorted.shape[1]
    d = 2 * half
    n_layers, n_exp, _, two_f = w_gu.shape
    ff = two_f // 2
    t_max = xs_sorted.shape[0] // tm
    tgu = min(512, two_f)
    tdn = _down_tile(d)
    n_gu, n_dn = two_f // tgu, d // tdn

    n_gu_steps = 2 * n_gu

    def gu_idx(t, s, te, tr, nu):
        step = jnp.minimum(s, n_gu_steps - 1)
        return (layer, te[t], step % 2, step // 2)

    def bgu_idx(t, s, te, tr, nu):
        return (layer, te[t], 0, jnp.minimum(s, n_gu_steps - 1) // 2)

    def dn_idx(t, s, te, tr, nu):
        return (layer, te[t], 0, jnp.maximum(s - n_gu_steps, 0))

    def y_idx(t, s, te, tr, nu):
        return (t, jnp.maximum(s - n_gu_steps, 0))

    nbytes = (2 * sub * half * 4 + tm * d * 2 + (n_gu + 1) * tm * tgu + 2 * (d // 2) * tgu * 4 + (d // 2) * tgu * 2
              + 2 * ff * tdn * 4 + ff * tdn * 2 + 2 * tm * tgu * 4 + 2 * tm * tdn * 4)
    grid_spec = pltpu.PrefetchScalarGridSpec(
        num_scalar_prefetch=3,
        grid=(n_used[0], n_gu_steps + n_dn),
        in_specs=[pl.BlockSpec(memory_space=pl.ANY),
                  pl.BlockSpec((None, None, d // 2, tgu), gu_idx),
                  pl.BlockSpec((None, None, 1, tgu), bgu_idx),
                  pl.BlockSpec((None, None, ff, tdn), dn_idx),
                  pl.BlockSpec((None, None, 1, tdn), dn_idx)],
        out_specs=pl.BlockSpec((tm, tdn // 2), y_idx),
        scratch_shapes=[pltpu.VMEM((2, max(sub // 2, V7X_SUBLANES), half), jnp.uint32),
                        pltpu.VMEM((tm, d), bf16),
                        pltpu.VMEM((n_gu + 1, tm, tgu // 2), bf16),
                        pltpu.VMEM((d // 2, tgu), bf16),
                        pltpu.VMEM((ff, tdn), bf16),
                        pltpu.VMEM((tm, tgu), f32),
                        pltpu.VMEM((tm, tgu), f32),
                        pltpu.SemaphoreType.DMA((2,))],
    )
    return pl.pallas_call(
        functools.partial(_moe_kernel, tm=tm, sub=sub, n_gu=n_gu, n_dn=n_dn, tgu=tgu),
        grid_spec=grid_spec,
        out_shape=jax.ShapeDtypeStruct((t_max * tm, d // 2), jnp.uint32),
        compiler_params=_params(("arbitrary", "arbitrary"), nbytes),
        name="moe_experts",
    )(tile_e, tile_rows, n_used, xs_sorted, w_gu,
      b_gu.reshape(n_layers, n_exp, 1, two_f), w_dn, b_dn.reshape(n_layers, n_exp, 1, d))


def _dispatch_kernel(dest_hbm, hp_ref, xs_hbm, dest_smem, dest_sem, row_sem, *, tb, n_steps):
    i = pl.program_id(0)
    slot = i % 2

    def dest_copy(step):
        return pltpu.make_async_copy(dest_hbm.at[step], dest_smem.at[step % 2], dest_sem)

    def row_copy(r, row):
        return pltpu.make_async_copy(hp_ref.at[pl.ds(r, 1), :], xs_hbm.at[pl.ds(row, 1), :], row_sem)

    @pl.when(i == 0)
    def _first():
        dest_copy(0).start()

    dest_copy(i).wait()

    @pl.when(i + 1 < n_steps)
    def _next():
        dest_copy(i + 1).start()

    def issue(r, c):
        for k in range(TOP_K):
            row_copy(r, dest_smem[slot, k * tb + r]).start()
        return c

    lax.fori_loop(0, tb, issue, 0, unroll=2)
    for _ in range(TOP_K):
        pltpu.make_async_copy(hp_ref, xs_hbm.at[pl.ds(0, tb), :], row_sem).wait()


def dispatch(hp, dest_blk, n_rows_out):
    n, half = hp.shape
    n_steps, per_step = dest_blk.shape
    tb = per_step // TOP_K
    return pl.pallas_call(
        functools.partial(_dispatch_kernel, tb=tb, n_steps=n_steps),
        grid=(n_steps,),
        in_specs=[pl.BlockSpec(memory_space=pl.ANY),
                  pl.BlockSpec((tb, half), lambda i: (i, 0))],
        out_specs=pl.BlockSpec(memory_space=pl.ANY),
        out_shape=jax.ShapeDtypeStruct((n_rows_out, half), jnp.uint32),
        scratch_shapes=[pltpu.SMEM((2, per_step), jnp.int32),
                        pltpu.SemaphoreType.DMA(()),
                        pltpu.SemaphoreType.DMA(())],
        compiler_params=_params(("arbitrary",), 2 * tb * half * 4),
        name="moe_dispatch",
    )(dest_blk, hp)


def _combine_kernel(dest_hbm, y_hbm, x_ref, gate_ref, gain_ref, *refs, tb, n_steps, final, pack):
    if final:
        o_ref, dest_smem, ybuf, dest_sem, row_sem = refs
        xo_ref = None
    else:
        xo_ref, o_ref, dest_smem, ybuf, dest_sem, row_sem = refs
    i = pl.program_id(0)
    n_rows = TOP_K * tb

    def dest_copy(step, slot):
        return pltpu.make_async_copy(dest_hbm.at[step], dest_smem.at[slot], dest_sem)

    def row_copy(slot, r):
        return pltpu.make_async_copy(y_hbm.at[pl.ds(dest_smem[slot, r], 1), :], ybuf.at[slot, pl.ds(r, 1), :],
                                     row_sem.at[slot])

    def wait_slot(slot):
        pltpu.make_async_copy(y_hbm.at[pl.ds(0, n_rows), :], ybuf.at[slot], row_sem.at[slot]).wait()

    @pl.when(i == 0)
    def _first():
        dest_copy(0, 0).start()
        dest_copy(0, 0).wait()

        def issue(r, c):
            row_copy(0, r).start()
            return c

        lax.fori_loop(0, n_rows, issue, 0, unroll=8)
        dest_copy(1, 1).start()

    dest_copy(0, 0).wait()
    for p in range(2):
        @pl.when(i % 2 == p)
        def _():
            for r in range(n_rows):
                row_copy(1 - p, r).start()
            wait_slot(p)
            g = gate_ref[...]
            acc = x_ref[...]
            for k in range(TOP_K):
                w = ybuf[p, k * tb:(k + 1) * tb, :]
                lo = lax.bitcast_convert_type(w << 16, f32)
                hi = lax.bitcast_convert_type(w & jnp.uint32(0xFFFF0000), f32)
                pieces = []
                for c0 in range(0, w.shape[1], pack):
                    pieces += [lo[:, c0:c0 + pack], hi[:, c0:c0 + pack]]
                acc = acc + g[:, k:k + 1] * jnp.concatenate(pieces, axis=1)
            if not final:
                xo_ref[...] = acc
            inv = lax.rsqrt(jnp.mean(acc * acc, axis=-1, keepdims=True) + EPS)
            o_ref[...] = (acc * inv * gain_ref[...]).astype(o_ref.dtype)

            @pl.when(i + 2 <= n_steps)
            def _():
                dest_copy(i + 2, p).start()

            @pl.when(i == n_steps - 1)
            def _():
                wait_slot(1 - p)


def combine(x2d, y, dest_blk, gates_t, gain2d, final):
    n, d = x2d.shape
    n_steps, per_step = dest_blk.shape
    tb = per_step // TOP_K
    pack = _down_tile(d) // 2
    dest_blk = jnp.concatenate([dest_blk, jnp.zeros((1, per_step), dest_blk.dtype)], axis=0)
    row_spec = pl.BlockSpec((tb, d), lambda i: (i, 0))
    if final:
        out_shape = jax.ShapeDtypeStruct((n, d), f32)
        out_specs = row_spec
    else:
        out_shape = [jax.ShapeDtypeStruct((n, d), f32), jax.ShapeDtypeStruct((n, d), bf16)]
        out_specs = [row_spec, row_spec]
    nbytes = 2 * TOP_K * tb * d * 2 + 8 * tb * d * 4
    return pl.pallas_call(
        functools.partial(_combine_kernel, tb=tb, n_steps=n_steps, final=final, pack=pack),
        grid=(n_steps,),
        in_specs=[pl.BlockSpec(memory_space=pl.ANY),
                  pl.BlockSpec(memory_space=pl.ANY),
                  row_spec,
                  pl.BlockSpec((tb, TOP_K), lambda i: (i, 0)),
                  pl.BlockSpec((1, d), lambda i: (0, 0))],
        out_specs=out_specs,
        out_shape=out_shape,
        scratch_shapes=[pltpu.SMEM((2, TOP_K * tb), jnp.int32),
                        pltpu.VMEM((2, TOP_K * tb, d // 2), jnp.uint32),
                        pltpu.SemaphoreType.DMA(()),
                        pltpu.SemaphoreType.DMA((2,))],
        compiler_params=_params(("arbitrary",), nbytes),
        name="moe_combine",
    )(dest_blk, y, x2d, gates_t, gain2d)


def _dispatch_plan(counts, idx, rank, tm, t_max, tb):
    n_exp = counts.shape[0]
    n = idx.shape[1]
    tiles_e = (counts + tm - 1) // tm
    tile_end = jnp.cumsum(tiles_e)
    tile_start = tile_end - tiles_e
    n_used = tile_end[-1]
    expert = jnp.arange(n_exp, dtype=jnp.int32)[:, None, None]
    dest = rank + jnp.sum(jnp.where(idx[None] == expert, (tile_start * tm)[:, None, None], 0), axis=0)
    tile = jnp.arange(t_max, dtype=jnp.int32)
    tile_c = jnp.minimum(tile, n_used - 1)
    tile_e = jnp.minimum(jnp.searchsorted(tile_end, tile_c, side="right"), n_exp - 1).astype(jnp.int32)
    rows = jnp.clip(counts[tile_e] - (tile_c - tile_start[tile_e]) * tm, 0, tm)
    tile_rows = jnp.where(tile < n_used, rows, 0).astype(jnp.int32)
    dest_blk = dest.reshape(TOP_K, n // tb, tb).transpose(1, 0, 2).reshape(n // tb, TOP_K * tb)
    return dest_blk, tile_e, tile_rows, n_used.reshape(1).astype(jnp.int32)


def kernel(x, norm_mix, w_in, attn_rel_bias, attn_out_gain, conv_w, conv_b, lru_w_a, lru_b_a, lru_w_i, lru_b_i,
           lru_lambda, rec_out_gain, w_out, norm_ffn, router_w, router_b, expert_w_gate_up, expert_b_gate_up,
           expert_w_down, expert_b_down, norm_final):
    batch, seq, d = x.shape
    n = batch * seq
    depth = norm_mix.shape[0]
    attn_width = attn_out_gain.shape[1]
    lru_width = rec_out_gain.shape[1]
    n_exp = router_w.shape[-1]
    sub = min(512, n)
    tm = sub * min(3, n // sub)
    t_max = (n * TOP_K) // tm + n_exp
    tb = min(128, n)

    x2d = x.reshape(n, d)
    h = rmsnorm_bf16(x2d, norm_mix, 0)
    out = None
    for l in range(depth):
        qkv = matmul_ws([h], w_in, l, [0], 0, 3 * attn_width, bf16, name="in_proj_qkv")
        rg = matmul_ws([h], w_in, l, [0], 3 * attn_width, 2 * lru_width, f32, name="in_proj_rec")
        attn = attention(qkv, attn_rel_bias[l], attn_out_gain, l, batch, seq, attn_width)
        rec = rglru_block(rg, conv_w, conv_b, lru_w_a, lru_b_a, lru_w_i, lru_b_i, lru_lambda, rec_out_gain,
                          l, batch, seq)
        assert attn_width == lru_width
        x2d = matmul_ws([attn, rec], w_out, l, [0, 1], 0, d, f32, residual=x2d, name="out_proj")
        hp, idx, gates, rank, cnt = router(x2d, norm_ffn, router_w, router_b, l)
        dest_blk, tile_e, tile_rows, n_used = _dispatch_plan(cnt[:, 0], idx, rank, tm, t_max, tb)
        xs_sorted = dispatch(hp, dest_blk, t_max * tm)
        y = moe_experts(xs_sorted, tile_e, tile_rows, n_used, expert_w_gate_up, expert_b_gate_up,
                        expert_w_down, expert_b_down, l, tm, sub)
        final = l == depth - 1
        gain = norm_final.reshape(1, d) if final else norm_mix[l + 1].reshape(1, d)
        res = combine(x2d, y, dest_blk, gates.T, gain, final)
        if final:
            out = res
        else:
            x2d, h = res
    return out.reshape(batch, seq, d)
```

```python
import functools
import math

import jax
import jax.numpy as jnp
from jax import lax
from jax.experimental import pallas as pl
from jax.experimental.pallas import tpu as pltpu

CHUNK = 64
LEFT_CHUNKS = 8
ATTN_HEAD_DIM = 128
REL_CLIP = 128
LRU_C = 8.0
TOP_K = 4
SWIGLU_LIMIT = 7.0
SWIGLU_ALPHA = 1.702
EPS = 1e-6

V7X_LANES = 128
V7X_SUBLANES = 8
V7X_VMEM_BYTES = 64 * 1024 * 1024

MASK_VALUE = -1e30

f32 = jnp.float32
bf16 = jnp.bfloat16


def _vmem_limit(nbytes):
    return int(min(nbytes + 12 * 1024 * 1024, V7X_VMEM_BYTES - 6 * 1024 * 1024))


def _params(sem, nbytes):
    return pltpu.CompilerParams(dimension_semantics=sem, vmem_limit_bytes=_vmem_limit(nbytes))


def _rmsnorm_kernel(x_ref, g_ref, o_ref):
    x = x_ref[...]
    inv = lax.rsqrt(jnp.mean(x * x, axis=-1, keepdims=True) + EPS)
    o_ref[...] = (x * inv * g_ref[...]).astype(o_ref.dtype)


def rmsnorm_bf16(x2d, gain_l, layer):
    n, d = x2d.shape
    tb = min(256, n)
    return pl.pallas_call(
        _rmsnorm_kernel,
        grid=(n // tb,),
        in_specs=[pl.BlockSpec((tb, d), lambda i: (i, 0)),
                  pl.BlockSpec((None, 1, d), lambda i: (layer, 0, 0))],
        out_specs=pl.BlockSpec((tb, d), lambda i: (i, 0)),
        out_shape=jax.ShapeDtypeStruct((n, d), bf16),
        compiler_params=_params(("arbitrary",), 2 * tb * d * 6),
        name="rmsnorm_bf16",
    )(x2d, gain_l.reshape(gain_l.shape[0], 1, d))


def _matmul_kernel(*refs, n_pairs, has_res):
    a_refs = refs[:n_pairs]
    w_refs = refs[n_pairs:2 * n_pairs]
    pos = 2 * n_pairs
    res_ref = refs[pos] if has_res else None
    pos += int(has_res)
    o_ref = refs[pos]
    wbf_refs = refs[pos + 1:pos + 1 + n_pairs]

    @pl.when(pl.program_id(1) == 0)
    def _cast_weights():
        for w_ref, wbf_ref in zip(w_refs, wbf_refs):
            wbf_ref[...] = w_ref[...].astype(bf16)

    acc = jnp.dot(a_refs[0][...], wbf_refs[0][...], preferred_element_type=f32)
    for a_ref, wbf_ref in zip(a_refs[1:], wbf_refs[1:]):
        acc += jnp.dot(a_ref[...], wbf_ref[...], preferred_element_type=f32)
    if has_res:
        acc += res_ref[...]
    o_ref[...] = acc.astype(o_ref.dtype)


def matmul_ws(a_list, w_stack, layer, k_blocks, col0, n_cols, out_dtype, residual=None, name="matmul_ws"):
    m = a_list[0].shape[0]
    tm = min(512, m)
    tn = min(1024, n_cols)
    assert m % tm == 0 and n_cols % tn == 0 and col0 % tn == 0
    cb0 = col0 // tn
    n_pairs = len(a_list)
    in_specs, scratch = [], []
    nbytes = 2 * tm * tn * 4 * (2 if residual is not None else 1)
    for a in a_list:
        kp = a.shape[1]
        in_specs.append(pl.BlockSpec((tm, kp), lambda j, i: (i, 0)))
        nbytes += 2 * tm * kp * 2
    for a, kb in zip(a_list, k_blocks):
        kp = a.shape[1]
        in_specs.append(pl.BlockSpec((None, kp, tn), functools.partial(lambda j, i, kb: (layer, kb, j + cb0), kb=kb)))
        scratch.append(pltpu.VMEM((kp, tn), bf16))
        nbytes += 2 * kp * tn * 4 + kp * tn * 2
    args = list(a_list) + [w_stack] * n_pairs
    if residual is not None:
        in_specs.append(pl.BlockSpec((tm, tn), lambda j, i: (i, j)))
        args.append(residual)
    return pl.pallas_call(
        functools.partial(_matmul_kernel, n_pairs=n_pairs, has_res=residual is not None),
        grid=(n_cols // tn, m // tm),
        in_specs=in_specs,
        out_specs=pl.BlockSpec((tm, tn), lambda j, i: (i, j)),
        out_shape=jax.ShapeDtypeStruct((m, n_cols), out_dtype),
        scratch_shapes=scratch,
        compiler_params=_params(("arbitrary", "arbitrary"), nbytes),
        name=name,
    )(*args)


def _attn_kernel(q_ref, *refs, n_kb, hb, tq, n_hg):
    k_refs = refs[:n_kb]
    v_refs = refs[n_kb:2 * n_kb]
    row0_ref, gain_ref, o_ref, raw_ref, bias_ref = refs[2 * n_kb:]
    i = pl.program_id(1)
    hg = pl.program_id(2)
    dh = ATTN_HEAD_DIM
    scale = 1.0 / math.sqrt(dh)
    n_keys = n_kb * tq

    @pl.when((pl.program_id(0) == 0) & (i == 0) & (hg == 0))
    def _build_bias():
        width = row0_ref.shape[-1]
        qc = ((n_kb - 1) * tq + lax.broadcasted_iota(jnp.int32, (tq, n_keys), 0)) // CHUNK
        kc = lax.broadcasted_iota(jnp.int32, (tq, n_keys), 1) // CHUNK
        valid = (kc <= qc) & (kc >= qc - LEFT_CHUNKS)
        for h in range(bias_ref.shape[0]):
            rows = pltpu.roll(jnp.broadcast_to(row0_ref[h], (tq, width)), 0, 1, stride=1, stride_axis=0)
            bias_ref[h] = jnp.where(valid, rows[:, :n_keys] * (1.0 / scale), MASK_VALUE)

    outs = []
    for hh in range(hb):
        sl = slice(hh * dh, (hh + 1) * dh)
        q = q_ref[:, sl]
        bias = bias_ref[hg * hb + hh]
        parts = []
        for j in range(n_kb - 1, -1, -1):
            s = lax.dot_general(q, k_refs[j][:, sl], (((1,), (1,)), ((), ())), preferred_element_type=f32)
            c0 = (n_kb - 1 - j) * tq
            s = s + bias[:, c0:c0 + tq]
            if j > 0:
                s = s + jnp.where(i >= j, 0.0, MASK_VALUE)
            parts.append(s)
        m = parts[0].max(axis=-1, keepdims=True)
        for s in parts[1:]:
            m = jnp.maximum(m, s.max(axis=-1, keepdims=True))
        l = jnp.zeros_like(m)
        acc = jnp.zeros((tq, dh), f32)
        for idx, s in enumerate(parts):
            j = n_kb - 1 - idx
            p = jnp.exp2((s - m) * (scale * math.log2(math.e)))
            l = l + p.sum(axis=-1, keepdims=True)
            acc = acc + jnp.dot(p.astype(bf16), v_refs[j][:, sl], preferred_element_type=f32)
        outs.append(acc / l)
    raw_ref[hg] = jnp.concatenate(outs, axis=-1)

    @pl.when(hg == n_hg - 1)
    def _finish():
        ss = jnp.zeros((tq, 1), f32)
        for g in range(n_hg):
            r = raw_ref[g]
            ss = ss + jnp.sum(r * r, axis=-1, keepdims=True)
        inv = lax.rsqrt(ss / (n_hg * hb * dh) + EPS)
        for g in range(n_hg):
            w = hb * dh
            o_ref[:, g * w:(g + 1) * w] = (raw_ref[g] * inv * gain_ref[:, g * w:(g + 1) * w]).astype(o_ref.dtype)


def _attn_bias_row0(rel_bias_l, tq, n_kb):
    n_heads = rel_bias_l.shape[0]
    rb = rel_bias_l.astype(f32)
    far = (n_kb - 1) * tq - REL_CLIP
    near = n_kb * tq - far - (2 * REL_CLIP + 1)
    assert far >= 0 and near >= 0
    fill = lambda col, width: jnp.broadcast_to(rb[:, col:col + 1], (n_heads, width))
    row0 = jnp.concatenate([fill(2 * REL_CLIP, far), rb[:, ::-1], fill(0, near), fill(2 * REL_CLIP, tq)], axis=1)
    return row0.reshape(n_heads, 1, (n_kb + 1) * tq)


def attention(qkv, rel_bias_l, gain_l, layer, batch, seq, attn_width):
    n = qkv.shape[0]
    n_heads = attn_width // ATTN_HEAD_DIM
    tq = min(256, seq)
    band = LEFT_CHUNKS * CHUNK
    assert band % tq == 0 and seq % tq == 0 and tq % CHUNK == 0
    n_kb = band // tq + 1
    nq = seq // tq
    hb = min(16, n_heads)
    n_hg = n_heads // hb
    w = hb * ATTN_HEAD_DIM
    row0 = _attn_bias_row0(rel_bias_l, tq, n_kb)

    def kv_spec(j, which):
        return pl.BlockSpec((tq, w), lambda b, i, g: (b * nq + jnp.maximum(i - j, 0), which * n_hg + g))

    in_specs = [pl.BlockSpec((tq, w), lambda b, i, g: (b * nq + i, g))]
    in_specs += [kv_spec(j, 1) for j in range(n_kb)] + [kv_spec(j, 2) for j in range(n_kb)]
    in_specs += [pl.BlockSpec(row0.shape, lambda b, i, g: (0, 0, 0)),
                 pl.BlockSpec((None, 1, attn_width), lambda b, i, g: (layer, 0, 0))]
    nbytes = 2 * (2 * n_kb + 1) * tq * w * 2 + n_heads * tq * n_kb * tq * 4 + 3 * tq * attn_width * 4
    return pl.pallas_call(
        functools.partial(_attn_kernel, n_kb=n_kb, hb=hb, tq=tq, n_hg=n_hg),
        grid=(batch, nq, n_hg),
        in_specs=in_specs,
        out_specs=pl.BlockSpec((tq, attn_width), lambda b, i, g: (b * nq + i, 0)),
        out_shape=jax.ShapeDtypeStruct((n, attn_width), bf16),
        scratch_shapes=[pltpu.VMEM((n_hg, tq, w), f32), pltpu.VMEM((n_heads, tq, n_kb * tq), f32)],
        compiler_params=_params(("arbitrary", "arbitrary", "arbitrary"), nbytes),
        name="band_attention",
    )(*([qkv] * (2 * n_kb + 1)), row0, gain_l.reshape(gain_l.shape[0], 1, attn_width))


def _sigmoid(x):
    return 0.5 * jnp.tanh(0.5 * x) + 0.5


def _lru_kernel(xr_ref, xg_ref, cw_ref, cb_ref, wa_ref, ba_ref, wi_ref, bi_ref, lam_ref, gain_ref,
                o_ref, xe_ref, a_ref, b_ref, h_ref, carry_ref, *, t, n_blocks, conv_width):
    i = pl.program_id(1)
    hdr = V7X_SUBLANES
    bw = wa_ref.shape[-1]

    @pl.when(i == 0)
    def _reset():
        xe_ref[0:hdr, :] = jnp.zeros((hdr, xe_ref.shape[1]), f32)
        carry_ref[...] = jnp.zeros_like(carry_ref)

    x = xr_ref[...]
    xe_ref[hdr:hdr + t, :] = x
    u = cb_ref[...] + cw_ref[conv_width - 1:conv_width, :] * x
    for j in range(conv_width - 1):
        u = u + cw_ref[j:j + 1, :] * xe_ref[pl.ds(hdr - (conv_width - 1) + j, t), :]
    xe_ref[0:hdr, :] = x[t - hdr:t, :]

    lam = lam_ref[...]
    nsp = -LRU_C * (jnp.maximum(-lam, 0.0) + jnp.log1p(jnp.exp(-jnp.abs(lam))))
    for nb in range(n_blocks):
        sl = slice(nb * bw, (nb + 1) * bw)
        ub = u[:, sl]
        ubb = ub.astype(bf16)
        r = _sigmoid(jnp.dot(ubb, wa_ref[nb].astype(bf16), preferred_element_type=f32) + ba_ref[:, sl])
        ig = _sigmoid(jnp.dot(ubb, wi_ref[nb].astype(bf16), preferred_element_type=f32) + bi_ref[:, sl])
        log_a = r * nsp[:, sl]
        a = jnp.exp(log_a)
        a_ref[:, sl] = a
        b_ref[:, sl] = jnp.sqrt(1.0 - a * a) * (ig * ub)

    a = a_ref[...]
    b = b_ref[...]
    row = lax.broadcasted_iota(jnp.int32, a.shape, 0) % hdr
    s = 1
    while s < hdr:
        keep = row >= s
        a_sh = jnp.where(keep, pltpu.roll(a, s, 0), 1.0)
        b_sh = jnp.where(keep, pltpu.roll(b, s, 0), 0.0)
        b = a * b_sh + b
        a = a * a_sh
        s *= 2
    a_ref[...] = a
    b_ref[...] = b

    def tile_step(j, carry):
        r0 = pl.multiple_of(j * hdr, hdr)
        h = a_ref[pl.ds(r0, hdr), :] * carry + b_ref[pl.ds(r0, hdr), :]
        h_ref[pl.ds(r0, hdr), :] = h
        return h[hdr - 1:hdr, :]

    carry_ref[...] = lax.fori_loop(0, t // hdr, tile_step, carry_ref[...])

    y = h_ref[...] * jax.nn.gelu(xg_ref[...], approximate=True)
    inv = lax.rsqrt(jnp.mean(y * y, axis=-1, keepdims=True) + EPS)
    o_ref[...] = (y * inv * gain_ref[...]).astype(o_ref.dtype)


def rglru_block(rg, conv_w, conv_b, w_a, b_a, w_i, b_i, lam, gain, layer, batch, seq):
    n = rg.shape[0]
    n_layers, conv_width, wr = conv_w.shape
    n_blocks = w_a.shape[1]
    bw = w_a.shape[2]
    t = min(512, seq)
    ns = seq // t
    row3 = lambda v: v.reshape(n_layers, 1, wr)
    vec_spec = pl.BlockSpec((None, 1, wr), lambda b, i: (layer, 0, 0))
    wspec = pl.BlockSpec((None, n_blocks, bw, bw), lambda b, i: (layer, 0, 0, 0))
    nbytes = 2 * 2 * t * wr * 4 + 2 * t * wr * 2 + 4 * (t + 8) * wr * 4 + 4 * n_blocks * bw * bw * 4 + 8 * t * wr * 4
    return pl.pallas_call(
        functools.partial(_lru_kernel, t=t, n_blocks=n_blocks, conv_width=conv_width),
        grid=(batch, ns),
        in_specs=[pl.BlockSpec((t, wr), lambda b, i: (b * ns + i, 0)),
                  pl.BlockSpec((t, wr), lambda b, i: (b * ns + i, 1)),
                  pl.BlockSpec((None, conv_width, wr), lambda b, i: (layer, 0, 0)),
                  vec_spec, wspec, vec_spec, wspec, vec_spec, vec_spec, vec_spec],
        out_specs=pl.BlockSpec((t, wr), lambda b, i: (b * ns + i, 0)),
        out_shape=jax.ShapeDtypeStruct((n, wr), bf16),
        scratch_shapes=[pltpu.VMEM((t + V7X_SUBLANES, wr), f32), pltpu.VMEM((t, wr), f32),
                        pltpu.VMEM((t, wr), f32), pltpu.VMEM((t, wr), f32), pltpu.VMEM((1, wr), f32)],
        compiler_params=_params(("arbitrary", "arbitrary"), nbytes),
        name="conv_rglru",
    )(rg, rg, conv_w, row3(conv_b), w_a, row3(b_a), w_i, row3(b_i), row3(lam), row3(gain))


def _router_kernel(x_ref, g_ref, whi_ref, wlo_ref, rb_ref, hp_ref, idx_ref, gate_ref, rank_ref, cnt_ref, carry_ref,
                   *, tb, n_exp):
    step = pl.program_id(0)

    @pl.when(step == 0)
    def _reset():
        carry_ref[...] = jnp.zeros_like(carry_ref)

    x = x_ref[...]
    inv = lax.rsqrt(jnp.mean(x * x, axis=-1, keepdims=True) + EPS)
    h = x * inv * g_ref[...]
    half = h.shape[1] // 2
    h_hi = h.astype(bf16)
    h_hi32 = h_hi.astype(f32)
    bits = lax.bitcast_convert_type(h_hi32, jnp.uint32)
    hp_ref[...] = (bits[:, :half] >> 16) | bits[:, half:]

    h_lo = (h - h_hi32).astype(bf16)
    w_hi = whi_ref[...]
    acc = (jnp.dot(h_hi, w_hi, preferred_element_type=f32) + jnp.dot(h_lo, w_hi, preferred_element_type=f32)
           + jnp.dot(h_hi, wlo_ref[...], preferred_element_type=f32))
    logits = acc.T[:n_exp] + rb_ref[...]
    e_iota = lax.broadcasted_iota(jnp.int32, (n_exp, tb), 0)
    tri = (lax.broadcasted_iota(jnp.int32, (tb, tb), 0) < lax.broadcasted_iota(jnp.int32, (tb, tb), 1)).astype(bf16)
    base = carry_ref[...]
    vals, idxs, ranks = [], [], []
    l = logits
    for _ in range(TOP_K):
        m = l.max(axis=0, keepdims=True)
        idx = jnp.min(jnp.where(l == m, e_iota, n_exp), axis=0, keepdims=True)
        onehot = e_iota == idx
        ohf = onehot.astype(f32)
        before = jnp.dot(ohf.astype(bf16), tri, preferred_element_type=f32)
        ranks.append(jnp.sum(jnp.where(onehot, before + base, 0.0), axis=0, keepdims=True))
        base = base + jnp.sum(ohf, axis=1, keepdims=True)
        vals.append(m)
        idxs.append(idx)
        l = jnp.where(onehot, -jnp.inf, l)
    carry_ref[...] = base
    exps = [jnp.exp(v - vals[0]) for v in vals]
    denom = exps[0]
    for e in exps[1:]:
        denom = denom + e
    idx_ref[...] = jnp.concatenate(idxs, axis=0)
    gate_ref[...] = jnp.concatenate([e / denom for e in exps], axis=0)
    rank_ref[...] = jnp.concatenate(ranks, axis=0).astype(jnp.int32)
    cnt_ref[...] = jnp.broadcast_to(base, cnt_ref.shape).astype(jnp.int32)


def router(x2d, gain_l, router_w_l, router_b_l, layer):
    n, d = x2d.shape
    n_exp = router_w_l.shape[-1]
    tb = min(512, n)
    e_pad = -(-n_exp // V7X_LANES) * V7X_LANES
    rw = jnp.pad(router_w_l[layer].astype(f32), ((0, 0), (0, e_pad - n_exp)))
    w_hi = rw.astype(bf16)
    w_lo = (rw - w_hi.astype(f32)).astype(bf16)
    rb = router_b_l[layer].reshape(n_exp, 1)
    slot_spec = pl.BlockSpec((TOP_K, tb), lambda i: (0, i))
    nbytes = 2 * tb * d * 4 + 2 * tb * d * 2 + 4 * e_pad * d * 2 + 6 * tb * d * 4
    return pl.pallas_call(
        functools.partial(_router_kernel, tb=tb, n_exp=n_exp),
        grid=(n // tb,),
        in_specs=[pl.BlockSpec((tb, d), lambda i: (i, 0)),
                  pl.BlockSpec((None, 1, d), lambda i: (layer, 0, 0)),
                  pl.BlockSpec((d, e_pad), lambda i: (0, 0)),
                  pl.BlockSpec((d, e_pad), lambda i: (0, 0)),
                  pl.BlockSpec((n_exp, 1), lambda i: (0, 0))],
        out_specs=[pl.BlockSpec((tb, d // 2), lambda i: (i, 0)), slot_spec, slot_spec, slot_spec,
                   pl.BlockSpec((n_exp, V7X_LANES), lambda i: (0, 0))],
        out_shape=[jax.ShapeDtypeStruct((n, d // 2), jnp.uint32),
                   jax.ShapeDtypeStruct((TOP_K, n), jnp.int32),
                   jax.ShapeDtypeStruct((TOP_K, n), f32),
                   jax.ShapeDtypeStruct((TOP_K, n), jnp.int32),
                   jax.ShapeDtypeStruct((n_exp, V7X_LANES), jnp.int32)],
        scratch_shapes=[pltpu.VMEM((n_exp, 1), f32)],
        compiler_params=_params(("arbitrary",), nbytes),
        name="router_topk",
    )(x2d, gain_l.reshape(gain_l.shape[0], 1, d), w_hi, w_lo, rb)


def _moe_kernel(tile_e_ref, tile_rows_ref, n_used_ref,
                xs_hbm, wgu_ref, bgu_ref, wdn_ref, bdn_ref,
                y_ref,
                stage_ref, xs_ref, act_ref, wgu_bf, wdn_bf, raw_a, raw_b, row_sem,
                *, tm, sub, n_gu, n_dn, tgu):
    raw_refs = (raw_a, raw_b)
    t = pl.program_id(0)
    s = pl.program_id(1)
    n_used = n_used_ref[0]
    half = stage_ref.shape[-1]
    kh_rows = wgu_bf.shape[0]
    th = tgu // 2
    n_sb = tm // sub
    n_gu_steps = 2 * n_gu
    rows = tile_rows_ref[t]
    ns = (rows + sub - 1) // sub

    piece = stage_ref.shape[1]
    n_pieces_max = tm // piece

    def n_pieces(tile):
        return (tile_rows_ref[tile] + sub - 1) // sub * (sub // piece)

    def piece_copy(tile, q):
        return pltpu.make_async_copy(xs_hbm.at[pl.ds(tile * tm + q * piece, piece), :], stage_ref.at[q % 2],
                                     row_sem.at[q % 2])

    def start_tile(tile):
        for q in range(2):
            @pl.when(q < n_pieces(tile))
            def _():
                piece_copy(tile, q).start()

    @pl.when(s == 0)
    def _stage_rows():
        @pl.when(t == 0)
        def _first():
            start_tile(0)
            for ref in raw_refs:
                ref[...] = jnp.zeros_like(ref)

        for q in range(n_pieces_max):
            @pl.when(q < n_pieces(t))
            def _unpack():
                piece_copy(t, q).wait()
                live = lax.broadcasted_iota(jnp.int32, (piece, half), 0) < rows - q * piece
                w = jnp.where(live, stage_ref[q % 2], jnp.uint32(0))
                xs_ref[q * piece:(q + 1) * piece, 0:half] = lax.bitcast_convert_type(w << 16, f32).astype(bf16)
                xs_ref[q * piece:(q + 1) * piece, half:2 * half] = lax.bitcast_convert_type(
                    w & jnp.uint32(0xFFFF0000), f32).astype(bf16)
                if q + 2 < n_pieces_max:
                    @pl.when(q + 2 < n_pieces(t))
                    def _():
                        piece_copy(t, q + 2).start()

    @pl.when((s == 1) & (t + 1 < n_used))
    def _prefetch_rows():
        start_tile(t + 1)

    def activate(src_ref, rb_src, s_dst):
        sel = (lax.broadcasted_iota(jnp.int32, (tgu, th), 0) == 2 * lax.broadcasted_iota(jnp.int32, (tgu, th), 1)).astype(bf16)
        r_src = pl.multiple_of(rb_src * sub, sub)
        gu = src_ref[pl.ds(r_src, sub), :]
        glu = jnp.minimum(gu, SWIGLU_LIMIT)
        glu = glu * _sigmoid(SWIGLU_ALPHA * glu)
        lin = jnp.clip(gu, -SWIGLU_LIMIT, SWIGLU_LIMIT) + 1.0
        prod = glu * pltpu.roll(lin, tgu - 1, 1)
        act_ref[s_dst, pl.ds(r_src, sub), :] = jnp.dot(prod.astype(bf16), sel, preferred_element_type=f32).astype(bf16)

    n_tile = s // 2
    p_cur = n_tile % 2

    @pl.when((s < n_gu_steps) & (s % 2 == 0))
    def _gate_up_first_half():
        wgu_bf[...] = wgu_ref[...].astype(bf16)

        for p in range(2):
            @pl.when(p_cur == p)
            def _():
                def body(rb, c):
                    r0 = pl.multiple_of(rb * sub, sub)
                    raw_refs[p][pl.ds(r0, sub), :] = jnp.dot(xs_ref[pl.ds(r0, sub), 0:kh_rows], wgu_bf[...],
                                                             preferred_element_type=f32) + bgu_ref[...]
                    activate(raw_refs[1 - p], rb, jnp.where(n_tile == 0, n_gu, n_tile - 1))
                    return c

                lax.fori_loop(0, ns, body, 0)

    @pl.when((s < n_gu_steps) & (s % 2 == 1))
    def _gate_up_second_half():
        wgu_bf[...] = wgu_ref[...].astype(bf16)
        for p in range(2):
            @pl.when(p_cur == p)
            def _():
                def body(rb, c):
                    r0 = pl.multiple_of(rb * sub, sub)
                    raw_refs[p][pl.ds(r0, sub), :] += jnp.dot(xs_ref[pl.ds(r0, sub), kh_rows:2 * kh_rows], wgu_bf[...],
                                                              preferred_element_type=f32)
                    return c

                lax.fori_loop(0, ns, body, 0)

    @pl.when(s == n_gu_steps)
    def _last_activation():
        def body(rb, c):
            activate(raw_refs[(n_gu - 1) % 2], rb, n_gu - 1)
            return c

        lax.fori_loop(0, ns, body, 0)

    @pl.when(s >= n_gu_steps)
    def _down():
        wdn_bf[...] = wdn_ref[...].astype(bf16)

        def body(rb, c):
            r0 = pl.multiple_of(rb * sub, sub)
            a = jnp.concatenate([act_ref[g, pl.ds(r0, sub), :] for g in range(n_gu)], axis=1)
            y = jnp.dot(a, wdn_bf[...], preferred_element_type=f32) + bdn_ref[...]
            hw = y.shape[1] // 2
            bits = lax.bitcast_convert_type(y.astype(bf16).astype(f32), jnp.uint32)
            y_ref[pl.ds(r0, sub), :] = (bits[:, :hw] >> 16) | bits[:, hw:]
            return c

        lax.fori_loop(0, ns, body, 0)

        def clear(rb, c):
            r0 = pl.multiple_of(rb * sub, sub)
            y_ref[pl.ds(r0, sub), :] = jnp.zeros((sub, y_ref.shape[1]), jnp.uint32)
            return c

        lax.fori_loop(ns, n_sb, clear, 0)


def _down_tile(d):
    return min(1024, d)


def moe_experts(xs_sorted, tile_e, tile_rows, n_used, w_gu, b_gu, w_dn, b_dn, layer, tm, sub):
    half = xs_sorted.shape[1]
    d = 2 * half
    n_layers, n_exp, _, two_f = w_gu.shape
    ff = two_f // 2
    t_max = xs_sorted.shape[0] // tm
    tgu = min(512, two_f)
    tdn = _down_tile(d)
    n_gu, n_dn = two_f // tgu, d // tdn

    n_gu_steps = 2 * n_gu

    def gu_idx(t, s, te, tr, nu):
        step = jnp.minimum(s, n_gu_steps - 1)
        return (layer, te[t], step % 2, step // 2)

    def bgu_idx(t, s, te, tr, nu):
        return (layer, te[t], 0, jnp.minimum(s, n_gu_steps - 1) // 2)

    def dn_idx(t, s, te, tr, nu):
        return (layer, te[t], 0, jnp.maximum(s - n_gu_steps, 0))

    def y_idx(t, s, te, tr, nu):
        return (t, jnp.maximum(s - n_gu_steps, 0))

    nbytes = (2 * sub * half * 4 + tm * d * 2 + (n_gu + 1) * tm * tgu + 2 * (d // 2) * tgu * 4 + (d // 2) * tgu * 2
              + 2 * ff * tdn * 4 + ff * tdn * 2 + 2 * tm * tgu * 4 + 2 * tm * tdn * 4)
    grid_spec = pltpu.PrefetchScalarGridSpec(
        num_scalar_prefetch=3,
        grid=(n_used[0], n_gu_steps + n_dn),
        in_specs=[pl.BlockSpec(memory_space=pl.ANY),
                  pl.BlockSpec((None, None, d // 2, tgu), gu_idx),
                  pl.BlockSpec((None, None, 1, tgu), bgu_idx),
                  pl.BlockSpec((None, None, ff, tdn), dn_idx),
                  pl.BlockSpec((None, None, 1, tdn), dn_idx)],
        out_specs=pl.BlockSpec((tm, tdn // 2), y_idx),
        scratch_shapes=[pltpu.VMEM((2, max(sub // 2, V7X_SUBLANES), half), jnp.uint32),
                        pltpu.VMEM((tm, d), bf16),
                        pltpu.VMEM((n_gu + 1, tm, tgu // 2), bf16),
                        pltpu.VMEM((d // 2, tgu), bf16),
                        pltpu.VMEM((ff, tdn), bf16),
                        pltpu.VMEM((tm, tgu), f32),
                        pltpu.VMEM((tm, tgu), f32),
                        pltpu.SemaphoreType.DMA((2,))],
    )
    return pl.pallas_call(
        functools.partial(_moe_kernel, tm=tm, sub=sub, n_gu=n_gu, n_dn=n_dn, tgu=tgu),
        grid_spec=grid_spec,
        out_shape=jax.ShapeDtypeStruct((t_max * tm, d // 2), jnp.uint32),
        compiler_params=_params(("arbitrary", "arbitrary"), nbytes),
        name="moe_experts",
    )(tile_e, tile_rows, n_used, xs_sorted, w_gu,
      b_gu.reshape(n_layers, n_exp, 1, two_f), w_dn, b_dn.reshape(n_layers, n_exp, 1, d))


def _dispatch_kernel(dest_hbm, hp_ref, xs_hbm, dest_smem, dest_sem, row_sem, *, tb, n_steps):
    i = pl.program_id(0)
    slot = i % 2

    def dest_copy(step):
        return pltpu.make_async_copy(dest_hbm.at[step], dest_smem.at[step % 2], dest_sem)

    def row_copy(r, row):
        return pltpu.make_async_copy(hp_ref.at[pl.ds(r, 1), :], xs_hbm.at[pl.ds(row, 1), :], row_sem)

    @pl.when(i == 0)
    def _first():
        dest_copy(0).start()

    dest_copy(i).wait()

    @pl.when(i + 1 < n_steps)
    def _next():
        dest_copy(i + 1).start()

    def issue(r, c):
        for k in range(TOP_K):
            row_copy(r, dest_smem[slot, k * tb + r]).start()
        return c

    lax.fori_loop(0, tb, issue, 0, unroll=2)
    for _ in range(TOP_K):
        pltpu.make_async_copy(hp_ref, xs_hbm.at[pl.ds(0, tb), :], row_sem).wait()


def dispatch(hp, dest_blk, n_rows_out):
    n, half = hp.shape
    n_steps, per_step = dest_blk.shape
    tb = per_step // TOP_K
    return pl.pallas_call(
        functools.partial(_dispatch_kernel, tb=tb, n_steps=n_steps),
        grid=(n_steps,),
        in_specs=[pl.BlockSpec(memory_space=pl.ANY),
                  pl.BlockSpec((tb, half), lambda i: (i, 0))],
        out_specs=pl.BlockSpec(memory_space=pl.ANY),
        out_shape=jax.ShapeDtypeStruct((n_rows_out, half), jnp.uint32),
        scratch_shapes=[pltpu.SMEM((2, per_step), jnp.int32),
                        pltpu.SemaphoreType.DMA(()),
                        pltpu.SemaphoreType.DMA(())],
        compiler_params=_params(("arbitrary",), 2 * tb * half * 4),
        name="moe_dispatch",
    )(dest_blk, hp)


def _combine_kernel(dest_hbm, y_hbm, x_ref, gate_ref, gain_ref, *refs, tb, n_steps, final, pack):
    if final:
        o_ref, dest_smem, ybuf, dest_sem, row_sem = refs
        xo_ref = None
    else:
        xo_ref, o_ref, dest_smem, ybuf, dest_sem, row_sem = refs
    i = pl.program_id(0)
    n_rows = TOP_K * tb

    def dest_copy(step, slot):
        return pltpu.make_async_copy(dest_hbm.at[step], dest_smem.at[slot], dest_sem)

    def row_copy(slot, r):
        return pltpu.make_async_copy(y_hbm.at[pl.ds(dest_smem[slot, r], 1), :], ybuf.at[slot, pl.ds(r, 1), :],
                                     row_sem.at[slot])

    def wait_slot(slot):
        pltpu.make_async_copy(y_hbm.at[pl.ds(0, n_rows), :], ybuf.at[slot], row_sem.at[slot]).wait()

    @pl.when(i == 0)
    def _first():
        dest_copy(0, 0).start()
        dest_copy(0, 0).wait()

        def issue(r, c):
            row_copy(0, r).start()
            return c

        lax.fori_loop(0, n_rows, issue, 0, unroll=8)
        dest_copy(1, 1).start()

    dest_copy(0, 0).wait()
    for p in range(2):
        @pl.when(i % 2 == p)
        def _():
            for r in range(n_rows):
                row_copy(1 - p, r).start()
            wait_slot(p)
            g = gate_ref[...]
            acc = x_ref[...]
            for k in range(TOP_K):
                w = ybuf[p, k * tb:(k + 1) * tb, :]
                lo = lax.bitcast_convert_type(w << 16, f32)
                hi = lax.bitcast_convert_type(w & jnp.uint32(0xFFFF0000), f32)
                pieces = []
                for c0 in range(0, w.shape[1], pack):
                    pieces += [lo[:, c0:c0 + pack], hi[:, c0:c0 + pack]]
                acc = acc + g[:, k:k + 1] * jnp.concatenate(pieces, axis=1)
            if not final:
                xo_ref[...] = acc
            inv = lax.rsqrt(jnp.mean(acc * acc, axis=-1, keepdims=True) + EPS)
            o_ref[...] = (acc * inv * gain_ref[...]).astype(o_ref.dtype)

            @pl.when(i + 2 <= n_steps)
            def _():
                dest_copy(i + 2, p).start()

            @pl.when(i == n_steps - 1)
            def _():
                wait_slot(1 - p)


def combine(x2d, y, dest_blk, gates_t, gain2d, final):
    n, d = x2d.shape
    n_steps, per_step = dest_blk.shape
    tb = per_step // TOP_K
    pack = _down_tile(d) // 2
    dest_blk = jnp.concatenate([dest_blk, jnp.zeros((1, per_step), dest_blk.dtype)], axis=0)
    row_spec = pl.BlockSpec((tb, d), lambda i: (i, 0))
    if final:
        out_shape = jax.ShapeDtypeStruct((n, d), f32)
        out_specs = row_spec
    else:
        out_shape = [jax.ShapeDtypeStruct((n, d), f32), jax.ShapeDtypeStruct((n, d), bf16)]
        out_specs = [row_spec, row_spec]
    nbytes = 2 * TOP_K * tb * d * 2 + 8 * tb * d * 4
    return pl.pallas_call(
        functools.partial(_combine_kernel, tb=tb, n_steps=n_steps, final=final, pack=pack),
        grid=(n_steps,),
        in_specs=[pl.BlockSpec(memory_space=pl.ANY),
                  pl.BlockSpec(memory_space=pl.ANY),
                  row_spec,
                  pl.BlockSpec((tb, TOP_K), lambda i: (i, 0)),
                  pl.BlockSpec((1, d), lambda i: (0, 0))],
        out_specs=out_specs,
        out_shape=out_shape,
        scratch_shapes=[pltpu.SMEM((2, TOP_K * tb), jnp.int32),
                        pltpu.VMEM((2, TOP_K * tb, d // 2), jnp.uint32),
                        pltpu.SemaphoreType.DMA(()),
                        pltpu.SemaphoreType.DMA((2,))],
        compiler_params=_params(("arbitrary",), nbytes),
        name="moe_combine",
    )(dest_blk, y, x2d, gates_t, gain2d)


def _dispatch_plan(counts, idx, rank, tm, t_max, tb):
    n_exp = counts.shape[0]
    n = idx.shape[1]
    tiles_e = (counts + tm - 1) // tm
    tile_end = jnp.cumsum(tiles_e)
    tile_start = tile_end - tiles_e
    n_used = tile_end[-1]
    expert = jnp.arange(n_exp, dtype=jnp.int32)[:, None, None]
    dest = rank + jnp.sum(jnp.where(idx[None] == expert, (tile_start * tm)[:, None, None], 0), axis=0)
    tile = jnp.arange(t_max, dtype=jnp.int32)
    tile_c = jnp.minimum(tile, n_used - 1)
    tile_e = jnp.minimum(jnp.searchsorted(tile_end, tile_c, side="right"), n_exp - 1).astype(jnp.int32)
    rows = jnp.clip(counts[tile_e] - (tile_c - tile_start[tile_e]) * tm, 0, tm)
    tile_rows = jnp.where(tile < n_used, rows, 0).astype(jnp.int32)
    dest_blk = dest.reshape(TOP_K, n // tb, tb).transpose(1, 0, 2).reshape(n // tb, TOP_K * tb)
    return dest_blk, tile_e, tile_rows, n_used.reshape(1).astype(jnp.int32)


def kernel(x, norm_mix, w_in, attn_rel_bias, attn_out_gain, conv_w, conv_b, lru_w_a, lru_b_a, lru_w_i, lru_b_i,
           lru_lambda, rec_out_gain, w_out, norm_ffn, router_w, router_b, expert_w_gate_up, expert_b_gate_up,
           expert_w_down, expert_b_down, norm_final):
    batch, seq, d = x.shape
    n = batch * seq
    depth = norm_mix.shape[0]
    attn_width = attn_out_gain.shape[1]
    lru_width = rec_out_gain.shape[1]
    n_exp = router_w.shape[-1]
    sub = min(512, n)
    tm = sub * min(3, n // sub)
    t_max = (n * TOP_K) // tm + n_exp
    tb = min(128, n)

    x2d = x.reshape(n, d)
    h = rmsnorm_bf16(x2d, norm_mix, 0)
    out = None
    for l in range(depth):
        qkv = matmul_ws([h], w_in, l, [0], 0, 3 * attn_width, bf16, name="in_proj_qkv")
        rg = matmul_ws([h], w_in, l, [0], 3 * attn_width, 2 * lru_width, f32, name="in_proj_rec")
        attn = attention(qkv, attn_rel_bias[l], attn_out_gain, l, batch, seq, attn_width)
        rec = rglru_block(rg, conv_w, conv_b, lru_w_a, lru_b_a, lru_w_i, lru_b_i, lru_lambda, rec_out_gain,
                          l, batch, seq)
        assert attn_width == lru_width
        x2d = matmul_ws([attn, rec], w_out, l, [0, 1], 0, d, f32, residual=x2d, name="out_proj")
        hp, idx, gates, rank, cnt = router(x2d, norm_ffn, router_w, router_b, l)
        dest_blk, tile_e, tile_rows, n_used = _dispatch_plan(cnt[:, 0], idx, rank, tm, t_max, tb)
        xs_sorted = dispatch(hp, dest_blk, t_max * tm)
        y = moe_experts(xs_sorted, tile_e, tile_rows, n_used, expert_w_gate_up, expert_b_gate_up,
                        expert_w_down, expert_b_down, l, tm, sub)
        final = l == depth - 1
        gain = norm_final.reshape(1, d) if final else norm_mix[l + 1].reshape(1, d)
        res = combine(x2d, y, dest_blk, gates.T, gain, final)
        if final:
            out = res
        else:
            x2d, h = res
    return out.reshape(batch, seq, d)
```
